```python
import math
import jax, jax.numpy as jnp
from jax import lax
import numpy as np

D_MODEL = 1024
BATCH = 4
SEQ = 8192
DEPTH = 2

CHUNK = 64
Q_BLOCK = 128
D_SSM = 512
SSM_GROUP = 16
N_GROUPS = D_SSM // SSM_GROUP
STATE = 64
N_HEADS = 8
HEAD_DIM = 64
D_ATTN = N_HEADS * HEAD_DIM
IDX_HEADS = 8
IDX_DIM = 32
TOPK_MAX = 256
IDX_SCALE = (IDX_HEADS ** -0.5) * (IDX_DIM ** -0.5)
ATTN_SCALE = HEAD_DIM ** -0.5
N_BUCKETS = 32
MAX_DIST = 128
D_FF = 2816
N_BRANCH = 2
IN_SIZES = (D_SSM, D_ATTN, D_ATTN, D_ATTN, IDX_HEADS * IDX_DIM, IDX_DIM, IDX_HEADS, D_MODEL, D_MODEL)
D_IN = sum(IN_SIZES)
ALPHA = (2 * DEPTH) ** 0.25
BETA = (8 * DEPTH) ** -0.25
LN_EPS = 1e-5

kernel_name = "hybrid_s5_dsa_macaron_deepnorm"


def layer_norm(x, g, b):
    x32 = x.astype(jnp.float32)
    mu = jnp.mean(x32, axis=-1, keepdims=True)
    var = jnp.mean(jnp.square(x32 - mu), axis=-1, keepdims=True)
    y = (x32 - mu) * lax.rsqrt(var + LN_EPS) * g.astype(jnp.float32) + b.astype(jnp.float32)
    return y.astype(x.dtype)


def swiglu_ffn(x, w_up, w_down):
    gate, up = jnp.split(x @ w_up, 2, axis=-1)
    return (jax.nn.silu(gate) * up) @ w_down


def t5_bucket(rel):
    half = N_BUCKETS // 2
    max_exact = half // 2
    ret = jnp.where(rel > 0, half, 0)
    n = jnp.abs(rel)
    n_f = jnp.maximum(n, 1).astype(jnp.float32)
    large = max_exact + (jnp.log(n_f / max_exact) / math.log(MAX_DIST / max_exact)
                         * (half - max_exact)).astype(jnp.int32)
    large = jnp.minimum(large, half - 1)
    return ret + jnp.where(n < max_exact, n, large)


def ssm_branch(u, lam_re, lam_im, log_dt, b_re, b_im, c_re, c_im, d_skip, w_glu):
    bsz, seq_len, _ = u.shape
    f32 = jnp.float32
    u32 = u.astype(f32).reshape(bsz, seq_len, N_GROUPS, SSM_GROUP)
    lam = lax.complex(jnp.minimum(lam_re.astype(f32), -1e-4), lam_im.astype(f32))
    dt = jnp.exp(log_dt.astype(f32))[:, None]
    a_bar = jnp.exp(lam * dt)
    b_bar = ((a_bar - 1.0) / lam)[:, :, None] * lax.complex(b_re.astype(f32), b_im.astype(f32))
    c_mat = lax.complex(c_re.astype(f32), c_im.astype(f32))
    bu = jnp.einsum('blgh,gph->blgp', u32.astype(jnp.complex64), b_bar)
    a_seq = jnp.broadcast_to(a_bar[None, None], (1, seq_len, N_GROUPS, STATE))

    def combine(e1, e2):
        a1, b1 = e1
        a2, b2 = e2
        return a1 * a2, a2 * b1 + b2

    _, states = lax.associative_scan(combine, (a_seq, bu), axis=1)
    y = jnp.einsum('blgp,ghp->blgh', states, c_mat).real \
        + d_skip.astype(f32).reshape(N_GROUPS, SSM_GROUP) * u32
    y = jax.nn.gelu(y.reshape(bsz, seq_len, D_SSM)).astype(u.dtype)
    return y * jax.nn.sigmoid(y @ w_glu)


def dsa_branch(q, k, v, q_idx, k_idx, w_idx, rel_bias):
    bsz, seq_len = q.shape[:2]
    n_sel = min(TOPK_MAX, seq_len // 4)
    nb = seq_len // Q_BLOCK
    key_chunk = jnp.arange(seq_len) // CHUNK
    k_idx32 = k_idx.astype(jnp.float32)
    bias_tab = rel_bias.astype(jnp.float32)

    def to_blocks(a):
        return a.reshape(bsz, nb, Q_BLOCK, *a.shape[2:]).swapaxes(0, 1)

    def block(args):
        qb, qib, wb, blk = args
        qpos = blk * Q_BLOCK + jnp.arange(Q_BLOCK)
        q_chunk = qpos // CHUNK
        admissible = key_chunk[None, :] <= q_chunk[:, None]
        s = jnp.einsum('bqhd,bsd->bqhs', qib.astype(jnp.float32), k_idx32)
        score = jnp.einsum('bqh,bqhs->bqs', wb.astype(jnp.float32), jax.nn.relu(s)) * IDX_SCALE
        score = jnp.where(admissible[None], score, -jnp.inf)
        _, sel = lax.top_k(score, n_sel)
        valid = (sel // CHUNK) <= q_chunk[None, :, None]
        k_sel = jax.vmap(lambda a, i: a[i])(k, sel)
        v_sel = jax.vmap(lambda a, i: a[i])(v, sel)
        logits = jnp.einsum('bqhd,bqkhd->bqkh', qb, k_sel).astype(jnp.float32) * ATTN_SCALE
        logits = logits + bias_tab[t5_bucket(sel - qpos[None, :, None])]
        logits = jnp.where(valid[..., None], logits, -1e30)
        p = jax.nn.softmax(logits, axis=2).astype(v.dtype)
        return jnp.einsum('bqkh,bqkhd->bqhd', p, v_sel)

    out = lax.map(block, (to_blocks(q), to_blocks(q_idx), to_blocks(w_idx), jnp.arange(nb)))
    return out.swapaxes(0, 1).reshape(bsz, seq_len, D_ATTN)


def hybrid_mixer(x, w_in, lam_re, lam_im, log_dt, b_re, b_im, c_re, c_im, d_skip, w_glu,
                 w_branch, w_out, rel_bias):
    bsz, seq_len, _ = x.shape
    splits = [int(s) for s in np.cumsum(IN_SIZES)[:-1]]
    u, q, k, v, qi, ki, wi, g_ssm, g_attn = jnp.split(x @ w_in, splits, axis=-1)
    heads = lambda a: a.reshape(bsz, seq_len, N_HEADS, HEAD_DIM)
    y_ssm = ssm_branch(u, lam_re, lam_im, log_dt, b_re, b_im, c_re, c_im, d_skip, w_glu)
    y_attn = dsa_branch(heads(q), heads(k), heads(v),
                        qi.reshape(bsz, seq_len, IDX_HEADS, IDX_DIM), ki, wi, rel_bias)
    merged = jax.nn.sigmoid(g_ssm) * (y_ssm @ w_branch[0]) \
        + jax.nn.sigmoid(g_attn) * (y_attn @ w_branch[1])
    return merged @ w_out


def setup_inputs(seed: int = 0) -> dict:
    key = jax.random.key(seed)
    ks = jax.random.split(key, 24)
    nrm = lambda k, shape, scale: jax.random.normal(k, shape, jnp.float32) * scale
    L_ = DEPTH
    lam_im0 = jnp.pi * jnp.arange(STATE, dtype=jnp.float32)
    return {
        "x": nrm(ks[0], (BATCH, SEQ, D_MODEL), 1.0),
        "ffn1_w_up": nrm(ks[1], (L_, D_MODEL, 2 * D_FF), D_MODEL ** -0.5),
        "ffn1_w_down": nrm(ks[2], (L_, D_FF, D_MODEL), BETA * D_FF ** -0.5),
        "ln1_g": 1.0 + nrm(ks[3], (L_, D_MODEL), 0.02),
        "ln1_b": nrm(ks[4], (L_, D_MODEL), 0.02),
        "w_in": nrm(ks[5], (L_, D_MODEL, D_IN), D_MODEL ** -0.5),
        "ssm_lam_re": -0.5 + nrm(ks[6], (L_, N_GROUPS, STATE), 0.01),
        "ssm_lam_im": lam_im0 + nrm(ks[7], (L_, N_GROUPS, STATE), 0.01),
        "ssm_log_dt": jax.random.uniform(ks[8], (L_, N_GROUPS), jnp.float32,
                                         math.log(1e-3), math.log(1e-1)),
        "ssm_b_re": nrm(ks[9], (L_, N_GROUPS, STATE, SSM_GROUP), (2 * SSM_GROUP) ** -0.5),
        "ssm_b_im": nrm(ks[10], (L_, N_GROUPS, STATE, SSM_GROUP), (2 * SSM_GROUP) ** -0.5),
        "ssm_c_re": nrm(ks[11], (L_, N_GROUPS, SSM_GROUP, STATE), STATE ** -0.5),
        "ssm_c_im": nrm(ks[12], (L_, N_GROUPS, SSM_GROUP, STATE), STATE ** -0.5),
        "ssm_d": nrm(ks[13], (L_, D_SSM), 1.0),
        "ssm_w_glu": nrm(ks[14], (L_, D_SSM, D_SSM), D_SSM ** -0.5),
        "w_branch": nrm(ks[15], (L_, N_BRANCH, D_SSM, D_MODEL), BETA * D_SSM ** -0.5),
        "w_out": nrm(ks[16], (L_, D_MODEL, D_MODEL), BETA * D_MODEL ** -0.5),
        "ln2_g": 1.0 + nrm(ks[17], (L_, D_MODEL), 0.02),
        "ln2_b": nrm(ks[18], (L_, D_MODEL), 0.02),
        "ffn2_w_up": nrm(ks[19], (L_, D_MODEL, 2 * D_FF), D_MODEL ** -0.5),
        "ffn2_w_down": nrm(ks[20], (L_, D_FF, D_MODEL), BETA * D_FF ** -0.5),
        "ln3_g": 1.0 + nrm(ks[21], (L_, D_MODEL), 0.02),
        "ln3_b": nrm(ks[22], (L_, D_MODEL), 0.02),
        "rel_bias": nrm(ks[23], (N_BUCKETS, N_HEADS), 0.5),
    }


def reference(x, ffn1_w_up, ffn1_w_down, ln1_g, ln1_b, w_in, ssm_lam_re, ssm_lam_im, ssm_log_dt,
              ssm_b_re, ssm_b_im, ssm_c_re, ssm_c_im, ssm_d, ssm_w_glu, w_branch, w_out,
              ln2_g, ln2_b, ffn2_w_up, ffn2_w_down, ln3_g, ln3_b, rel_bias):
    for l in range(DEPTH):
        x = layer_norm(ALPHA * x + 0.5 * swiglu_ffn(x, ffn1_w_up[l], ffn1_w_down[l]), ln1_g[l], ln1_b[l])
        mix = hybrid_mixer(x, w_in[l], ssm_lam_re[l], ssm_lam_im[l], ssm_log_dt[l], ssm_b_re[l],
                           ssm_b_im[l], ssm_c_re[l], ssm_c_im[l], ssm_d[l], ssm_w_glu[l],
                           w_branch[l], w_out[l], rel_bias)
        x = layer_norm(ALPHA * x + mix, ln2_g[l], ln2_b[l])
        x = layer_norm(ALPHA * x + 0.5 * swiglu_ffn(x, ffn2_w_up[l], ffn2_w_down[l]), ln3_g[l], ln3_b[l])
    return x
```

```python
import functools
import math

import numpy as np
import jax
import jax.numpy as jnp
from jax import lax
from jax.experimental import pallas as pl
from jax.experimental.pallas import tpu as pltpu

F32 = jnp.float32
BF16 = jnp.bfloat16
I32 = jnp.int32

DEPTH = 2
CHUNK = 64
SSM_GROUP = 16
STATE = 64
N_HEADS = 8
HEAD_DIM = 64
IDX_HEADS = 8
IDX_DIM = 32
TOPK_MAX = 256
ATTN_SCALE = HEAD_DIM ** -0.5
N_BUCKETS = 32
MAX_DIST = 128
ALPHA = (2 * DEPTH) ** 0.25
LN_EPS = 1e-5

V7X_VMEM_LIMIT_BYTES = 58 * 1024 * 1024
FFN_TM = 512
FFN_FC = 256
PROJ_TM = 512
MERGE_TM = 512
SSM_T = 32
DSA_TQ = 256
DSA_TK = 256
MASK_NEG = -1e30
INT_MIN = -2 ** 31
KEY_NEG_INF = int(np.int32(np.uint32(0x807FFFFF)))


def _resident(shape):
    nd = len(shape)
    return pl.BlockSpec(shape, lambda *_: (0,) * nd, pipeline_mode=pl.Buffered(1))


def _layer_norm(y, g, b):
    mu = jnp.mean(y, axis=-1, keepdims=True)
    yc = y - mu
    var = jnp.mean(yc * yc, axis=-1, keepdims=True)
    return yc * lax.rsqrt(var + LN_EPS) * g + b


def _ffn_ln_kernel(x_ref, wup_ref, wdn_ref, g_ref, b_ref, o_ref, *, d_ff, fc):
    x = x_ref[...]
    xb = x.astype(BF16)
    acc = jnp.zeros(x.shape, F32)
    for j in range(d_ff // fc):
        gate = jnp.dot(xb, wup_ref[:, j * fc:(j + 1) * fc], preferred_element_type=F32)
        up = jnp.dot(xb, wup_ref[:, d_ff + j * fc:d_ff + (j + 1) * fc], preferred_element_type=F32)
        a = (gate * jax.nn.sigmoid(gate)) * up
        acc = acc + jnp.dot(a.astype(BF16), wdn_ref[j * fc:(j + 1) * fc, :], preferred_element_type=F32)
    o_ref[...] = _layer_norm(ALPHA * x + 0.5 * acc, g_ref[...], b_ref[...])


def _ffn_ln(x2, w_up, w_down, g, b):
    n, d = x2.shape
    d_ff = w_down.shape[0]
    tm = min(FFN_TM, n)
    assert n % tm == 0 and d_ff % FFN_FC == 0
    return pl.pallas_call(
        functools.partial(_ffn_ln_kernel, d_ff=d_ff, fc=FFN_FC),
        out_shape=jax.ShapeDtypeStruct((n, d), F32),
        grid=(n // tm,),
        in_specs=[pl.BlockSpec((tm, d), lambda i: (i, 0)),
                  _resident(w_up.shape), _resident(w_down.shape),
                  _resident((1, d)), _resident((1, d))],
        out_specs=pl.BlockSpec((tm, d), lambda i: (i, 0)),
        compiler_params=pltpu.CompilerParams(dimension_semantics=("parallel",),
                                             vmem_limit_bytes=V7X_VMEM_LIMIT_BYTES),
        name="ffn_ln",
    )(x2, w_up, w_down, g.reshape(1, d), b.reshape(1, d))


def _in_proj_kernel(x_ref, w_ref, u_ref, q_ref, k_ref, v_ref, qi_ref, kw_ref, gs_ref, ga_ref, *, offs):
    xb = x_ref[...].astype(BF16)

    def proj(name):
        lo, hi = offs[name]
        return jnp.dot(xb, w_ref[:, lo:hi], preferred_element_type=F32)

    u_ref[...] = proj("u")
    q_ref[...] = (proj("q") * ATTN_SCALE).astype(BF16)
    k_ref[...] = proj("k").astype(BF16)
    v_ref[...] = proj("v").astype(BF16)
    qi_ref[...] = proj("qi").astype(BF16)
    kw_ref[...] = proj("kw")
    gs_ref[...] = jax.nn.sigmoid(proj("gs")).astype(BF16)
    ga_ref[...] = jax.nn.sigmoid(proj("ga")).astype(BF16)


def _in_proj(x2, w_cat, offs):
    n, d = x2.shape
    tm = min(PROJ_TM, n)
    assert n % tm == 0
    widths = {k: hi - lo for k, (lo, hi) in offs.items()}
    names = ("u", "q", "k", "v", "qi", "kw", "gs", "ga")
    dtypes = dict(u=F32, q=BF16, k=BF16, v=BF16, qi=BF16, kw=F32, gs=BF16, ga=BF16)
    return pl.pallas_call(
        functools.partial(_in_proj_kernel, offs=offs),
        out_shape=[jax.ShapeDtypeStruct((n, widths[k]), dtypes[k]) for k in names],
        grid=(n // tm,),
        in_specs=[pl.BlockSpec((tm, d), lambda i: (i, 0)), _resident(w_cat.shape)],
        out_specs=[pl.BlockSpec((tm, widths[k]), lambda i: (i, 0)) for k in names],
        compiler_params=pltpu.CompilerParams(dimension_semantics=("parallel",),
                                             vmem_limit_bytes=V7X_VMEM_LIMIT_BYTES),
        name="in_proj",
    )(x2, w_cat)


def _ssm_kernel(u_ref, m_ref, wre_ref, wim_ref, v_ref, at_ref, d_ref, y_ref, hin_scr, *, n_chunks, bsz):
    u = u_ref[0]
    ub = u.astype(BF16)
    s_re = jnp.dot(ub, wre_ref[0], preferred_element_type=F32)
    s_im = jnp.dot(ub, wim_ref[0], preferred_element_type=F32)
    a_re = at_ref[0, 0:1, :]
    a_im = at_ref[0, 1:2, :]
    h_re = jnp.zeros((bsz, STATE), F32)
    h_im = jnp.zeros((bsz, STATE), F32)
    for c in range(n_chunks):
        rows = slice(c * bsz, (c + 1) * bsz)
        hin_scr[rows, 0:STATE] = h_re
        hin_scr[rows, STATE:2 * STATE] = h_im
        n_re = a_re * h_re - a_im * h_im + s_re[rows]
        n_im = a_re * h_im + a_im * h_re + s_im[rows]
        h_re, h_im = n_re, n_im
    y = jnp.dot(ub, m_ref[0], preferred_element_type=F32)
    y = y + jnp.dot(hin_scr[...].astype(BF16), v_ref[0], preferred_element_type=F32)
    y_ref[0] = y + d_ref[0] * u


def _ssm_tables(lam_re, lam_im, log_dt, b_re, b_im, c_re, c_im, d_skip, t_len):
    g = lam_re.shape[0]
    lam = lax.complex(jnp.minimum(lam_re.astype(F32), -1e-4), lam_im.astype(F32))
    dt = jnp.exp(log_dt.astype(F32))[:, None]
    a_bar = jnp.exp(lam * dt)
    b_bar = ((a_bar - 1.0) / lam)[:, :, None] * lax.complex(b_re.astype(F32), b_im.astype(F32))
    c_mat = lax.complex(c_re.astype(F32), c_im.astype(F32))
    taus = jnp.arange(t_len + 1, dtype=F32)
    apow = jnp.exp((lam * dt)[:, None, :] * taus[None, :, None])
    hp = lax.Precision.HIGHEST
    kern = jnp.einsum("ghp,gtp,gpk->gthk", c_mat, apow[:, :t_len], b_bar, precision=hp).real
    s_idx = jnp.arange(t_len)[:, None]
    t_idx = jnp.arange(t_len)[None, :]
    lag = t_idx - s_idx
    toep = kern[:, jnp.clip(lag, 0, t_len - 1)]
    toep = jnp.where((lag >= 0)[None, :, :, None, None], toep, 0.0)
    m_tab = toep.transpose(0, 1, 4, 2, 3).reshape(g, t_len * SSM_GROUP, t_len * SSM_GROUP)
    w_c = apow[:, :t_len][:, ::-1, None, :] * b_bar.transpose(0, 2, 1)[:, None, :, :]
    w_c = w_c.reshape(g, t_len * SSM_GROUP, STATE)
    v_c = c_mat.transpose(0, 2, 1)[:, :, None, :] * apow[:, 1:t_len + 1].transpose(0, 2, 1)[:, :, :, None]
    v_c = v_c.reshape(g, STATE, t_len * SSM_GROUP)
    v_tab = jnp.concatenate([v_c.real, -v_c.imag], axis=1)
    a_t = jnp.stack([apow[:, t_len].real, apow[:, t_len].imag], axis=1)
    d_tab = jnp.tile(d_skip.astype(F32).reshape(g, 1, SSM_GROUP), (1, t_len, 1)).reshape(g, 1, t_len * SSM_GROUP)
    return (m_tab.astype(BF16), w_c.real.astype(BF16), w_c.imag.astype(BF16), v_tab.astype(BF16), a_t, d_tab)


def _ssm(u, tables, bsz, seq):
    m_tab, w_re, w_im, v_tab, a_t, d_tab = tables
    g = m_tab.shape[0]
    t_len = SSM_T
    assert seq % t_len == 0
    n_chunks = seq // t_len
    rows, cols = n_chunks * bsz, t_len * SSM_GROUP
    u_t = u.reshape(bsz, n_chunks, t_len, g, SSM_GROUP).transpose(3, 1, 0, 2, 4).reshape(g, rows, cols)
    per_group = lambda shape: pl.BlockSpec((1,) + shape, lambda i: (i, 0, 0))
    y_t = pl.pallas_call(
        functools.partial(_ssm_kernel, n_chunks=n_chunks, bsz=bsz),
        out_shape=jax.ShapeDtypeStruct((g, rows, cols), F32),
        grid=(g,),
        in_specs=[per_group((rows, cols)), per_group((cols, cols)), per_group((cols, STATE)),
                  per_group((cols, STATE)), per_group((2 * STATE, cols)), per_group((2, STATE)),
                  per_group((1, cols))],
        out_specs=per_group((rows, cols)),
        scratch_shapes=[pltpu.VMEM((rows, 2 * STATE), F32)],
        compiler_params=pltpu.CompilerParams(dimension_semantics=("parallel",),
                                             vmem_limit_bytes=V7X_VMEM_LIMIT_BYTES),
        name="ssm",
    )(u_t, m_tab, w_re, w_im, v_tab, a_t, d_tab)
    return y_t.reshape(g, n_chunks, bsz, t_len, SSM_GROUP).transpose(2, 1, 3, 0, 4).reshape(bsz * seq, g * SSM_GROUP)


def _t5_bucket_np(rel):
    half = N_BUCKETS // 2
    max_exact = half // 2
    ret = np.where(rel > 0, half, 0)
    n = np.abs(rel)
    n_f = np.maximum(n, 1).astype(np.float32)
    large = max_exact + (np.log(n_f / np.float32(max_exact)) / np.float32(math.log(MAX_DIST / max_exact))
                         * (half - max_exact)).astype(np.int32)
    large = np.minimum(large, half - 1)
    return ret + np.where(n < max_exact, n, large)


def _near_bias_tables(rel_bias, tq, tk):
    kk = np.arange(tk)[:, None]
    qq = np.arange(tq)[None, :]
    b_diag = _t5_bucket_np(kk - qq)
    b_prev = _t5_bucket_np(kk - qq - tk)
    far = int(_t5_bucket_np(np.array(-(tk + 1))))
    assert far == int(_t5_bucket_np(np.array(-(1 << 24)))), "far keys must share one bucket"
    tab = rel_bias.astype(F32)
    near = jnp.stack([tab[b_diag], tab[b_prev]], axis=0) - tab[far][None, None, None, :]
    return near.transpose(3, 0, 1, 2)


def _sortable(x):
    b = pltpu.bitcast(x, I32)
    return b ^ ((b >> 31) & 0x7FFFFFFF)


def _dsa_kernel(qi_ref, wi_ref, q_ref, kidx_ref, k_ref, vt_ref, near_ref, o_ref,
                key_scr, mb_scr, acc_scr, *, tq, tk, n_sel):
    i = pl.program_id(1)
    n_tiles = i + 1
    nt_dims = (((1,), (1,)), ((), ()))

    def score_tile(t):
        kid = kidx_ref[pl.ds(pl.multiple_of(t * tk, tk), tk), :]
        acc = jnp.zeros((tk, tq), F32)
        for h in range(IDX_HEADS):
            s = lax.dot_general(kid, qi_ref[h], nt_dims, preferred_element_type=F32)
            acc = acc + wi_ref[h:h + 1, :] * jnp.maximum(s, 0.0)
        return acc

    def score_body(t, carry):
        key_scr[pl.ds(pl.multiple_of(t * tk, tk), tk), :] = _sortable(score_tile(t) + 0.0)
        return carry

    lax.fori_loop(0, i, score_body, 0)
    kpos = lax.broadcasted_iota(I32, (tk, tq), 0)
    qpos = lax.broadcasted_iota(I32, (tk, tq), 1)
    admissible = (kpos // CHUNK) <= (qpos // CHUNK)
    diag = jnp.where(admissible, score_tile(i) + 0.0, -jnp.inf)
    key_scr[pl.ds(pl.multiple_of(i * tk, tk), tk), :] = _sortable(diag)

    def count(pred_fn):
        def body(t, c):
            kt = key_scr[pl.ds(pl.multiple_of(t * tk, tk), tk), :]
            hit = jnp.where(pred_fn(kt), 1, 0).astype(I32)
            return c + hit.reshape(tk // 8, 8, tq).sum(axis=0)
        part = lax.fori_loop(0, n_tiles, body, jnp.zeros((8, tq), I32))
        return jnp.sum(part, axis=0, keepdims=True)

    q_in_blk = lax.broadcasted_iota(I32, (1, tq), 1)
    n_adm = (i * tq + q_in_blk) // CHUNK * CHUNK + CHUNK
    take_all = jnp.where(n_adm <= n_sel, 1, 0).astype(I32)

    def radix_cond(st):
        b, _, done, _ = st
        return jnp.logical_and(b >= 0, jnp.min(done) == 0)

    def radix_body(st):
        b, prefix, done, thr = st
        cand_u = prefix | jnp.left_shift(jnp.int32(1), b)
        cand_s = cand_u ^ INT_MIN
        cnt = count(lambda kt: kt >= cand_s)
        live = done == 0
        prefix = jnp.where(jnp.logical_and(live, cnt >= n_sel), cand_u, prefix)
        hit = jnp.logical_and(live, cnt == n_sel)
        thr = jnp.where(hit, cand_s, thr)
        done = jnp.where(hit, 1, done)
        return b - 1, prefix, done, thr

    _, prefix, done, thr = lax.while_loop(
        radix_cond, radix_body,
        (jnp.int32(31), jnp.zeros((1, tq), I32), take_all, jnp.full((1, tq), KEY_NEG_INF, I32)))
    thr = jnp.where(done == 1, thr, prefix ^ INT_MIN)
    n_gt = count(lambda kt: kt > thr)
    need = jnp.where(take_all == 1, 0, n_sel - n_gt).astype(F32)

    r_i = lax.broadcasted_iota(I32, (tk, tk), 0)
    c_i = lax.broadcasted_iota(I32, (tk, tk), 1)
    strict_lower = jnp.where(c_i < r_i, 1.0, 0.0).astype(BF16)

    def mask_body(t, ties_before):
        rows = pl.ds(pl.multiple_of(t * tk, tk), tk)
        kt = key_scr[rows, :]
        eq = kt == thr
        eq_f = jnp.where(eq, 1.0, 0.0)
        rank = jnp.dot(strict_lower, eq_f.astype(BF16), preferred_element_type=F32) + ties_before
        sel = jnp.logical_or(kt > thr, jnp.logical_and(eq, rank < need))
        mb_scr[rows, :] = jnp.where(sel, 0.0, MASK_NEG)
        return ties_before + jnp.sum(eq_f.reshape(tk // 8, 8, tq).sum(axis=0), axis=0, keepdims=True)

    lax.fori_loop(0, n_tiles, mask_body, jnp.zeros((1, tq), F32))

    lane = lax.broadcasted_iota(I32, (tq, 2 * HEAD_DIM), 1)
    for h in range(N_HEADS):
        pair = slice((h // 2) * 2 * HEAD_DIM, (h // 2 + 1) * 2 * HEAD_DIM)
        q_pair = q_ref[:, pair]
        own = (lane < HEAD_DIM) if h % 2 == 0 else (lane >= HEAD_DIM)
        q_h = jnp.where(own, q_pair, jnp.zeros_like(q_pair))

        def attn_body(t, carry, h=h, pair=pair, q_h=q_h):
            m, l, acc = carry
            rows = pl.ds(pl.multiple_of(t * tk, tk), tk)
            lg = lax.dot_general(k_ref[rows, pair], q_h, nt_dims, preferred_element_type=F32)
            lg = lg + mb_scr[rows, :]
            lg = lax.cond(t >= i - 1, lambda x: x + near_ref[h, i - t], lambda x: x, lg)
            m_new = jnp.maximum(m, jnp.max(lg, axis=0, keepdims=True))
            alpha = jnp.exp(m - m_new)
            p = jnp.exp(lg - m_new)
            l = alpha * l + jnp.sum(p, axis=0, keepdims=True)
            pv = jnp.dot(vt_ref[h * HEAD_DIM:(h + 1) * HEAD_DIM, rows], p.astype(BF16),
                         preferred_element_type=F32)
            return m_new, l, alpha * acc + pv

        init = (jnp.full((1, tq), MASK_NEG, F32), jnp.zeros((1, tq), F32), jnp.zeros((HEAD_DIM, tq), F32))
        _, l, acc = lax.fori_loop(0, n_tiles, attn_body, init)
        acc_scr[h * HEAD_DIM:(h + 1) * HEAD_DIM, :] = acc / l
    o_ref[0] = acc_scr[...].T


def _dsa(q, k, v, qi, kw, rel_bias, bsz, seq):
    d_attn = q.shape[1]
    tq, tk = DSA_TQ, DSA_TK
    assert tq == tk and seq % tq == 0 and tq % CHUNK == 0
    n_sel = min(TOPK_MAX, seq // 4)
    q3 = q.reshape(bsz, seq, d_attn)
    k3 = k.reshape(bsz, seq, d_attn)
    vt = v.reshape(bsz, seq, d_attn).transpose(0, 2, 1)
    qi4 = qi.reshape(bsz, seq, IDX_HEADS, IDX_DIM).transpose(0, 2, 1, 3)
    kidx = kw[:, :IDX_DIM].astype(BF16).reshape(bsz, seq, IDX_DIM)
    wi_t = kw[:, IDX_DIM:IDX_DIM + IDX_HEADS].reshape(bsz, seq, IDX_HEADS).transpose(0, 2, 1)
    near = _near_bias_tables(rel_bias, tq, tk)
    per_batch = lambda shape: pl.BlockSpec((None,) + shape, lambda b, i: (b,) + (0,) * len(shape),
                                           pipeline_mode=pl.Buffered(1))
    return pl.pallas_call(
        functools.partial(_dsa_kernel, tq=tq, tk=tk, n_sel=n_sel),
        out_shape=jax.ShapeDtypeStruct((bsz, seq, d_attn), F32),
        grid=(bsz, seq // tq),
        in_specs=[pl.BlockSpec((None, IDX_HEADS, tq, IDX_DIM), lambda b, i: (b, 0, i, 0)),
                  pl.BlockSpec((None, IDX_HEADS, tq), lambda b, i: (b, 0, i)),
                  pl.BlockSpec((None, tq, d_attn), lambda b, i: (b, i, 0)),
                  per_batch((seq, IDX_DIM)), per_batch((seq, d_attn)), per_batch((d_attn, seq)),
                  _resident(near.shape)],
        out_specs=pl.BlockSpec((1, tq, d_attn), lambda b, i: (b, i, 0)),
        scratch_shapes=[pltpu.VMEM((seq, tq), I32), pltpu.VMEM((seq, tq), F32), pltpu.VMEM((d_attn, tq), F32)],
        compiler_params=pltpu.CompilerParams(dimension_semantics=("parallel", "arbitrary"),
                                             vmem_limit_bytes=V7X_VMEM_LIMIT_BYTES),
        name="dsa",
    )(qi4, wi_t, q3, kidx, k3, vt, near).reshape(bsz * seq, d_attn)


def _merge_ln_kernel(x_ref, ys_ref, ya_ref, gs_ref, ga_ref, wglu_ref, wb0_ref, wb1_ref, wout_ref,
                     g_ref, b_ref, o_ref):
    y = jax.nn.gelu(ys_ref[...])
    glu = y * jax.nn.sigmoid(jnp.dot(y.astype(BF16), wglu_ref[...], preferred_element_type=F32))
    p_ssm = jnp.dot(glu.astype(BF16), wb0_ref[...], preferred_element_type=F32)
    p_att = jnp.dot(ya_ref[...].astype(BF16), wb1_ref[...], preferred_element_type=F32)
    merged = gs_ref[...].astype(F32) * p_ssm + ga_ref[...].astype(F32) * p_att
    mix = jnp.dot(merged.astype(BF16), wout_ref[...], preferred_element_type=F32)
    o_ref[...] = _layer_norm(ALPHA * x_ref[...] + mix, g_ref[...], b_ref[...])


def _merge_ln(x2, y_ssm, y_attn, gs, ga, w_glu, wb0, wb1, w_out, g, b):
    n, d = x2.shape
    tm = min(MERGE_TM, n)
    assert n % tm == 0
    tile = lambda a: pl.BlockSpec((tm, a.shape[1]), lambda i: (i, 0))
    return pl.pallas_call(
        _merge_ln_kernel,
        out_shape=jax.ShapeDtypeStruct((n, d), F32),
        grid=(n // tm,),
        in_specs=[tile(x2), tile(y_ssm), tile(y_attn), tile(gs), tile(ga),
                  _resident(w_glu.shape), _resident(wb0.shape), _resident(wb1.shape), _resident(w_out.shape),
                  _resident((1, d)), _resident((1, d))],
        out_specs=pl.BlockSpec((tm, d), lambda i: (i, 0)),
        compiler_params=pltpu.CompilerParams(dimension_semantics=("parallel",),
                                             vmem_limit_bytes=V7X_VMEM_LIMIT_BYTES),
        name="merge_ln",
    )(x2, y_ssm, y_attn, gs, ga, w_glu, wb0, wb1, w_out, g.reshape(1, d), b.reshape(1, d))


def _split_w_in(w_in, d_model, d_ssm, d_attn):
    sizes = (d_ssm, d_attn, d_attn, d_attn, IDX_HEADS * IDX_DIM, IDX_DIM, IDX_HEADS, d_model, d_model)
    assert w_in.shape[1] == sum(sizes)
    edges = np.concatenate([[0], np.cumsum(sizes)])
    seg = [w_in[:, int(edges[j]):int(edges[j + 1])] for j in range(len(sizes))]
    kw = jnp.concatenate([seg[5], seg[6], jnp.zeros((w_in.shape[0], 128 - IDX_DIM - IDX_HEADS), w_in.dtype)], axis=1)
    parts = dict(u=seg[0], q=seg[1], k=seg[2], v=seg[3], qi=seg[4], kw=kw, gs=seg[7], ga=seg[8])
    offs, cols, pos = {}, [], 0
    for name, w in parts.items():
        offs[name] = (pos, pos + w.shape[1])
        pos += w.shape[1]
        cols.append(w)
    return jnp.concatenate(cols, axis=1).astype(BF16), offs


def kernel(x, ffn1_w_up, ffn1_w_down, ln1_g, ln1_b, w_in, ssm_lam_re, ssm_lam_im, ssm_log_dt, ssm_b_re, ssm_b_im, ssm_c_re, ssm_c_im, ssm_d, ssm_w_glu, w_branch, w_out, ln2_g, ln2_b, ffn2_w_up, ffn2_w_down, ln3_g, ln3_b, rel_bias):
    bsz, seq, d_model = x.shape
    d_ssm = ssm_w_glu.shape[1]
    d_attn = w_branch.shape[2]
    x2 = x.reshape(bsz * seq, d_model)
    for l in range(ffn1_w_up.shape[0]):
        x2 = _ffn_ln(x2, ffn1_w_up[l].astype(BF16), ffn1_w_down[l].astype(BF16), ln1_g[l], ln1_b[l])
        w_cat, offs = _split_w_in(w_in[l], d_model, d_ssm, d_attn)
        u, q, k, v, qi, kw, gs, ga = _in_proj(x2, w_cat, offs)
        tables = _ssm_tables(ssm_lam_re[l], ssm_lam_im[l], ssm_log_dt[l], ssm_b_re[l], ssm_b_im[l],
                             ssm_c_re[l], ssm_c_im[l], ssm_d[l], SSM_T)
        y_ssm = _ssm(u, tables, bsz, seq)
        y_attn = _dsa(q, k, v, qi, kw, rel_bias, bsz, seq)
        x2 = _merge_ln(x2, y_ssm, y_attn, gs, ga, ssm_w_glu[l].astype(BF16), w_branch[l, 0].astype(BF16),
                       w_branch[l, 1].astype(BF16), w_out[l].astype(BF16), ln2_g[l], ln2_b[l])
        x2 = _ffn_ln(x2, ffn2_w_up[l].astype(BF16), ffn2_w_down[l].astype(BF16), ln3_g[l], ln3_b[l])
    return x2.reshape(bsz, seq, d_model)
```

```python
import functools
import math

import numpy as np
import jax
import jax.numpy as jnp
from jax import lax
from jax.experimental import pallas as pl
from jax.experimental.pallas import tpu as pltpu

F32 = jnp.float32
BF16 = jnp.bfloat16
I32 = jnp.int32

DEPTH = 2
CHUNK = 64
SSM_GROUP = 16
STATE = 64
N_HEADS = 8
HEAD_DIM = 64
IDX_HEADS = 8
IDX_DIM = 32
TOPK_MAX = 256
ATTN_SCALE = HEAD_DIM ** -0.5
LOG2E = math.log2(math.e)
N_BUCKETS = 32
MAX_DIST = 128
ALPHA = (2 * DEPTH) ** 0.25
LN_EPS = 1e-5

V7X_VMEM_LIMIT_BYTES = 58 * 1024 * 1024
FFN_TM = 512
FFN_FC = 256
PROJ_TM = 512
MERGE_TM = 512
SSM_T = 32
DSA_TQ = 256
DSA_TK = 256
MASK_NEG = -1e30
V_ROWS = HEAD_DIM + 16
INT_MIN = -2 ** 31
KEY_NEG_INF = int(np.int32(np.uint32(0x807FFFFF)))


def _resident(shape):
    nd = len(shape)
    return pl.BlockSpec(shape, lambda *_: (0,) * nd, pipeline_mode=pl.Buffered(1))


def _layer_norm(y, g, b):
    mu = jnp.mean(y, axis=-1, keepdims=True)
    yc = y - mu
    var = jnp.mean(yc * yc, axis=-1, keepdims=True)
    return yc * lax.rsqrt(var + LN_EPS) * g + b


def _ffn_ln_kernel(x_ref, wup_ref, wdn_ref, g_ref, b_ref, o_ref, *, d_ff, fc):
    x = x_ref[...]
    xb = x.astype(BF16)
    acc = jnp.zeros(x.shape, F32)
    for j in range(d_ff // fc):
        gate = jnp.dot(xb, wup_ref[:, j * fc:(j + 1) * fc], preferred_element_type=F32)
        up = jnp.dot(xb, wup_ref[:, d_ff + j * fc:d_ff + (j + 1) * fc], preferred_element_type=F32)
        a = (gate * jax.nn.sigmoid(gate)) * up
        acc = acc + jnp.dot(a.astype(BF16), wdn_ref[j * fc:(j + 1) * fc, :], preferred_element_type=F32)
    o_ref[...] = _layer_norm(ALPHA * x + 0.5 * acc, g_ref[...], b_ref[...])


def _ffn_ln(x2, w_up, w_down, g, b):
    n, d = x2.shape
    d_ff = w_down.shape[0]
    tm = min(FFN_TM, n)
    assert n % tm == 0 and d_ff % FFN_FC == 0
    return pl.pallas_call(
        functools.partial(_ffn_ln_kernel, d_ff=d_ff, fc=FFN_FC),
        out_shape=jax.ShapeDtypeStruct((n, d), F32),
        grid=(n // tm,),
        in_specs=[pl.BlockSpec((tm, d), lambda i: (i, 0)),
                  _resident(w_up.shape), _resident(w_down.shape),
                  _resident((1, d)), _resident((1, d))],
        out_specs=pl.BlockSpec((tm, d), lambda i: (i, 0)),
        compiler_params=pltpu.CompilerParams(dimension_semantics=("parallel",),
                                             vmem_limit_bytes=V7X_VMEM_LIMIT_BYTES),
        name="ffn_ln",
    )(x2, w_up, w_down, g.reshape(1, d), b.reshape(1, d))


def _in_proj_kernel(x_ref, w_ref, u_ref, q_ref, k_ref, v_ref, qi_ref, kw_ref, gs_ref, ga_ref, *, offs):
    xb = x_ref[...].astype(BF16)

    def proj(name):
        lo, hi = offs[name]
        return jnp.dot(xb, w_ref[:, lo:hi], preferred_element_type=F32)

    u_ref[...] = proj("u")
    q_ref[...] = (proj("q") * (ATTN_SCALE * LOG2E)).astype(BF16)
    k_ref[...] = proj("k").astype(BF16)
    v_ref[...] = proj("v").astype(BF16)
    qi_ref[...] = proj("qi").astype(BF16)
    kw_ref[...] = proj("kw")
    gs_ref[...] = jax.nn.sigmoid(proj("gs")).astype(BF16)
    ga_ref[...] = jax.nn.sigmoid(proj("ga")).astype(BF16)


def _in_proj(x2, w_cat, offs):
    n, d = x2.shape
    tm = min(PROJ_TM, n)
    assert n % tm == 0
    widths = {k: hi - lo for k, (lo, hi) in offs.items()}
    names = ("u", "q", "k", "v", "qi", "kw", "gs", "ga")
    dtypes = dict(u=F32, q=BF16, k=BF16, v=BF16, qi=BF16, kw=F32, gs=BF16, ga=BF16)
    return pl.pallas_call(
        functools.partial(_in_proj_kernel, offs=offs),
        out_shape=[jax.ShapeDtypeStruct((n, widths[k]), dtypes[k]) for k in names],
        grid=(n // tm,),
        in_specs=[pl.BlockSpec((tm, d), lambda i: (i, 0)), _resident(w_cat.shape)],
        out_specs=[pl.BlockSpec((tm, widths[k]), lambda i: (i, 0)) for k in names],
        compiler_params=pltpu.CompilerParams(dimension_semantics=("parallel",),
                                             vmem_limit_bytes=V7X_VMEM_LIMIT_BYTES),
        name="in_proj",
    )(x2, w_cat)


def _ssm_kernel(u_ref, m_ref, wre_ref, wim_ref, v_ref, at_ref, d_ref, y_ref, hin_scr, *, n_chunks, bsz):
    u = u_ref[0]
    ub = u.astype(BF16)
    s_re = jnp.dot(ub, wre_ref[0], preferred_element_type=F32)
    s_im = jnp.dot(ub, wim_ref[0], preferred_element_type=F32)
    a_re = at_ref[0, 0:1, :]
    a_im = at_ref[0, 1:2, :]
    h_re = jnp.zeros((bsz, STATE), F32)
    h_im = jnp.zeros((bsz, STATE), F32)
    for c in range(n_chunks):
        rows = slice(c * bsz, (c + 1) * bsz)
        hin_scr[rows, 0:STATE] = h_re
        hin_scr[rows, STATE:2 * STATE] = h_im
        n_re = a_re * h_re - a_im * h_im + s_re[rows]
        n_im = a_re * h_im + a_im * h_re + s_im[rows]
        h_re, h_im = n_re, n_im
    y = jnp.dot(ub, m_ref[0], preferred_element_type=F32)
    y = y + jnp.dot(hin_scr[...].astype(BF16), v_ref[0], preferred_element_type=F32)
    y_ref[0] = y + d_ref[0] * u


def _ssm_tables(lam_re, lam_im, log_dt, b_re, b_im, c_re, c_im, d_skip, t_len):
    g = lam_re.shape[0]
    lam = lax.complex(jnp.minimum(lam_re.astype(F32), -1e-4), lam_im.astype(F32))
    dt = jnp.exp(log_dt.astype(F32))[:, None]
    a_bar = jnp.exp(lam * dt)
    b_bar = ((a_bar - 1.0) / lam)[:, :, None] * lax.complex(b_re.astype(F32), b_im.astype(F32))
    c_mat = lax.complex(c_re.astype(F32), c_im.astype(F32))
    taus = jnp.arange(t_len + 1, dtype=F32)
    apow = jnp.exp((lam * dt)[:, None, :] * taus[None, :, None])
    hp = lax.Precision.HIGHEST
    kern = jnp.einsum("ghp,gtp,gpk->gthk", c_mat, apow[:, :t_len], b_bar, precision=hp).real
    s_idx = jnp.arange(t_len)[:, None]
    t_idx = jnp.arange(t_len)[None, :]
    lag = t_idx - s_idx
    toep = kern[:, jnp.clip(lag, 0, t_len - 1)]
    toep = jnp.where((lag >= 0)[None, :, :, None, None], toep, 0.0)
    m_tab = toep.transpose(0, 1, 4, 2, 3).reshape(g, t_len * SSM_GROUP, t_len * SSM_GROUP)
    w_c = apow[:, :t_len][:, ::-1, None, :] * b_bar.transpose(0, 2, 1)[:, None, :, :]
    w_c = w_c.reshape(g, t_len * SSM_GROUP, STATE)
    v_c = c_mat.transpose(0, 2, 1)[:, :, None, :] * apow[:, 1:t_len + 1].transpose(0, 2, 1)[:, :, :, None]
    v_c = v_c.reshape(g, STATE, t_len * SSM_GROUP)
    v_tab = jnp.concatenate([v_c.real, -v_c.imag], axis=1)
    a_t = jnp.stack([apow[:, t_len].real, apow[:, t_len].imag], axis=1)
    d_tab = jnp.tile(d_skip.astype(F32).reshape(g, 1, SSM_GROUP), (1, t_len, 1)).reshape(g, 1, t_len * SSM_GROUP)
    return (m_tab.astype(BF16), w_c.real.astype(BF16), w_c.imag.astype(BF16), v_tab.astype(BF16), a_t, d_tab)


def _ssm(u, tables, bsz, seq):
    m_tab, w_re, w_im, v_tab, a_t, d_tab = tables
    g = m_tab.shape[0]
    t_len = SSM_T
    assert seq % t_len == 0
    n_chunks = seq // t_len
    rows, cols = n_chunks * bsz, t_len * SSM_GROUP
    u_t = u.reshape(bsz, n_chunks, t_len, g, SSM_GROUP).transpose(3, 1, 0, 2, 4).reshape(g, rows, cols)
    per_group = lambda shape: pl.BlockSpec((1,) + shape, lambda i: (i, 0, 0))
    y_t = pl.pallas_call(
        functools.partial(_ssm_kernel, n_chunks=n_chunks, bsz=bsz),
        out_shape=jax.ShapeDtypeStruct((g, rows, cols), F32),
        grid=(g,),
        in_specs=[per_group((rows, cols)), per_group((cols, cols)), per_group((cols, STATE)),
                  per_group((cols, STATE)), per_group((2 * STATE, cols)), per_group((2, STATE)),
                  per_group((1, cols))],
        out_specs=per_group((rows, cols)),
        scratch_shapes=[pltpu.VMEM((rows, 2 * STATE), F32)],
        compiler_params=pltpu.CompilerParams(dimension_semantics=("parallel",),
                                             vmem_limit_bytes=V7X_VMEM_LIMIT_BYTES),
        name="ssm",
    )(u_t, m_tab, w_re, w_im, v_tab, a_t, d_tab)
    return y_t.reshape(g, n_chunks, bsz, t_len, SSM_GROUP).transpose(2, 1, 3, 0, 4).reshape(bsz * seq, g * SSM_GROUP)


def _t5_bucket_np(rel):
    half = N_BUCKETS // 2
    max_exact = half // 2
    ret = np.where(rel > 0, half, 0)
    n = np.abs(rel)
    n_f = np.maximum(n, 1).astype(np.float32)
    large = max_exact + (np.log(n_f / np.float32(max_exact)) / np.float32(math.log(MAX_DIST / max_exact))
                         * (half - max_exact)).astype(np.int32)
    large = np.minimum(large, half - 1)
    return ret + np.where(n < max_exact, n, large)


def _near_bias_tables(rel_bias, tq, tk):
    kk = np.arange(tk)[:, None]
    qq = np.arange(tq)[None, :]
    b_diag = _t5_bucket_np(kk - qq)
    b_prev = _t5_bucket_np(kk - qq - tk)
    far = int(_t5_bucket_np(np.array(-(tk + 1))))
    assert far == int(_t5_bucket_np(np.array(-(1 << 24)))), "far keys must share one bucket"
    tab = rel_bias.astype(F32)
    near = jnp.stack([tab[b_diag], tab[b_prev]], axis=0) - tab[far][None, None, None, :]
    return near.transpose(3, 0, 1, 2) * LOG2E


def _sortable(x):
    b = pltpu.bitcast(x, I32)
    return b ^ ((b >> 31) & 0x7FFFFFFF)


def _pair_lanes(h):
    return slice((h // 2) * 2 * HEAD_DIM, (h // 2 + 1) * 2 * HEAD_DIM)


def _dsa_kernel(qi_ref, wi_ref, q_ref, kidx_ref, k_ref, vt_ref, near_ref, o_ref,
                key_scr, mb_scr, qh_scr, m_scr, p_scr, acc_scr, out_scr, *, tq, tk, n_sel):
    i = pl.program_id(1)
    n_tiles = i + 1
    nt_dims = (((1,), (1,)), ((), ()))

    def score_tile(t):
        kid = kidx_ref[pl.ds(pl.multiple_of(t * tk, tk), tk), :]
        acc = jnp.zeros((tk, tq), F32)
        for h in range(IDX_HEADS):
            s = lax.dot_general(kid, qi_ref[h], nt_dims, preferred_element_type=F32)
            acc = acc + wi_ref[h:h + 1, :] * jnp.maximum(s, 0.0)
        return acc

    def score_body(t, carry):
        key_scr[pl.ds(pl.multiple_of(t * tk, tk), tk), :] = _sortable(score_tile(t) + 0.0)
        return carry

    lax.fori_loop(0, i, score_body, 0)
    kpos = lax.broadcasted_iota(I32, (tk, tq), 0)
    qpos = lax.broadcasted_iota(I32, (tk, tq), 1)
    admissible = (kpos // CHUNK) <= (qpos // CHUNK)
    diag = jnp.where(admissible, score_tile(i) + 0.0, -jnp.inf)
    key_scr[pl.ds(pl.multiple_of(i * tk, tk), tk), :] = _sortable(diag)

    def count(pred_fn):
        def body(t, c):
            kt = key_scr[pl.ds(pl.multiple_of(t * tk, tk), tk), :]
            hit = jnp.where(pred_fn(kt), 1, 0).astype(I32)
            return c + hit.reshape(tk // 8, 8, tq).sum(axis=0)
        part = lax.fori_loop(0, n_tiles, body, jnp.zeros((8, tq), I32))
        return jnp.sum(part, axis=0, keepdims=True)

    q_in_blk = lax.broadcasted_iota(I32, (1, tq), 1)
    n_adm = (i * tq + q_in_blk) // CHUNK * CHUNK + CHUNK
    take_all = jnp.where(n_adm <= n_sel, 1, 0).astype(I32)

    def radix_cond(st):
        b, _, done, _ = st
        return jnp.logical_and(b >= 0, jnp.min(done) == 0)

    def radix_body(st):
        b, prefix, done, thr = st
        cand_u = prefix | jnp.left_shift(jnp.int32(1), b)
        cand_s = cand_u ^ INT_MIN
        cnt = count(lambda kt: kt >= cand_s)
        live = done == 0
        prefix = jnp.where(jnp.logical_and(live, cnt >= n_sel), cand_u, prefix)
        hit = jnp.logical_and(live, cnt == n_sel)
        thr = jnp.where(hit, cand_s, thr)
        done = jnp.where(hit, 1, done)
        return b - 1, prefix, done, thr

    _, prefix, done, thr = lax.while_loop(
        radix_cond, radix_body,
        (jnp.int32(31), jnp.zeros((1, tq), I32), take_all, jnp.full((1, tq), KEY_NEG_INF, I32)))
    thr = jnp.where(done == 1, thr, prefix ^ INT_MIN)
    n_gt = count(lambda kt: kt > thr)
    need = jnp.where(take_all == 1, 0, n_sel - n_gt).astype(F32)

    r_i = lax.broadcasted_iota(I32, (tk, tk), 0)
    c_i = lax.broadcasted_iota(I32, (tk, tk), 1)
    strict_lower = jnp.where(c_i < r_i, 1.0, 0.0).astype(BF16)

    def mask_body(t, ties_before):
        rows = pl.ds(pl.multiple_of(t * tk, tk), tk)
        kt = key_scr[rows, :]
        eq = kt == thr
        eq_f = jnp.where(eq, 1.0, 0.0)
        rank = jnp.dot(strict_lower, eq_f.astype(BF16), preferred_element_type=F32) + ties_before
        sel = jnp.logical_or(kt > thr, jnp.logical_and(eq, rank < need))
        mb_scr[rows, :] = jnp.where(sel, 0.0, MASK_NEG)
        return ties_before + jnp.sum(eq_f.reshape(tk // 8, 8, tq).sum(axis=0), axis=0, keepdims=True)

    lax.fori_loop(0, n_tiles, mask_body, jnp.zeros((1, tq), F32))

    lane = lax.broadcasted_iota(I32, (tq, 2 * HEAD_DIM), 1)
    for h in range(N_HEADS):
        q_pair = q_ref[:, _pair_lanes(h)]
        own = (lane < HEAD_DIM) if h % 2 == 0 else (lane >= HEAD_DIM)
        qh_scr[h] = jnp.where(own, q_pair, jnp.zeros_like(q_pair))
    m_scr[...] = jnp.full(m_scr.shape, MASK_NEG, F32)
    acc_scr[...] = jnp.zeros(acc_scr.shape, F32)

    def masked_logits(t, h, near_which):
        rows = pl.ds(pl.multiple_of(t * tk, tk), tk)
        lg = lax.dot_general(k_ref[rows, _pair_lanes(h)], qh_scr[h], nt_dims,
                             preferred_element_type=F32) + mb_scr[rows, :]
        if near_which is not None:
            lg = lg + near_ref[h, near_which]
        return lg

    def over_tiles(tile_fn):
        def far_body(t, carry):
            tile_fn(t, None)
            return carry

        lax.fori_loop(0, jnp.maximum(i - 1, 0), far_body, 0)

        @pl.when(i >= 1)
        def _():
            tile_fn(i - 1, 1)

        tile_fn(i, 0)

    def max_tile(t, near_which):
        for h in range(N_HEADS):
            part = masked_logits(t, h, near_which).reshape(tk // 8, 8, tq).max(axis=0)
            m_scr[h * 8:(h + 1) * 8, :] = jnp.maximum(m_scr[h * 8:(h + 1) * 8, :], part)

    over_tiles(max_tile)
    for h in range(N_HEADS):
        m_scr[h * 8:(h + 1) * 8, :] = jnp.broadcast_to(
            jnp.max(m_scr[h * 8:(h + 1) * 8, :], axis=0, keepdims=True), (8, tq))

    def pv_tile(t, near_which):
        rows = pl.ds(pl.multiple_of(t * tk, tk), tk)
        for h in range(N_HEADS):
            p_scr[h] = jnp.exp2(masked_logits(t, h, near_which) - m_scr[h * 8:h * 8 + 1, :]).astype(BF16)
        for h in range(N_HEADS):
            vrows = slice(h * V_ROWS, (h + 1) * V_ROWS)
            acc_scr[vrows, :] += jnp.dot(vt_ref[vrows, rows], p_scr[h], preferred_element_type=F32)

    over_tiles(pv_tile)
    for h in range(N_HEADS):
        num = acc_scr[h * V_ROWS:h * V_ROWS + HEAD_DIM, :]
        den = acc_scr[h * V_ROWS + HEAD_DIM:h * V_ROWS + HEAD_DIM + 1, :]
        out_scr[h * HEAD_DIM:(h + 1) * HEAD_DIM, :] = num / den
    o_ref[0] = out_scr[...].T


def _dsa(q, k, v, qi, kw, rel_bias, bsz, seq):
    d_attn = q.shape[1]
    tq, tk = DSA_TQ, DSA_TK
    assert tq == tk and seq % tq == 0 and tq % CHUNK == 0
    n_sel = min(TOPK_MAX, seq // 4)
    q3 = q.reshape(bsz, seq, d_attn)
    k3 = k.reshape(bsz, seq, d_attn)
    vt = v.reshape(bsz, seq, N_HEADS, HEAD_DIM).transpose(0, 2, 3, 1)
    vt = jnp.concatenate([vt, jnp.ones((bsz, N_HEADS, 1, seq), BF16),
                          jnp.zeros((bsz, N_HEADS, V_ROWS - HEAD_DIM - 1, seq), BF16)], axis=2)
    vt = vt.reshape(bsz, N_HEADS * V_ROWS, seq)
    qi4 = qi.reshape(bsz, seq, IDX_HEADS, IDX_DIM).transpose(0, 2, 1, 3)
    kidx = kw[:, :IDX_DIM].astype(BF16).reshape(bsz, seq, IDX_DIM)
    wi_t = kw[:, IDX_DIM:IDX_DIM + IDX_HEADS].reshape(bsz, seq, IDX_HEADS).transpose(0, 2, 1)
    near = _near_bias_tables(rel_bias, tq, tk)
    per_batch = lambda shape: pl.BlockSpec((None,) + shape, lambda b, i: (b,) + (0,) * len(shape),
                                           pipeline_mode=pl.Buffered(1))
    return pl.pallas_call(
        functools.partial(_dsa_kernel, tq=tq, tk=tk, n_sel=n_sel),
        out_shape=jax.ShapeDtypeStruct((bsz, seq, d_attn), F32),
        grid=(bsz, seq // tq),
        in_specs=[pl.BlockSpec((None, IDX_HEADS, tq, IDX_DIM), lambda b, i: (b, 0, i, 0)),
                  pl.BlockSpec((None, IDX_HEADS, tq), lambda b, i: (b, 0, i)),
                  pl.BlockSpec((None, tq, d_attn), lambda b, i: (b, i, 0)),
                  per_batch((seq, IDX_DIM)), per_batch((seq, d_attn)), per_batch((N_HEADS * V_ROWS, seq)),
                  _resident(near.shape)],
        out_specs=pl.BlockSpec((1, tq, d_attn), lambda b, i: (b, i, 0)),
        scratch_shapes=[pltpu.VMEM((seq, tq), I32), pltpu.VMEM((seq, tq), F32),
                        pltpu.VMEM((N_HEADS, tq, 2 * HEAD_DIM), BF16), pltpu.VMEM((N_HEADS * 8, tq), F32),
                        pltpu.VMEM((N_HEADS, tk, tq), BF16),
                        pltpu.VMEM((N_HEADS * V_ROWS, tq), F32), pltpu.VMEM((d_attn, tq), F32)],
        compiler_params=pltpu.CompilerParams(dimension_semantics=("parallel", "arbitrary"),
                                             vmem_limit_bytes=V7X_VMEM_LIMIT_BYTES),
        name="dsa",
    )(qi4, wi_t, q3, kidx, k3, vt, near).reshape(bsz * seq, d_attn)


def _merge_ln_kernel(x_ref, ys_ref, ya_ref, gs_ref, ga_ref, wglu_ref, wb0_ref, wb1_ref, wout_ref,
                     g_ref, b_ref, o_ref):
    y = jax.nn.gelu(ys_ref[...])
    glu = y * jax.nn.sigmoid(jnp.dot(y.astype(BF16), wglu_ref[...], preferred_element_type=F32))
    p_ssm = jnp.dot(glu.astype(BF16), wb0_ref[...], preferred_element_type=F32)
    p_att = jnp.dot(ya_ref[...].astype(BF16), wb1_ref[...], preferred_element_type=F32)
    merged = gs_ref[...].astype(F32) * p_ssm + ga_ref[...].astype(F32) * p_att
    mix = jnp.dot(merged.astype(BF16), wout_ref[...], preferred_element_type=F32)
    o_ref[...] = _layer_norm(ALPHA * x_ref[...] + mix, g_ref[...], b_ref[...])


def _merge_ln(x2, y_ssm, y_attn, gs, ga, w_glu, wb0, wb1, w_out, g, b):
    n, d = x2.shape
    tm = min(MERGE_TM, n)
    assert n % tm == 0
    tile = lambda a: pl.BlockSpec((tm, a.shape[1]), lambda i: (i, 0))
    return pl.pallas_call(
        _merge_ln_kernel,
        out_shape=jax.ShapeDtypeStruct((n, d), F32),
        grid=(n // tm,),
        in_specs=[tile(x2), tile(y_ssm), tile(y_attn), tile(gs), tile(ga),
                  _resident(w_glu.shape), _resident(wb0.shape), _resident(wb1.shape), _resident(w_out.shape),
                  _resident((1, d)), _resident((1, d))],
        out_specs=pl.BlockSpec((tm, d), lambda i: (i, 0)),
        compiler_params=pltpu.CompilerParams(dimension_semantics=("parallel",),
                                             vmem_limit_bytes=V7X_VMEM_LIMIT_BYTES),
        name="merge_ln",
    )(x2, y_ssm, y_attn, gs, ga, w_glu, wb0, wb1, w_out, g.reshape(1, d), b.reshape(1, d))


def _split_w_in(w_in, d_model, d_ssm, d_attn):
    sizes = (d_ssm, d_attn, d_attn, d_attn, IDX_HEADS * IDX_DIM, IDX_DIM, IDX_HEADS, d_model, d_model)
    assert w_in.shape[1] == sum(sizes)
    edges = np.concatenate([[0], np.cumsum(sizes)])
    seg = [w_in[:, int(edges[j]):int(edges[j + 1])] for j in range(len(sizes))]
    kw = jnp.concatenate([seg[5], seg[6], jnp.zeros((w_in.shape[0], 128 - IDX_DIM - IDX_HEADS), w_in.dtype)], axis=1)
    parts = dict(u=seg[0], q=seg[1], k=seg[2], v=seg[3], qi=seg[4], kw=kw, gs=seg[7], ga=seg[8])
    offs, cols, pos = {}, [], 0
    for name, w in parts.items():
        offs[name] = (pos, pos + w.shape[1])
        pos += w.shape[1]
        cols.append(w)
    return jnp.concatenate(cols, axis=1).astype(BF16), offs


def kernel(x, ffn1_w_up, ffn1_w_down, ln1_g, ln1_b, w_in, ssm_lam_re, ssm_lam_im, ssm_log_dt, ssm_b_re, ssm_b_im, ssm_c_re, ssm_c_im, ssm_d, ssm_w_glu, w_branch, w_out, ln2_g, ln2_b, ffn2_w_up, ffn2_w_down, ln3_g, ln3_b, rel_bias):
    bsz, seq, d_model = x.shape
    d_ssm = ssm_w_glu.shape[1]
    d_attn = w_branch.shape[2]
    x2 = x.reshape(bsz * seq, d_model)
    for l in range(ffn1_w_up.shape[0]):
        x2 = _ffn_ln(x2, ffn1_w_up[l].astype(BF16), ffn1_w_down[l].astype(BF16), ln1_g[l], ln1_b[l])
        w_cat, offs = _split_w_in(w_in[l], d_model, d_ssm, d_attn)
        u, q, k, v, qi, kw, gs, ga = _in_proj(x2, w_cat, offs)
        tables = _ssm_tables(ssm_lam_re[l], ssm_lam_im[l], ssm_log_dt[l], ssm_b_re[l], ssm_b_im[l],
                             ssm_c_re[l], ssm_c_im[l], ssm_d[l], SSM_T)
        y_ssm = _ssm(u, tables, bsz, seq)
        y_attn = _dsa(q, k, v, qi, kw, rel_bias, bsz, seq)
        x2 = _merge_ln(x2, y_ssm, y_attn, gs, ga, ssm_w_glu[l].astype(BF16), w_branch[l, 0].astype(BF16),
                       w_branch[l, 1].astype(BF16), w_out[l].astype(BF16), ln2_g[l], ln2_b[l])
        x2 = _ffn_ln(x2, ffn2_w_up[l].astype(BF16), ffn2_w_down[l].astype(BF16), ln3_g[l], ln3_b[l])
    return x2.reshape(bsz, seq, d_model)
```

```python
import functools
import math

import numpy as np
import jax
import jax.numpy as jnp
from jax import lax
from jax.experimental import pallas as pl
from jax.experimental.pallas import tpu as pltpu

F32 = jnp.float32
BF16 = jnp.bfloat16
I32 = jnp.int32
I16 = jnp.int16

DEPTH = 2
CHUNK = 64
SSM_GROUP = 16
STATE = 64
N_HEADS = 8
HEAD_DIM = 64
IDX_HEADS = 8
IDX_DIM = 32
TOPK_MAX = 256
ATTN_SCALE = HEAD_DIM ** -0.5
LOG2E = math.log2(math.e)
N_BUCKETS = 32
MAX_DIST = 128
ALPHA = (2 * DEPTH) ** 0.25
LN_EPS = 1e-5

V7X_VMEM_LIMIT_BYTES = 58 * 1024 * 1024
LANES = 128
FFN_TM = 512
FFN_FC = 256
PROJ_TM = 512
MERGE_TM = 512
SSM_T = 32
DSA_TQ = 256
DSA_TK = 256
MASK_NEG = -1e30
V_ROWS = HEAD_DIM + 16
KEY_NEG_INF = int(np.int32(np.uint32(0x807FFFFF)))
HALF16 = 1 << 15


def _resident(shape):
    nd = len(shape)
    return pl.BlockSpec(shape, lambda *_: (0,) * nd, pipeline_mode=pl.Buffered(1))


def _layer_norm(y, g, b):
    mu = jnp.mean(y, axis=-1, keepdims=True)
    yc = y - mu
    var = jnp.mean(yc * yc, axis=-1, keepdims=True)
    return yc * lax.rsqrt(var + LN_EPS) * g + b


NT_DIMS = (((1,), (1,)), ((), ()))


def _ffn_ln_kernel(x_ref, wup_ref, wdn_ref, g_ref, b_ref, o_ref, *, d_ff, fc):
    x = x_ref[...]
    xb = x.astype(BF16)
    acc = jnp.zeros(x.shape, F32)
    for j in range(d_ff // fc):
        gate = jnp.dot(xb, wup_ref[:, j * fc:(j + 1) * fc], preferred_element_type=F32)
        up = jnp.dot(xb, wup_ref[:, d_ff + j * fc:d_ff + (j + 1) * fc], preferred_element_type=F32)
        a = (gate * jax.nn.sigmoid(gate)) * up
        acc = acc + jnp.dot(a.astype(BF16), wdn_ref[j * fc:(j + 1) * fc, :], preferred_element_type=F32)
    o_ref[...] = _layer_norm(ALPHA * x + 0.5 * acc, g_ref[...], b_ref[...])


def _ffn_ln(x2, w_up, w_down, g, b):
    n, d = x2.shape
    d_ff = w_down.shape[0]
    tm = min(FFN_TM, n)
    assert n % tm == 0 and d_ff % FFN_FC == 0
    return pl.pallas_call(
        functools.partial(_ffn_ln_kernel, d_ff=d_ff, fc=FFN_FC),
        out_shape=jax.ShapeDtypeStruct((n, d), F32),
        grid=(n // tm,),
        in_specs=[pl.BlockSpec((tm, d), lambda i: (i, 0)),
                  _resident(w_up.shape), _resident(w_down.shape),
                  _resident((1, d)), _resident((1, d))],
        out_specs=pl.BlockSpec((tm, d), lambda i: (i, 0)),
        compiler_params=pltpu.CompilerParams(dimension_semantics=("parallel",),
                                             vmem_limit_bytes=V7X_VMEM_LIMIT_BYTES),
        name="ffn_ln",
    )(x2, w_up, w_down, g.reshape(1, d), b.reshape(1, d))


def _in_proj_kernel(x_ref, w_ref, wt_ref, u_ref, q_ref, k_ref, kidx_ref, gs_ref, ga_ref,
                    vt_ref, qit_ref, kwt_ref, *, offs, offs_t):
    xb = x_ref[...].astype(BF16)

    def proj(name):
        lo, hi = offs[name]
        return jnp.dot(xb, w_ref[:, lo:hi], preferred_element_type=F32)

    def proj_t(name):
        lo, hi = offs_t[name]
        return lax.dot_general(wt_ref[lo:hi, :], xb, NT_DIMS, preferred_element_type=F32)

    u_ref[...] = proj("u")
    q_ref[...] = (proj("q") * (ATTN_SCALE * LOG2E)).astype(BF16)
    k_ref[...] = proj("k").astype(BF16)
    kidx_ref[...] = proj("kidx")[:, :IDX_DIM].astype(BF16)
    gs_ref[...] = jax.nn.sigmoid(proj("gs")).astype(BF16)
    ga_ref[...] = jax.nn.sigmoid(proj("ga")).astype(BF16)
    qit_ref[...] = proj_t("qi").astype(BF16)
    kwt_ref[...] = proj_t("kw")
    v_t = proj_t("v").astype(BF16)
    tm = v_t.shape[1]
    pad_rows = lax.broadcasted_iota(I32, (V_ROWS - HEAD_DIM, tm), 0)
    pad = jnp.where(pad_rows == 0, 1.0, 0.0).astype(BF16)
    for h in range(N_HEADS):
        vt_ref[h * V_ROWS:h * V_ROWS + HEAD_DIM, :] = v_t[h * HEAD_DIM:(h + 1) * HEAD_DIM, :]
        vt_ref[h * V_ROWS + HEAD_DIM:(h + 1) * V_ROWS, :] = pad


def _in_proj(x2, w_nat, offs, w_t, offs_t):
    n, d = x2.shape
    tm = min(PROJ_TM, n)
    assert n % tm == 0
    width = lambda name: offs[name][1] - offs[name][0]
    rows_t = lambda name: offs_t[name][1] - offs_t[name][0]
    tok = lambda w, dt: (jax.ShapeDtypeStruct((n, w), dt), pl.BlockSpec((tm, w), lambda i: (i, 0)))
    chan = lambda r, dt: (jax.ShapeDtypeStruct((r, n), dt), pl.BlockSpec((r, tm), lambda i: (0, i)))
    outs = [tok(width("u"), F32), tok(width("q"), BF16), tok(width("k"), BF16), tok(IDX_DIM, BF16),
            tok(width("gs"), BF16), tok(width("ga"), BF16),
            chan(N_HEADS * V_ROWS, BF16), chan(rows_t("qi"), BF16), chan(rows_t("kw"), F32)]
    return pl.pallas_call(
        functools.partial(_in_proj_kernel, offs=offs, offs_t=offs_t),
        out_shape=[o[0] for o in outs],
        grid=(n // tm,),
        in_specs=[pl.BlockSpec((tm, d), lambda i: (i, 0)), _resident(w_nat.shape), _resident(w_t.shape)],
        out_specs=[o[1] for o in outs],
        compiler_params=pltpu.CompilerParams(dimension_semantics=("parallel",),
                                             vmem_limit_bytes=V7X_VMEM_LIMIT_BYTES),
        name="in_proj",
    )(x2, w_nat, w_t)


def _ssm_kernel(u_ref, m_ref, wre_ref, wim_ref, v_ref, at_ref, d_ref, y_ref, hin_scr, *, n_chunks, bsz):
    u = u_ref[0]
    ub = u.astype(BF16)
    s_re = jnp.dot(ub, wre_ref[0], preferred_element_type=F32)
    s_im = jnp.dot(ub, wim_ref[0], preferred_element_type=F32)
    a_re = at_ref[0, 0:1, :]
    a_im = at_ref[0, 1:2, :]
    h_re = jnp.zeros((bsz, STATE), F32)
    h_im = jnp.zeros((bsz, STATE), F32)
    for c in range(n_chunks):
        rows = slice(c * bsz, (c + 1) * bsz)
        hin_scr[rows, 0:STATE] = h_re
        hin_scr[rows, STATE:2 * STATE] = h_im
        n_re = a_re * h_re - a_im * h_im + s_re[rows]
        n_im = a_re * h_im + a_im * h_re + s_im[rows]
        h_re, h_im = n_re, n_im
    y = jnp.dot(ub, m_ref[0], preferred_element_type=F32)
    y = y + jnp.dot(hin_scr[...].astype(BF16), v_ref[0], preferred_element_type=F32)
    y_ref[0] = y + d_ref[0] * u


def _ssm_tables(lam_re, lam_im, log_dt, b_re, b_im, c_re, c_im, d_skip, t_len):
    g = lam_re.shape[0]
    lam = lax.complex(jnp.minimum(lam_re.astype(F32), -1e-4), lam_im.astype(F32))
    dt = jnp.exp(log_dt.astype(F32))[:, None]
    a_bar = jnp.exp(lam * dt)
    b_bar = ((a_bar - 1.0) / lam)[:, :, None] * lax.complex(b_re.astype(F32), b_im.astype(F32))
    c_mat = lax.complex(c_re.astype(F32), c_im.astype(F32))
    taus = jnp.arange(t_len + 1, dtype=F32)
    apow = jnp.exp((lam * dt)[:, None, :] * taus[None, :, None])
    hp = lax.Precision.HIGHEST
    kern = jnp.einsum("ghp,gtp,gpk->gkth", c_mat, apow[:, :t_len], b_bar, precision=hp).real
    padded = jnp.concatenate([jnp.zeros_like(kern), kern], axis=2)
    m_tab = jnp.stack([padded[:, :, t_len - s:2 * t_len - s, :] for s in range(t_len)], axis=1)
    m_tab = m_tab.reshape(g, t_len * SSM_GROUP, t_len * SSM_GROUP)
    w_c = apow[:, :t_len][:, ::-1, None, :] * b_bar.transpose(0, 2, 1)[:, None, :, :]
    w_c = w_c.reshape(g, t_len * SSM_GROUP, STATE)
    v_c = c_mat.transpose(0, 2, 1)[:, :, None, :] * apow[:, 1:t_len + 1].transpose(0, 2, 1)[:, :, :, None]
    v_c = v_c.reshape(g, STATE, t_len * SSM_GROUP)
    v_tab = jnp.concatenate([v_c.real, -v_c.imag], axis=1)
    a_t = jnp.stack([apow[:, t_len].real, apow[:, t_len].imag], axis=1)
    d_tab = jnp.tile(d_skip.astype(F32).reshape(g, 1, SSM_GROUP), (1, t_len, 1)).reshape(g, 1, t_len * SSM_GROUP)
    return (m_tab.astype(BF16), w_c.real.astype(BF16), w_c.imag.astype(BF16), v_tab.astype(BF16), a_t, d_tab)


def _ssm(u, tables, bsz, seq):
    m_tab, w_re, w_im, v_tab, a_t, d_tab = tables
    g = m_tab.shape[0]
    t_len = SSM_T
    assert seq % t_len == 0
    n_chunks = seq // t_len
    rows, cols = n_chunks * bsz, t_len * SSM_GROUP
    u_t = u.reshape(bsz, n_chunks, t_len, g, SSM_GROUP).transpose(3, 1, 0, 2, 4).reshape(g, rows, cols)
    per_group = lambda shape: pl.BlockSpec((1,) + shape, lambda i: (i, 0, 0))
    y_t = pl.pallas_call(
        functools.partial(_ssm_kernel, n_chunks=n_chunks, bsz=bsz),
        out_shape=jax.ShapeDtypeStruct((g, rows, cols), F32),
        grid=(g,),
        in_specs=[per_group((rows, cols)), per_group((cols, cols)), per_group((cols, STATE)),
                  per_group((cols, STATE)), per_group((2 * STATE, cols)), per_group((2, STATE)),
                  per_group((1, cols))],
        out_specs=per_group((rows, cols)),
        scratch_shapes=[pltpu.VMEM((rows, 2 * STATE), F32)],
        compiler_params=pltpu.CompilerParams(dimension_semantics=("parallel",),
                                             vmem_limit_bytes=V7X_VMEM_LIMIT_BYTES),
        name="ssm",
    )(u_t, m_tab, w_re, w_im, v_tab, a_t, d_tab)
    return y_t.reshape(g, n_chunks, bsz, t_len, SSM_GROUP).transpose(2, 1, 3, 0, 4).reshape(bsz * seq, g * SSM_GROUP)


def _t5_bucket_np(rel):
    half = N_BUCKETS // 2
    max_exact = half // 2
    ret = np.where(rel > 0, half, 0)
    n = np.abs(rel)
    n_f = np.maximum(n, 1).astype(np.float32)
    large = max_exact + (np.log(n_f / np.float32(max_exact)) / np.float32(math.log(MAX_DIST / max_exact))
                         * (half - max_exact)).astype(np.int32)
    large = np.minimum(large, half - 1)
    return ret + np.where(n < max_exact, n, large)


def _near_bias_tables(rel_bias, tq, tk):
    kk = np.arange(tk)[:, None]
    qq = np.arange(tq)[None, :]
    b_diag = _t5_bucket_np(kk - qq)
    b_prev = _t5_bucket_np(kk - qq - tk)
    far = int(_t5_bucket_np(np.array(-(tk + 1))))
    assert far == int(_t5_bucket_np(np.array(-(1 << 24)))), "far keys must share one bucket"
    tab = rel_bias.astype(F32)
    near = jnp.stack([tab[b_diag], tab[b_prev]], axis=0) - tab[far][None, None, None, :]
    return near.transpose(3, 0, 1, 2) * LOG2E


def _sortable(x):
    b = pltpu.bitcast(x, I32)
    return b ^ ((b >> 31) & 0x7FFFFFFF)


def _pair_lanes(h):
    return slice((h // 2) * 2 * HEAD_DIM, (h // 2 + 1) * 2 * HEAD_DIM)


def _dsa_kernel(qit_ref, wi_ref, q_ref, kidx_ref, k_ref, vt_ref, near_ref, o_ref,
                key_scr, k16_scr, mb_scr, qh_scr, m_scr, p_scr, acc_scr, out_scr, *, tq, tk, n_sel):
    i = pl.program_id(1)
    n_tiles = i + 1
    tile_rows = lambda t: pl.ds(pl.multiple_of(t * tk, tk), tk)

    def score_tile(t):
        kid = kidx_ref[tile_rows(t), :]
        acc = jnp.zeros((tk, tq), F32)
        for h in range(IDX_HEADS):
            s = jnp.dot(kid, qit_ref[h * IDX_DIM:(h + 1) * IDX_DIM, :], preferred_element_type=F32)
            acc = acc + wi_ref[h:h + 1, :] * jnp.maximum(s, 0.0)
        return acc

    def store_keys(t, score):
        key = _sortable(score)
        key_scr[tile_rows(t), :] = key
        k16_scr[tile_rows(t), :] = (key >> 16).astype(I16)

    def score_body(t, carry):
        store_keys(t, score_tile(t) + 0.0)
        return carry

    lax.fori_loop(0, i, score_body, 0)
    kpos = lax.broadcasted_iota(I32, (tk, tq), 0)
    qpos = lax.broadcasted_iota(I32, (tk, tq), 1)
    admissible = (kpos // CHUNK) <= (qpos // CHUNK)
    store_keys(i, jnp.where(admissible, score_tile(i) + 0.0, -jnp.inf))

    def count16(cand):
        c16 = cand.astype(I16)

        def body(t, c):
            hit = jnp.where(k16_scr[tile_rows(t), :] >= c16, jnp.int16(1), jnp.int16(0))
            for r in range(tk // 16):
                c = c + hit[r * 16:(r + 1) * 16]
            return c

        part = lax.fori_loop(0, n_tiles, body, jnp.zeros((16, tq), I16))
        return jnp.sum(part.astype(I32), axis=0, keepdims=True)

    def radix16(done, thr, c_low, key_of_digit):
        def cond(st):
            b, _, done, _, _ = st
            return jnp.logical_and(b >= 0, jnp.min(done) == 0)

        def body(st):
            b, digit, done, thr, c_low = st
            cand = digit | jnp.left_shift(jnp.int32(1), b)
            cnt = count16(cand - HALF16)
            live = done == 0
            accept = jnp.logical_and(live, cnt >= n_sel)
            digit = jnp.where(accept, cand, digit)
            c_low = jnp.where(accept, cnt, c_low)
            hit = jnp.logical_and(live, cnt == n_sel)
            thr = jnp.where(hit, key_of_digit(cand), thr)
            done = jnp.where(hit, 1, done)
            return b - 1, digit, done, thr, c_low

        _, digit, done, thr, c_low = lax.while_loop(
            cond, body, (jnp.int32(15), jnp.zeros((1, tq), I32), done, thr, c_low))
        return digit, done, thr, c_low

    q_in_blk = lax.broadcasted_iota(I32, (1, tq), 1)
    n_adm = (i * tq + q_in_blk) // CHUNK * CHUNK + CHUNK
    take_all = n_adm <= n_sel
    done0 = jnp.where(take_all, 1, 0).astype(I32)
    thr0 = jnp.full((1, tq), KEY_NEG_INF + 1, I32)
    c_low0 = jnp.full((1, tq), 1, I32) * (n_tiles * tk)

    hi_digit, done, thr, c_low = radix16(done0, thr0, c_low0, lambda d: jnp.left_shift(d - HALF16, 16))
    hi16 = hi_digit - HALF16

    def low_digit_body(t, carry):
        key = key_scr[tile_rows(t), :]
        hi = key >> 16
        low = (key & 0xFFFF) - HALF16
        k16_scr[tile_rows(t), :] = jnp.where(hi > hi16, HALF16 - 1, jnp.where(hi < hi16, -HALF16, low)).astype(I16)
        return carry

    lax.fori_loop(0, n_tiles, low_digit_body, 0)
    lo_digit, done, thr, c_low = radix16(done, thr, c_low, lambda d: jnp.left_shift(hi16, 16) | d)
    thr = jnp.where(done == 1, thr, jnp.left_shift(hi16, 16) | lo_digit)
    excess = jnp.where(done == 1, 0, c_low - n_sel)
    any_excess = jnp.max(excess) > 0

    @pl.when(jnp.logical_not(any_excess))
    def _():
        def mask_body(t, carry):
            mb_scr[tile_rows(t), :] = jnp.where(key_scr[tile_rows(t), :] >= thr, 0.0, MASK_NEG)
            return carry

        lax.fori_loop(0, n_tiles, mask_body, 0)

    @pl.when(any_excess)
    def _():
        def gt_body(t, c):
            hit = jnp.where(key_scr[tile_rows(t), :] > thr, 1, 0).astype(I32)
            return c + hit.reshape(tk // 8, 8, tq).sum(axis=0)

        n_gt = jnp.sum(lax.fori_loop(0, n_tiles, gt_body, jnp.zeros((8, tq), I32)), axis=0, keepdims=True)
        need = jnp.where(take_all, 2 * tk * n_tiles, n_sel - n_gt).astype(F32)
        r_i = lax.broadcasted_iota(I32, (tk, tk), 0)
        c_i = lax.broadcasted_iota(I32, (tk, tk), 1)
        strict_lower = jnp.where(c_i < r_i, 1.0, 0.0).astype(BF16)

        def mask_body(t, ties_before):
            kt = key_scr[tile_rows(t), :]
            eq = kt == thr
            eq_f = jnp.where(eq, 1.0, 0.0)
            rank = jnp.dot(strict_lower, eq_f.astype(BF16), preferred_element_type=F32) + ties_before
            sel = jnp.logical_or(kt > thr, jnp.logical_and(eq, rank < need))
            mb_scr[tile_rows(t), :] = jnp.where(sel, 0.0, MASK_NEG)
            return ties_before + jnp.sum(eq_f.reshape(tk // 8, 8, tq).sum(axis=0), axis=0, keepdims=True)

        lax.fori_loop(0, n_tiles, mask_body, jnp.zeros((1, tq), F32))

    lane = lax.broadcasted_iota(I32, (tq, 2 * HEAD_DIM), 1)
    for h in range(N_HEADS):
        q_pair = q_ref[:, _pair_lanes(h)]
        own = (lane < HEAD_DIM) if h % 2 == 0 else (lane >= HEAD_DIM)
        qh_scr[h] = jnp.where(own, q_pair, jnp.zeros_like(q_pair))
    m_scr[...] = jnp.full(m_scr.shape, MASK_NEG, F32)
    acc_scr[...] = jnp.zeros(acc_scr.shape, F32)

    def masked_logits(t, h, near_which):
        lg = lax.dot_general(k_ref[tile_rows(t), _pair_lanes(h)], qh_scr[h], NT_DIMS,
                             preferred_element_type=F32) + mb_scr[tile_rows(t), :]
        if near_which is not None:
            lg = lg + near_ref[h, near_which]
        return lg

    def over_tiles(tile_fn):
        def far_body(t, carry):
            tile_fn(t, None)
            return carry

        lax.fori_loop(0, jnp.maximum(i - 1, 0), far_body, 0)

        @pl.when(i >= 1)
        def _():
            tile_fn(i - 1, 1)

        tile_fn(i, 0)

    def max_tile(t, near_which):
        for h in range(N_HEADS):
            part = masked_logits(t, h, near_which).reshape(tk // 8, 8, tq).max(axis=0)
            m_scr[h * 8:(h + 1) * 8, :] = jnp.maximum(m_scr[h * 8:(h + 1) * 8, :], part)

    over_tiles(max_tile)
    for h in range(N_HEADS):
        m_scr[h * 8:(h + 1) * 8, :] = jnp.broadcast_to(
            jnp.max(m_scr[h * 8:(h + 1) * 8, :], axis=0, keepdims=True), (8, tq))

    def pv_tile(t, near_which):
        for h in range(N_HEADS):
            p_scr[h] = jnp.exp2(masked_logits(t, h, near_which) - m_scr[h * 8:h * 8 + 1, :]).astype(BF16)
        for h in range(N_HEADS):
            vrows = slice(h * V_ROWS, (h + 1) * V_ROWS)
            acc_scr[vrows, :] += jnp.dot(vt_ref[vrows, tile_rows(t)], p_scr[h], preferred_element_type=F32)

    over_tiles(pv_tile)
    for h in range(N_HEADS):
        num = acc_scr[h * V_ROWS:h * V_ROWS + HEAD_DIM, :]
        den = acc_scr[h * V_ROWS + HEAD_DIM:h * V_ROWS + HEAD_DIM + 1, :]
        out_scr[h * HEAD_DIM:(h + 1) * HEAD_DIM, :] = num / den
    o_ref[...] = out_scr[...].T


def _dsa(q, k, kidx, vt, qit, kwt, rel_bias, bsz, seq):
    d_attn = q.shape[1]
    tq, tk = DSA_TQ, DSA_TK
    assert tq == tk and seq % tq == 0 and tq % CHUNK == 0 and IDX_DIM % IDX_HEADS == 0
    n_sel = min(TOPK_MAX, seq // 4)
    nq = seq // tq
    near = _near_bias_tables(rel_bias, tq, tk)
    once = dict(pipeline_mode=pl.Buffered(1))
    return pl.pallas_call(
        functools.partial(_dsa_kernel, tq=tq, tk=tk, n_sel=n_sel),
        out_shape=jax.ShapeDtypeStruct((bsz * seq, d_attn), F32),
        grid=(bsz, nq),
        in_specs=[pl.BlockSpec((IDX_HEADS * IDX_DIM, tq), lambda b, i: (0, b * nq + i)),
                  pl.BlockSpec((IDX_HEADS, tq), lambda b, i: (IDX_DIM // IDX_HEADS, b * nq + i)),
                  pl.BlockSpec((tq, d_attn), lambda b, i: (b * nq + i, 0)),
                  pl.BlockSpec((seq, IDX_DIM), lambda b, i: (b, 0), **once),
                  pl.BlockSpec((seq, d_attn), lambda b, i: (b, 0), **once),
                  pl.BlockSpec((N_HEADS * V_ROWS, seq), lambda b, i: (0, b), **once),
                  _resident(near.shape)],
        out_specs=pl.BlockSpec((tq, d_attn), lambda b, i: (b * nq + i, 0)),
        scratch_shapes=[pltpu.VMEM((seq, tq), I32), pltpu.VMEM((seq, tq), I16), pltpu.VMEM((seq, tq), F32),
                        pltpu.VMEM((N_HEADS, tq, 2 * HEAD_DIM), BF16), pltpu.VMEM((N_HEADS * 8, tq), F32),
                        pltpu.VMEM((N_HEADS, tk, tq), BF16),
                        pltpu.VMEM((N_HEADS * V_ROWS, tq), F32), pltpu.VMEM((d_attn, tq), F32)],
        compiler_params=pltpu.CompilerParams(dimension_semantics=("parallel", "arbitrary"),
                                             vmem_limit_bytes=V7X_VMEM_LIMIT_BYTES),
        name="dsa",
    )(qit, kwt, q, kidx, k, vt, near)


def _merge_ln_kernel(x_ref, ys_ref, ya_ref, gs_ref, ga_ref, wglu_ref, wb0_ref, wb1_ref, wout_ref,
                     g_ref, b_ref, o_ref):
    y = jax.nn.gelu(ys_ref[...])
    glu = y * jax.nn.sigmoid(jnp.dot(y.astype(BF16), wglu_ref[...], preferred_element_type=F32))
    p_ssm = jnp.dot(glu.astype(BF16), wb0_ref[...], preferred_element_type=F32)
    p_att = jnp.dot(ya_ref[...].astype(BF16), wb1_ref[...], preferred_element_type=F32)
    merged = gs_ref[...].astype(F32) * p_ssm + ga_ref[...].astype(F32) * p_att
    mix = jnp.dot(merged.astype(BF16), wout_ref[...], preferred_element_type=F32)
    o_ref[...] = _layer_norm(ALPHA * x_ref[...] + mix, g_ref[...], b_ref[...])


def _merge_ln(x2, y_ssm, y_attn, gs, ga, w_glu, wb0, wb1, w_out, g, b):
    n, d = x2.shape
    tm = min(MERGE_TM, n)
    assert n % tm == 0
    tile = lambda a: pl.BlockSpec((tm, a.shape[1]), lambda i: (i, 0))
    return pl.pallas_call(
        _merge_ln_kernel,
        out_shape=jax.ShapeDtypeStruct((n, d), F32),
        grid=(n // tm,),
        in_specs=[tile(x2), tile(y_ssm), tile(y_attn), tile(gs), tile(ga),
                  _resident(w_glu.shape), _resident(wb0.shape), _resident(wb1.shape), _resident(w_out.shape),
                  _resident((1, d)), _resident((1, d))],
        out_specs=pl.BlockSpec((tm, d), lambda i: (i, 0)),
        compiler_params=pltpu.CompilerParams(dimension_semantics=("parallel",),
                                             vmem_limit_bytes=V7X_VMEM_LIMIT_BYTES),
        name="merge_ln",
    )(x2, y_ssm, y_attn, gs, ga, w_glu, wb0, wb1, w_out, g.reshape(1, d), b.reshape(1, d))


def _pack_cols(parts):
    offs, pos = {}, 0
    for name, w in parts.items():
        offs[name] = (pos, pos + w.shape[1])
        pos += w.shape[1]
    return jnp.concatenate(list(parts.values()), axis=1), offs


def _split_w_in(w_in, d_model, d_ssm, d_attn):
    sizes = (d_ssm, d_attn, d_attn, d_attn, IDX_HEADS * IDX_DIM, IDX_DIM, IDX_HEADS, d_model, d_model)
    assert w_in.shape[1] == sum(sizes)
    edges = np.concatenate([[0], np.cumsum(sizes)])
    u, q, k, v, qi, ki, wi, gs, ga = [w_in[:, int(edges[j]):int(edges[j + 1])] for j in range(len(sizes))]
    zeros = lambda c: jnp.zeros((w_in.shape[0], c), w_in.dtype)
    w_nat, offs = _pack_cols(dict(u=u, q=q, k=k, kidx=jnp.concatenate([ki, zeros(LANES - IDX_DIM)], axis=1),
                                  gs=gs, ga=ga))
    kw = jnp.concatenate([ki, wi, zeros(LANES - IDX_DIM - IDX_HEADS)], axis=1)
    w_t, offs_t = _pack_cols(dict(v=v, qi=qi, kw=kw))
    return w_nat.astype(BF16), offs, w_t.T.astype(BF16), offs_t


def kernel(x, ffn1_w_up, ffn1_w_down, ln1_g, ln1_b, w_in, ssm_lam_re, ssm_lam_im, ssm_log_dt, ssm_b_re, ssm_b_im, ssm_c_re, ssm_c_im, ssm_d, ssm_w_glu, w_branch, w_out, ln2_g, ln2_b, ffn2_w_up, ffn2_w_down, ln3_g, ln3_b, rel_bias):
    bsz, seq, d_model = x.shape
    d_ssm = ssm_w_glu.shape[1]
    d_attn = w_branch.shape[2]
    x2 = x.reshape(bsz * seq, d_model)
    for l in range(ffn1_w_up.shape[0]):
        x2 = _ffn_ln(x2, ffn1_w_up[l].astype(BF16), ffn1_w_down[l].astype(BF16), ln1_g[l], ln1_b[l])
        u, q, k, kidx, gs, ga, vt, qit, kwt = _in_proj(x2, *_split_w_in(w_in[l], d_model, d_ssm, d_attn))
        tables = _ssm_tables(ssm_lam_re[l], ssm_lam_im[l], ssm_log_dt[l], ssm_b_re[l], ssm_b_im[l],
                             ssm_c_re[l], ssm_c_im[l], ssm_d[l], SSM_T)
        y_ssm = _ssm(u, tables, bsz, seq)
        y_attn = _dsa(q, k, kidx, vt, qit, kwt, rel_bias, bsz, seq)
        x2 = _merge_ln(x2, y_ssm, y_attn, gs, ga, ssm_w_glu[l].astype(BF16), w_branch[l, 0].astype(BF16),
                       w_branch[l, 1].astype(BF16), w_out[l].astype(BF16), ln2_g[l], ln2_b[l])
        x2 = _ffn_ln(x2, ffn2_w_up[l].astype(BF16), ffn2_w_down[l].astype(BF16), ln3_g[l], ln3_b[l])
    return x2.reshape(bsz, seq, d_model)
```

```python
import functools
import math

import numpy as np
import jax
import jax.numpy as jnp
from jax import lax
from jax.experimental import pallas as pl
from jax.experimental.pallas import tpu as pltpu

F32 = jnp.float32
BF16 = jnp.bfloat16
I32 = jnp.int32
I16 = jnp.int16

DEPTH = 2
CHUNK = 64
SSM_GROUP = 16
STATE = 64
N_HEADS = 8
HEAD_DIM = 64
IDX_HEADS = 8
IDX_DIM = 32
TOPK_MAX = 256
ATTN_SCALE = HEAD_DIM ** -0.5
LOG2E = math.log2(math.e)
N_BUCKETS = 32
MAX_DIST = 128
ALPHA = (2 * DEPTH) ** 0.25
LN_EPS = 1e-5

V7X_VMEM_LIMIT_BYTES = 58 * 1024 * 1024
LANES = 128
FFN_TM = 512
FFN_FC = 256
PROJ_TM = 512
MERGE_TM = 512
SSM_T = 32
DSA_TQ = 256
DSA_TK = 256
MASK_NEG = -1e30
V_ROWS = HEAD_DIM + 16
KEY_NEG_INF = int(np.int32(np.uint32(0x807FFFFF)))
HALF16 = 1 << 15
MIN_SAFE_DEN = 2.0 ** -80
BOUND_SLACK = 1.02


def _resident(shape):
    nd = len(shape)
    return pl.BlockSpec(shape, lambda *_: (0,) * nd, pipeline_mode=pl.Buffered(1))


def _layer_norm(y, g, b):
    mu = jnp.mean(y, axis=-1, keepdims=True)
    yc = y - mu
    var = jnp.mean(yc * yc, axis=-1, keepdims=True)
    return yc * lax.rsqrt(var + LN_EPS) * g + b


NT_DIMS = (((1,), (1,)), ((), ()))


def _ffn_ln_kernel(x_ref, wup_ref, wdn_ref, g_ref, b_ref, o_ref, *, d_ff, fc):
    x = x_ref[...]
    xb = x.astype(BF16)
    acc = jnp.zeros(x.shape, F32)
    for j in range(d_ff // fc):
        gate = jnp.dot(xb, wup_ref[:, j * fc:(j + 1) * fc], preferred_element_type=F32)
        up = jnp.dot(xb, wup_ref[:, d_ff + j * fc:d_ff + (j + 1) * fc], preferred_element_type=F32)
        a = (gate * jax.nn.sigmoid(gate)) * up
        acc = acc + jnp.dot(a.astype(BF16), wdn_ref[j * fc:(j + 1) * fc, :], preferred_element_type=F32)
    o_ref[...] = _layer_norm(ALPHA * x + 0.5 * acc, g_ref[...], b_ref[...])


def _ffn_ln(x2, w_up, w_down, g, b):
    n, d = x2.shape
    d_ff = w_down.shape[0]
    tm = min(FFN_TM, n)
    assert n % tm == 0 and d_ff % FFN_FC == 0
    return pl.pallas_call(
        functools.partial(_ffn_ln_kernel, d_ff=d_ff, fc=FFN_FC),
        out_shape=jax.ShapeDtypeStruct((n, d), F32),
        grid=(n // tm,),
        in_specs=[pl.BlockSpec((tm, d), lambda i: (i, 0)),
                  _resident(w_up.shape), _resident(w_down.shape),
                  _resident((1, d)), _resident((1, d))],
        out_specs=pl.BlockSpec((tm, d), lambda i: (i, 0)),
        compiler_params=pltpu.CompilerParams(dimension_semantics=("parallel",),
                                             vmem_limit_bytes=V7X_VMEM_LIMIT_BYTES),
        name="ffn_ln",
    )(x2, w_up, w_down, g.reshape(1, d), b.reshape(1, d))


def _in_proj_kernel(x_ref, w_ref, wt_ref, u_ref, q_ref, k_ref, kidx_ref, gs_ref, ga_ref,
                    vt_ref, qit_ref, kwt_ref, *, offs, offs_t):
    xb = x_ref[...].astype(BF16)

    def proj(name):
        lo, hi = offs[name]
        return jnp.dot(xb, w_ref[:, lo:hi], preferred_element_type=F32)

    def proj_t(name):
        lo, hi = offs_t[name]
        return lax.dot_general(wt_ref[lo:hi, :], xb, NT_DIMS, preferred_element_type=F32)

    u_ref[...] = proj("u")
    q_ref[...] = (proj("q") * (ATTN_SCALE * LOG2E)).astype(BF16)
    k_ref[...] = proj("k").astype(BF16)
    kidx_ref[...] = proj("kidx")[:, :IDX_DIM].astype(BF16)
    gs_ref[...] = jax.nn.sigmoid(proj("gs")).astype(BF16)
    ga_ref[...] = jax.nn.sigmoid(proj("ga")).astype(BF16)
    qit_ref[...] = proj_t("qi").astype(BF16)
    kwt_ref[...] = proj_t("kw")
    v_t = proj_t("v").astype(BF16)
    tm = v_t.shape[1]
    pad_rows = lax.broadcasted_iota(I32, (V_ROWS - HEAD_DIM, tm), 0)
    pad = jnp.where(pad_rows == 0, 1.0, 0.0).astype(BF16)
    for h in range(N_HEADS):
        vt_ref[h * V_ROWS:h * V_ROWS + HEAD_DIM, :] = v_t[h * HEAD_DIM:(h + 1) * HEAD_DIM, :]
        vt_ref[h * V_ROWS + HEAD_DIM:(h + 1) * V_ROWS, :] = pad


def _in_proj(x2, w_nat, offs, w_t, offs_t):
    n, d = x2.shape
    tm = min(PROJ_TM, n)
    assert n % tm == 0
    width = lambda name: offs[name][1] - offs[name][0]
    rows_t = lambda name: offs_t[name][1] - offs_t[name][0]
    tok = lambda w, dt: (jax.ShapeDtypeStruct((n, w), dt), pl.BlockSpec((tm, w), lambda i: (i, 0)))
    chan = lambda r, dt: (jax.ShapeDtypeStruct((r, n), dt), pl.BlockSpec((r, tm), lambda i: (0, i)))
    outs = [tok(width("u"), F32), tok(width("q"), BF16), tok(width("k"), BF16), tok(IDX_DIM, BF16),
            tok(width("gs"), BF16), tok(width("ga"), BF16),
            chan(N_HEADS * V_ROWS, BF16), chan(rows_t("qi"), BF16), chan(rows_t("kw"), F32)]
    return pl.pallas_call(
        functools.partial(_in_proj_kernel, offs=offs, offs_t=offs_t),
        out_shape=[o[0] for o in outs],
        grid=(n // tm,),
        in_specs=[pl.BlockSpec((tm, d), lambda i: (i, 0)), _resident(w_nat.shape), _resident(w_t.shape)],
        out_specs=[o[1] for o in outs],
        compiler_params=pltpu.CompilerParams(dimension_semantics=("parallel",),
                                             vmem_limit_bytes=V7X_VMEM_LIMIT_BYTES),
        name="in_proj",
    )(x2, w_nat, w_t)


def _ssm_kernel(u_ref, m_ref, wre_ref, wim_ref, v_ref, at_ref, d_ref, y_ref, hin_scr, *, n_chunks, bsz):
    u = u_ref[0]
    ub = u.astype(BF16)
    s_re = jnp.dot(ub, wre_ref[0], preferred_element_type=F32)
    s_im = jnp.dot(ub, wim_ref[0], preferred_element_type=F32)
    a_re = at_ref[0, 0:1, :]
    a_im = at_ref[0, 1:2, :]
    h_re = jnp.zeros((bsz, STATE), F32)
    h_im = jnp.zeros((bsz, STATE), F32)
    for c in range(n_chunks):
        rows = slice(c * bsz, (c + 1) * bsz)
        hin_scr[rows, 0:STATE] = h_re
        hin_scr[rows, STATE:2 * STATE] = h_im
        n_re = a_re * h_re - a_im * h_im + s_re[rows]
        n_im = a_re * h_im + a_im * h_re + s_im[rows]
        h_re, h_im = n_re, n_im
    y = jnp.dot(ub, m_ref[0], preferred_element_type=F32)
    y = y + jnp.dot(hin_scr[...].astype(BF16), v_ref[0], preferred_element_type=F32)
    y_ref[0] = y + d_ref[0] * u


def _ssm_tables(lam_re, lam_im, log_dt, b_re, b_im, c_re, c_im, d_skip, t_len):
    g = lam_re.shape[0]
    lam = lax.complex(jnp.minimum(lam_re.astype(F32), -1e-4), lam_im.astype(F32))
    dt = jnp.exp(log_dt.astype(F32))[:, None]
    a_bar = jnp.exp(lam * dt)
    b_bar = ((a_bar - 1.0) / lam)[:, :, None] * lax.complex(b_re.astype(F32), b_im.astype(F32))
    c_mat = lax.complex(c_re.astype(F32), c_im.astype(F32))
    taus = jnp.arange(t_len + 1, dtype=F32)
    apow = jnp.exp((lam * dt)[:, None, :] * taus[None, :, None])
    hp = lax.Precision.HIGHEST
    kern = jnp.einsum("ghp,gtp,gpk->gkth", c_mat, apow[:, :t_len], b_bar, precision=hp).real
    padded = jnp.concatenate([jnp.zeros_like(kern), kern], axis=2)
    m_tab = jnp.stack([padded[:, :, t_len - s:2 * t_len - s, :] for s in range(t_len)], axis=1)
    m_tab = m_tab.reshape(g, t_len * SSM_GROUP, t_len * SSM_GROUP)
    w_c = apow[:, :t_len][:, ::-1, None, :] * b_bar.transpose(0, 2, 1)[:, None, :, :]
    w_c = w_c.reshape(g, t_len * SSM_GROUP, STATE)
    v_c = c_mat.transpose(0, 2, 1)[:, :, None, :] * apow[:, 1:t_len + 1].transpose(0, 2, 1)[:, :, :, None]
    v_c = v_c.reshape(g, STATE, t_len * SSM_GROUP)
    v_tab = jnp.concatenate([v_c.real, -v_c.imag], axis=1)
    a_t = jnp.stack([apow[:, t_len].real, apow[:, t_len].imag], axis=1)
    d_tab = jnp.tile(d_skip.astype(F32).reshape(g, 1, SSM_GROUP), (1, t_len, 1)).reshape(g, 1, t_len * SSM_GROUP)
    return (m_tab.astype(BF16), w_c.real.astype(BF16), w_c.imag.astype(BF16), v_tab.astype(BF16), a_t, d_tab)


def _ssm(u, tables, bsz, seq):
    m_tab, w_re, w_im, v_tab, a_t, d_tab = tables
    g = m_tab.shape[0]
    t_len = SSM_T
    assert seq % t_len == 0
    n_chunks = seq // t_len
    rows, cols = n_chunks * bsz, t_len * SSM_GROUP
    u_t = u.reshape(bsz, n_chunks, t_len, g, SSM_GROUP).transpose(3, 1, 0, 2, 4).reshape(g, rows, cols)
    per_group = lambda shape: pl.BlockSpec((1,) + shape, lambda i: (i, 0, 0))
    y_t = pl.pallas_call(
        functools.partial(_ssm_kernel, n_chunks=n_chunks, bsz=bsz),
        out_shape=jax.ShapeDtypeStruct((g, rows, cols), F32),
        grid=(g,),
        in_specs=[per_group((rows, cols)), per_group((cols, cols)), per_group((cols, STATE)),
                  per_group((cols, STATE)), per_group((2 * STATE, cols)), per_group((2, STATE)),
                  per_group((1, cols))],
        out_specs=per_group((rows, cols)),
        scratch_shapes=[pltpu.VMEM((rows, 2 * STATE), F32)],
        compiler_params=pltpu.CompilerParams(dimension_semantics=("parallel",),
                                             vmem_limit_bytes=V7X_VMEM_LIMIT_BYTES),
        name="ssm",
    )(u_t, m_tab, w_re, w_im, v_tab, a_t, d_tab)
    return y_t.reshape(g, n_chunks, bsz, t_len, SSM_GROUP).transpose(2, 1, 3, 0, 4).reshape(bsz * seq, g * SSM_GROUP)


def _t5_bucket_np(rel):
    half = N_BUCKETS // 2
    max_exact = half // 2
    ret = np.where(rel > 0, half, 0)
    n = np.abs(rel)
    n_f = np.maximum(n, 1).astype(np.float32)
    large = max_exact + (np.log(n_f / np.float32(max_exact)) / np.float32(math.log(MAX_DIST / max_exact))
                         * (half - max_exact)).astype(np.int32)
    large = np.minimum(large, half - 1)
    return ret + np.where(n < max_exact, n, large)


def _near_bias_tables(rel_bias, tq, tk):
    kk = np.arange(tk)[:, None]
    qq = np.arange(tq)[None, :]
    b_diag = _t5_bucket_np(kk - qq)
    b_prev = _t5_bucket_np(kk - qq - tk)
    far = int(_t5_bucket_np(np.array(-(tk + 1))))
    assert far == int(_t5_bucket_np(np.array(-(1 << 24)))), "far keys must share one bucket"
    tab = rel_bias.astype(F32)
    one_hot = jax.nn.one_hot(np.stack([b_diag, b_prev]), N_BUCKETS, dtype=F32)
    near = jnp.einsum("wkqb,bh->hwkq", one_hot, tab - tab[far][None, :], precision=lax.Precision.HIGHEST)
    return near * LOG2E


def _sortable(x):
    b = pltpu.bitcast(x, I32)
    return b ^ ((b >> 31) & 0x7FFFFFFF)


def _pair_lanes(h):
    return slice((h // 2) * 2 * HEAD_DIM, (h // 2 + 1) * 2 * HEAD_DIM)


def _dsa_kernel(qit_ref, wi_ref, q_ref, bound_ref, kidx_ref, k_ref, vt_ref, near_ref, o_ref,
                key_scr, k16_scr, mb_scr, qh_scr, m_scr, p_scr, acc_scr, out_scr, *, tq, tk, n_sel):
    i = pl.program_id(1)
    n_tiles = i + 1
    tile_rows = lambda t: pl.ds(pl.multiple_of(t * tk, tk), tk)

    def score_tile(t):
        kid = kidx_ref[tile_rows(t), :]
        acc = jnp.zeros((tk, tq), F32)
        for h in range(IDX_HEADS):
            s = jnp.dot(kid, qit_ref[h * IDX_DIM:(h + 1) * IDX_DIM, :], preferred_element_type=F32)
            acc = acc + wi_ref[h:h + 1, :] * jnp.maximum(s, 0.0)
        return acc

    def store_keys(t, score):
        key = _sortable(score)
        key_scr[tile_rows(t), :] = key
        k16_scr[tile_rows(t), :] = (key >> 16).astype(I16)

    def score_body(t, carry):
        store_keys(t, score_tile(t) + 0.0)
        return carry

    lax.fori_loop(0, i, score_body, 0)
    kpos = lax.broadcasted_iota(I32, (tk, tq), 0)
    qpos = lax.broadcasted_iota(I32, (tk, tq), 1)
    admissible = (kpos // CHUNK) <= (qpos // CHUNK)
    store_keys(i, jnp.where(admissible, score_tile(i) + 0.0, -jnp.inf))

    def count16(cand):
        c16 = cand.astype(I16)

        def body(t, c):
            hit = jnp.where(k16_scr[tile_rows(t), :] >= c16, jnp.int16(1), jnp.int16(0))
            for r in range(tk // 16):
                c = c + hit[r * 16:(r + 1) * 16]
            return c

        part = lax.fori_loop(0, n_tiles, body, jnp.zeros((16, tq), I16))
        return jnp.sum(part.astype(I32), axis=0, keepdims=True)

    def radix16(c_low):
        digit = jnp.zeros((1, tq), I32)
        for b in range(15, -1, -1):
            cand = digit | (1 << b)
            cnt = count16(cand - HALF16)
            accept = cnt >= n_sel
            digit = jnp.where(accept, cand, digit)
            c_low = jnp.where(accept, cnt, c_low)
        return digit, c_low

    q_in_blk = lax.broadcasted_iota(I32, (1, tq), 1)
    n_adm = (i * tq + q_in_blk) // CHUNK * CHUNK + CHUNK
    take_all = n_adm <= n_sel

    hi_digit, c_low = radix16(jnp.full((1, tq), 1, I32) * (n_tiles * tk))
    hi16 = hi_digit - HALF16

    def low_digit_body(t, carry):
        key = key_scr[tile_rows(t), :]
        hi = key >> 16
        low = (key & 0xFFFF) - HALF16
        k16_scr[tile_rows(t), :] = jnp.where(hi > hi16, HALF16 - 1, jnp.where(hi < hi16, -HALF16, low)).astype(I16)
        return carry

    lax.fori_loop(0, n_tiles, low_digit_body, 0)
    lo_digit, c_low = radix16(c_low)
    thr = jnp.where(take_all, KEY_NEG_INF + 1, jnp.left_shift(hi16, 16) | lo_digit)
    excess = jnp.where(take_all, 0, c_low - n_sel)
    any_excess = jnp.max(excess) > 0

    @pl.when(jnp.logical_not(any_excess))
    def _():
        def mask_body(t, carry):
            mb_scr[tile_rows(t), :] = jnp.where(key_scr[tile_rows(t), :] >= thr, 0.0, MASK_NEG)
            return carry

        lax.fori_loop(0, n_tiles, mask_body, 0)

    @pl.when(any_excess)
    def _():
        def gt_body(t, c):
            hit = jnp.where(key_scr[tile_rows(t), :] > thr, 1, 0).astype(I32)
            return c + hit.reshape(tk // 8, 8, tq).sum(axis=0)

        n_gt = jnp.sum(lax.fori_loop(0, n_tiles, gt_body, jnp.zeros((8, tq), I32)), axis=0, keepdims=True)
        need = jnp.where(take_all, 2 * tk * n_tiles, n_sel - n_gt).astype(F32)
        r_i = lax.broadcasted_iota(I32, (tk, tk), 0)
        c_i = lax.broadcasted_iota(I32, (tk, tk), 1)
        strict_lower = jnp.where(c_i < r_i, 1.0, 0.0).astype(BF16)

        def mask_body(t, ties_before):
            kt = key_scr[tile_rows(t), :]
            eq = kt == thr
            eq_f = jnp.where(eq, 1.0, 0.0)
            rank = jnp.dot(strict_lower, eq_f.astype(BF16), preferred_element_type=F32) + ties_before
            sel = jnp.logical_or(kt > thr, jnp.logical_and(eq, rank < need))
            mb_scr[tile_rows(t), :] = jnp.where(sel, 0.0, MASK_NEG)
            return ties_before + jnp.sum(eq_f.reshape(tk // 8, 8, tq).sum(axis=0), axis=0, keepdims=True)

        lax.fori_loop(0, n_tiles, mask_body, jnp.zeros((1, tq), F32))

    lane = lax.broadcasted_iota(I32, (tq, 2 * HEAD_DIM), 1)
    for h in range(N_HEADS):
        q_pair = q_ref[:, _pair_lanes(h)]
        own = (lane < HEAD_DIM) if h % 2 == 0 else (lane >= HEAD_DIM)
        qh_scr[h] = jnp.where(own, q_pair, jnp.zeros_like(q_pair))

    def masked_logits(t, h, near_which):
        lg = lax.dot_general(k_ref[tile_rows(t), _pair_lanes(h)], qh_scr[h], NT_DIMS,
                             preferred_element_type=F32) + mb_scr[tile_rows(t), :]
        if near_which is not None:
            lg = lg + near_ref[h, near_which]
        return lg

    def over_tiles(tile_fn):
        def far_body(t, carry):
            tile_fn(t, None)
            return carry

        lax.fori_loop(0, jnp.maximum(i - 1, 0), far_body, 0)

        @pl.when(i >= 1)
        def _():
            tile_fn(i - 1, 1)

        tile_fn(i, 0)

    def pv_tile_against(shift_row):
        def tile_fn(t, near_which):
            for h in range(N_HEADS):
                p_scr[h] = jnp.exp2(masked_logits(t, h, near_which) - shift_row(h)).astype(BF16)
            for h in range(N_HEADS):
                vrows = slice(h * V_ROWS, (h + 1) * V_ROWS)
                acc_scr[vrows, :] += jnp.dot(vt_ref[vrows, tile_rows(t)], p_scr[h], preferred_element_type=F32)
        return tile_fn

    den_row = lambda h: acc_scr[h * V_ROWS + HEAD_DIM:h * V_ROWS + HEAD_DIM + 1, :]

    acc_scr[...] = jnp.zeros(acc_scr.shape, F32)
    over_tiles(pv_tile_against(lambda h: bound_ref[h:h + 1, :]))
    den_min = den_row(0)
    for h in range(1, N_HEADS):
        den_min = jnp.minimum(den_min, den_row(h))
    bound_too_loose = jnp.logical_not(jnp.min(den_min) >= MIN_SAFE_DEN)

    @pl.when(bound_too_loose)
    def _():
        m_scr[...] = jnp.full(m_scr.shape, MASK_NEG, F32)

        def max_tile(t, near_which):
            for h in range(N_HEADS):
                part = masked_logits(t, h, near_which).reshape(tk // 8, 8, tq).max(axis=0)
                m_scr[h * 8:(h + 1) * 8, :] = jnp.maximum(m_scr[h * 8:(h + 1) * 8, :], part)

        over_tiles(max_tile)
        for h in range(N_HEADS):
            m_scr[h * 8:(h + 1) * 8, :] = jnp.broadcast_to(
                jnp.max(m_scr[h * 8:(h + 1) * 8, :], axis=0, keepdims=True), (8, tq))
        acc_scr[...] = jnp.zeros(acc_scr.shape, F32)
        over_tiles(pv_tile_against(lambda h: m_scr[h * 8:h * 8 + 1, :]))

    for h in range(N_HEADS):
        out_scr[h * HEAD_DIM:(h + 1) * HEAD_DIM, :] = acc_scr[h * V_ROWS:h * V_ROWS + HEAD_DIM, :] / den_row(h)
    o_ref[...] = out_scr[...].T


def _logit_bound(q, k, near, bsz, seq):
    n = q.shape[0]
    qn = jnp.sqrt(jnp.sum(jnp.square(q.astype(F32)).reshape(n, N_HEADS, HEAD_DIM), axis=-1))
    kn = jnp.sqrt(jnp.sum(jnp.square(k.astype(F32)).reshape(bsz, seq, N_HEADS, HEAD_DIM), axis=-1))
    k_max = jnp.max(kn, axis=1)
    near_max = jnp.maximum(jnp.max(near, axis=(1, 2, 3)), 0.0)
    bound = qn.reshape(bsz, seq, N_HEADS) * k_max[:, None, :] * BOUND_SLACK + near_max + 1.0
    return bound.reshape(n, N_HEADS).T


def _dsa(q, k, kidx, vt, qit, kwt, near, bsz, seq):
    d_attn = q.shape[1]
    tq, tk = DSA_TQ, DSA_TK
    assert tq == tk and seq % tq == 0 and tq % CHUNK == 0 and IDX_DIM % IDX_HEADS == 0
    n_sel = min(TOPK_MAX, seq // 4)
    nq = seq // tq
    bound = _logit_bound(q, k, near, bsz, seq)
    once = dict(pipeline_mode=pl.Buffered(1))
    return pl.pallas_call(
        functools.partial(_dsa_kernel, tq=tq, tk=tk, n_sel=n_sel),
        out_shape=jax.ShapeDtypeStruct((bsz * seq, d_attn), F32),
        grid=(bsz, nq),
        in_specs=[pl.BlockSpec((IDX_HEADS * IDX_DIM, tq), lambda b, i: (0, b * nq + i)),
                  pl.BlockSpec((IDX_HEADS, tq), lambda b, i: (IDX_DIM // IDX_HEADS, b * nq + i)),
                  pl.BlockSpec((tq, d_attn), lambda b, i: (b * nq + i, 0)),
                  pl.BlockSpec((N_HEADS, tq), lambda b, i: (0, b * nq + i)),
                  pl.BlockSpec((seq, IDX_DIM), lambda b, i: (b, 0), **once),
                  pl.BlockSpec((seq, d_attn), lambda b, i: (b, 0), **once),
                  pl.BlockSpec((N_HEADS * V_ROWS, seq), lambda b, i: (0, b), **once),
                  _resident(near.shape)],
        out_specs=pl.BlockSpec((tq, d_attn), lambda b, i: (b * nq + i, 0)),
        scratch_shapes=[pltpu.VMEM((seq, tq), I32), pltpu.VMEM((seq, tq), I16), pltpu.VMEM((seq, tq), F32),
                        pltpu.VMEM((N_HEADS, tq, 2 * HEAD_DIM), BF16), pltpu.VMEM((N_HEADS * 8, tq), F32),
                        pltpu.VMEM((N_HEADS, tk, tq), BF16),
                        pltpu.VMEM((N_HEADS * V_ROWS, tq), F32), pltpu.VMEM((d_attn, tq), F32)],
        compiler_params=pltpu.CompilerParams(dimension_semantics=("parallel", "arbitrary"),
                                             vmem_limit_bytes=V7X_VMEM_LIMIT_BYTES),
        name="dsa",
    )(qit, kwt, q, bound, kidx, k, vt, near)


def _merge_ln_kernel(x_ref, ys_ref, ya_ref, gs_ref, ga_ref, wglu_ref, wb0_ref, wb1_ref, wout_ref,
                     g_ref, b_ref, o_ref):
    y = jax.nn.gelu(ys_ref[...])
    glu = y * jax.nn.sigmoid(jnp.dot(y.astype(BF16), wglu_ref[...], preferred_element_type=F32))
    p_ssm = jnp.dot(glu.astype(BF16), wb0_ref[...], preferred_element_type=F32)
    p_att = jnp.dot(ya_ref[...].astype(BF16), wb1_ref[...], preferred_element_type=F32)
    merged = gs_ref[...].astype(F32) * p_ssm + ga_ref[...].astype(F32) * p_att
    mix = jnp.dot(merged.astype(BF16), wout_ref[...], preferred_element_type=F32)
    o_ref[...] = _layer_norm(ALPHA * x_ref[...] + mix, g_ref[...], b_ref[...])


def _merge_ln(x2, y_ssm, y_attn, gs, ga, w_glu, wb0, wb1, w_out, g, b):
    n, d = x2.shape
    tm = min(MERGE_TM, n)
    assert n % tm == 0
    tile = lambda a: pl.BlockSpec((tm, a.shape[1]), lambda i: (i, 0))
    return pl.pallas_call(
        _merge_ln_kernel,
        out_shape=jax.ShapeDtypeStruct((n, d), F32),
        grid=(n // tm,),
        in_specs=[tile(x2), tile(y_ssm), tile(y_attn), tile(gs), tile(ga),
                  _resident(w_glu.shape), _resident(wb0.shape), _resident(wb1.shape), _resident(w_out.shape),
                  _resident((1, d)), _resident((1, d))],
        out_specs=pl.BlockSpec((tm, d), lambda i: (i, 0)),
        compiler_params=pltpu.CompilerParams(dimension_semantics=("parallel",),
                                             vmem_limit_bytes=V7X_VMEM_LIMIT_BYTES),
        name="merge_ln",
    )(x2, y_ssm, y_attn, gs, ga, w_glu, wb0, wb1, w_out, g.reshape(1, d), b.reshape(1, d))


def _pack_cols(parts):
    offs, pos = {}, 0
    for name, w in parts.items():
        offs[name] = (pos, pos + w.shape[1])
        pos += w.shape[1]
    return jnp.concatenate(list(parts.values()), axis=1), offs


def _split_w_in(w_in, d_model, d_ssm, d_attn):
    sizes = (d_ssm, d_attn, d_attn, d_attn, IDX_HEADS * IDX_DIM, IDX_DIM, IDX_HEADS, d_model, d_model)
    assert w_in.shape[1] == sum(sizes)
    edges = np.concatenate([[0], np.cumsum(sizes)])
    u, q, k, v, qi, ki, wi, gs, ga = [w_in[:, int(edges[j]):int(edges[j + 1])] for j in range(len(sizes))]
    zeros = lambda c: jnp.zeros((w_in.shape[0], c), w_in.dtype)
    w_nat, offs = _pack_cols(dict(u=u, q=q, k=k, kidx=jnp.concatenate([ki, zeros(LANES - IDX_DIM)], axis=1),
                                  gs=gs, ga=ga))
    kw = jnp.concatenate([ki, wi, zeros(LANES - IDX_DIM - IDX_HEADS)], axis=1)
    w_t, offs_t = _pack_cols(dict(v=v, qi=qi, kw=kw))
    return w_nat.astype(BF16), offs, w_t.T.astype(BF16), offs_t


def kernel(x, ffn1_w_up, ffn1_w_down, ln1_g, ln1_b, w_in, ssm_lam_re, ssm_lam_im, ssm_log_dt, ssm_b_re, ssm_b_im, ssm_c_re, ssm_c_im, ssm_d, ssm_w_glu, w_branch, w_out, ln2_g, ln2_b, ffn2_w_up, ffn2_w_down, ln3_g, ln3_b, rel_bias):
    bsz, seq, d_model = x.shape
    d_ssm = ssm_w_glu.shape[1]
    d_attn = w_branch.shape[2]
    x2 = x.reshape(bsz * seq, d_model)
    near = _near_bias_tables(rel_bias, DSA_TQ, DSA_TK)
    for l in range(ffn1_w_up.shape[0]):
        x2 = _ffn_ln(x2, ffn1_w_up[l].astype(BF16), ffn1_w_down[l].astype(BF16), ln1_g[l], ln1_b[l])
        u, q, k, kidx, gs, ga, vt, qit, kwt = _in_proj(x2, *_split_w_in(w_in[l], d_model, d_ssm, d_attn))
        tables = _ssm_tables(ssm_lam_re[l], ssm_lam_im[l], ssm_log_dt[l], ssm_b_re[l], ssm_b_im[l],
                             ssm_c_re[l], ssm_c_im[l], ssm_d[l], SSM_T)
        y_ssm = _ssm(u, tables, bsz, seq)
        y_attn = _dsa(q, k, kidx, vt, qit, kwt, near, bsz, seq)
        x2 = _merge_ln(x2, y_ssm, y_attn, gs, ga, ssm_w_glu[l].astype(BF16), w_branch[l, 0].astype(BF16),
                       w_branch[l, 1].astype(BF16), w_out[l].astype(BF16), ln2_g[l], ln2_b[l])
        x2 = _ffn_ln(x2, ffn2_w_up[l].astype(BF16), ffn2_w_down[l].astype(BF16), ln3_g[l], ln3_b[l])
    return x2.reshape(bsz, seq, d_model)
```

```python
import functools
import math

import numpy as np
import jax
import jax.numpy as jnp
from jax import lax
from jax.experimental import pallas as pl
from jax.experimental.pallas import tpu as pltpu

F32 = jnp.float32
BF16 = jnp.bfloat16
I32 = jnp.int32
I16 = jnp.int16

DEPTH = 2
CHUNK = 64
SSM_GROUP = 16
STATE = 64
N_HEADS = 8
HEAD_DIM = 64
IDX_HEADS = 8
IDX_DIM = 32
TOPK_MAX = 256
ATTN_SCALE = HEAD_DIM ** -0.5
LOG2E = math.log2(math.e)
N_BUCKETS = 32
MAX_DIST = 128
ALPHA = (2 * DEPTH) ** 0.25
LN_EPS = 1e-5

V7X_VMEM_LIMIT_BYTES = 58 * 1024 * 1024
LANES = 128
FFN_TM = 512
FFN_FC = 256
PROJ_TM = 512
MERGE_TM = 512
SSM_T = 32
DSA_TQ = 256
DSA_TK = 256
COUNT_TILES = 2
MASK_NEG = -1e30
V_ROWS = HEAD_DIM + 16
KEY_NEG_INF = int(np.int32(np.uint32(0x807FFFFF)))
HALF16 = 1 << 15
MIN_SAFE_DEN = 2.0 ** -80
BOUND_SLACK = 1.02


def _resident(shape):
    nd = len(shape)
    return pl.BlockSpec(shape, lambda *_: (0,) * nd, pipeline_mode=pl.Buffered(1))


def _layer_norm(y, g, b):
    mu = jnp.mean(y, axis=-1, keepdims=True)
    yc = y - mu
    var = jnp.mean(yc * yc, axis=-1, keepdims=True)
    return yc * lax.rsqrt(var + LN_EPS) * g + b


NT_DIMS = (((1,), (1,)), ((), ()))


def _ffn_ln_kernel(x_ref, wup_ref, wdn_ref, g_ref, b_ref, o_ref, *, d_ff, fc):
    x = x_ref[...]
    xb = x.astype(BF16)
    acc = jnp.zeros(x.shape, F32)
    for j in range(d_ff // fc):
        gate = jnp.dot(xb, wup_ref[:, j * fc:(j + 1) * fc], preferred_element_type=F32)
        up = jnp.dot(xb, wup_ref[:, d_ff + j * fc:d_ff + (j + 1) * fc], preferred_element_type=F32)
        a = (gate * jax.nn.sigmoid(gate)) * up
        acc = acc + jnp.dot(a.astype(BF16), wdn_ref[j * fc:(j + 1) * fc, :], preferred_element_type=F32)
    o_ref[...] = _layer_norm(ALPHA * x + 0.5 * acc, g_ref[...], b_ref[...])


def _ffn_ln(x2, w_up, w_down, g, b):
    n, d = x2.shape
    d_ff = w_down.shape[0]
    tm = min(FFN_TM, n)
    assert n % tm == 0 and d_ff % FFN_FC == 0
    return pl.pallas_call(
        functools.partial(_ffn_ln_kernel, d_ff=d_ff, fc=FFN_FC),
        out_shape=jax.ShapeDtypeStruct((n, d), F32),
        grid=(n // tm,),
        in_specs=[pl.BlockSpec((tm, d), lambda i: (i, 0)),
                  _resident(w_up.shape), _resident(w_down.shape),
                  _resident((1, d)), _resident((1, d))],
        out_specs=pl.BlockSpec((tm, d), lambda i: (i, 0)),
        compiler_params=pltpu.CompilerParams(dimension_semantics=("parallel",),
                                             vmem_limit_bytes=V7X_VMEM_LIMIT_BYTES),
        name="ffn_ln",
    )(x2, w_up, w_down, g.reshape(1, d), b.reshape(1, d))


def _in_proj_kernel(x_ref, w_ref, wt_ref, u_ref, q_ref, k_ref, kidx_ref, gs_ref, ga_ref,
                    vt_ref, qit_ref, kwt_ref, *, offs, offs_t):
    xb = x_ref[...].astype(BF16)

    def proj(name):
        lo, hi = offs[name]
        return jnp.dot(xb, w_ref[:, lo:hi], preferred_element_type=F32)

    def proj_t(name):
        lo, hi = offs_t[name]
        return lax.dot_general(wt_ref[lo:hi, :], xb, NT_DIMS, preferred_element_type=F32)

    u_ref[...] = proj("u")
    q_ref[...] = (proj("q") * (ATTN_SCALE * LOG2E)).astype(BF16)
    k_ref[...] = proj("k").astype(BF16)
    kidx_ref[...] = proj("kidx")[:, :IDX_DIM].astype(BF16)
    gs_ref[...] = jax.nn.sigmoid(proj("gs")).astype(BF16)
    ga_ref[...] = jax.nn.sigmoid(proj("ga")).astype(BF16)
    qit_ref[...] = proj_t("qi").astype(BF16)
    kwt_ref[...] = proj_t("kw")
    v_t = proj_t("v").astype(BF16)
    tm = v_t.shape[1]
    pad_rows = lax.broadcasted_iota(I32, (V_ROWS - HEAD_DIM, tm), 0)
    pad = jnp.where(pad_rows == 0, 1.0, 0.0).astype(BF16)
    for h in range(N_HEADS):
        vt_ref[h * V_ROWS:h * V_ROWS + HEAD_DIM, :] = v_t[h * HEAD_DIM:(h + 1) * HEAD_DIM, :]
        vt_ref[h * V_ROWS + HEAD_DIM:(h + 1) * V_ROWS, :] = pad


def _in_proj(x2, w_nat, offs, w_t, offs_t):
    n, d = x2.shape
    tm = min(PROJ_TM, n)
    assert n % tm == 0
    width = lambda name: offs[name][1] - offs[name][0]
    rows_t = lambda name: offs_t[name][1] - offs_t[name][0]
    tok = lambda w, dt: (jax.ShapeDtypeStruct((n, w), dt), pl.BlockSpec((tm, w), lambda i: (i, 0)))
    chan = lambda r, dt: (jax.ShapeDtypeStruct((r, n), dt), pl.BlockSpec((r, tm), lambda i: (0, i)))
    outs = [tok(width("u"), F32), tok(width("q"), BF16), tok(width("k"), BF16), tok(IDX_DIM, BF16),
            tok(width("gs"), BF16), tok(width("ga"), BF16),
            chan(N_HEADS * V_ROWS, BF16), chan(rows_t("qi"), BF16), chan(rows_t("kw"), F32)]
    return pl.pallas_call(
        functools.partial(_in_proj_kernel, offs=offs, offs_t=offs_t),
        out_shape=[o[0] for o in outs],
        grid=(n // tm,),
        in_specs=[pl.BlockSpec((tm, d), lambda i: (i, 0)), _resident(w_nat.shape), _resident(w_t.shape)],
        out_specs=[o[1] for o in outs],
        compiler_params=pltpu.CompilerParams(dimension_semantics=("parallel",),
                                             vmem_limit_bytes=V7X_VMEM_LIMIT_BYTES),
        name="in_proj",
    )(x2, w_nat, w_t)


def _ssm_kernel(u_ref, m_ref, wre_ref, wim_ref, v_ref, at_ref, d_ref, y_ref, hin_scr, *, n_chunks, bsz):
    u = u_ref[0]
    ub = u.astype(BF16)
    s_re = jnp.dot(ub, wre_ref[0], preferred_element_type=F32)
    s_im = jnp.dot(ub, wim_ref[0], preferred_element_type=F32)
    a_re = at_ref[0, 0:1, :]
    a_im = at_ref[0, 1:2, :]
    h_re = jnp.zeros((bsz, STATE), F32)
    h_im = jnp.zeros((bsz, STATE), F32)
    for c in range(n_chunks):
        rows = slice(c * bsz, (c + 1) * bsz)
        hin_scr[rows, 0:STATE] = h_re
        hin_scr[rows, STATE:2 * STATE] = h_im
        n_re = a_re * h_re - a_im * h_im + s_re[rows]
        n_im = a_re * h_im + a_im * h_re + s_im[rows]
        h_re, h_im = n_re, n_im
    y = jnp.dot(ub, m_ref[0], preferred_element_type=F32)
    y = y + jnp.dot(hin_scr[...].astype(BF16), v_ref[0], preferred_element_type=F32)
    y_ref[0] = y + d_ref[0] * u


def _ssm_tables(lam_re, lam_im, log_dt, b_re, b_im, c_re, c_im, d_skip, t_len):
    g = lam_re.shape[0]
    lam = lax.complex(jnp.minimum(lam_re.astype(F32), -1e-4), lam_im.astype(F32))
    dt = jnp.exp(log_dt.astype(F32))[:, None]
    a_bar = jnp.exp(lam * dt)
    b_bar = ((a_bar - 1.0) / lam)[:, :, None] * lax.complex(b_re.astype(F32), b_im.astype(F32))
    c_mat = lax.complex(c_re.astype(F32), c_im.astype(F32))
    taus = jnp.arange(t_len + 1, dtype=F32)
    apow = jnp.exp((lam * dt)[:, None, :] * taus[None, :, None])
    hp = lax.Precision.HIGHEST
    kern = jnp.einsum("ghp,gtp,gpk->gkth", c_mat, apow[:, :t_len], b_bar, precision=hp).real
    row_len = 2 * t_len * SSM_GROUP
    padded = jnp.concatenate([jnp.zeros_like(kern), kern], axis=2).reshape(g, SSM_GROUP, row_len)
    period = jnp.pad(padded, ((0, 0), (0, 0), (0, SSM_GROUP)))
    skewed = jnp.tile(period, (1, 1, t_len))[:, :, :t_len * row_len].reshape(g, SSM_GROUP, t_len, row_len)
    m_tab = skewed[:, :, :, row_len // 2:].transpose(0, 2, 1, 3)
    m_tab = m_tab.reshape(g, t_len * SSM_GROUP, t_len * SSM_GROUP)
    w_c = apow[:, :t_len][:, ::-1, None, :] * b_bar.transpose(0, 2, 1)[:, None, :, :]
    w_c = w_c.reshape(g, t_len * SSM_GROUP, STATE)
    v_c = c_mat.transpose(0, 2, 1)[:, :, None, :] * apow[:, 1:t_len + 1].transpose(0, 2, 1)[:, :, :, None]
    v_c = v_c.reshape(g, STATE, t_len * SSM_GROUP)
    v_tab = jnp.concatenate([v_c.real, -v_c.imag], axis=1)
    a_t = jnp.stack([apow[:, t_len].real, apow[:, t_len].imag], axis=1)
    d_tab = jnp.tile(d_skip.astype(F32).reshape(g, 1, SSM_GROUP), (1, t_len, 1)).reshape(g, 1, t_len * SSM_GROUP)
    return (m_tab.astype(BF16), w_c.real.astype(BF16), w_c.imag.astype(BF16), v_tab.astype(BF16), a_t, d_tab)


def _ssm(u, tables, bsz, seq):
    m_tab, w_re, w_im, v_tab, a_t, d_tab = tables
    g = m_tab.shape[0]
    t_len = SSM_T
    assert seq % t_len == 0
    n_chunks = seq // t_len
    rows, cols = n_chunks * bsz, t_len * SSM_GROUP
    u_t = u.reshape(bsz, n_chunks, t_len, g, SSM_GROUP).transpose(3, 1, 0, 2, 4).reshape(g, rows, cols)
    per_group = lambda shape: pl.BlockSpec((1,) + shape, lambda i: (i, 0, 0))
    y_t = pl.pallas_call(
        functools.partial(_ssm_kernel, n_chunks=n_chunks, bsz=bsz),
        out_shape=jax.ShapeDtypeStruct((g, rows, cols), F32),
        grid=(g,),
        in_specs=[per_group((rows, cols)), per_group((cols, cols)), per_group((cols, STATE)),
                  per_group((cols, STATE)), per_group((2 * STATE, cols)), per_group((2, STATE)),
                  per_group((1, cols))],
        out_specs=per_group((rows, cols)),
        scratch_shapes=[pltpu.VMEM((rows, 2 * STATE), F32)],
        compiler_params=pltpu.CompilerParams(dimension_semantics=("parallel",),
                                             vmem_limit_bytes=V7X_VMEM_LIMIT_BYTES),
        name="ssm",
    )(u_t, m_tab, w_re, w_im, v_tab, a_t, d_tab)
    return y_t.reshape(g, n_chunks, bsz, t_len, SSM_GROUP).transpose(2, 1, 3, 0, 4).reshape(bsz * seq, g * SSM_GROUP)


def _t5_bucket_np(rel):
    half = N_BUCKETS // 2
    max_exact = half // 2
    ret = np.where(rel > 0, half, 0)
    n = np.abs(rel)
    n_f = np.maximum(n, 1).astype(np.float32)
    large = max_exact + (np.log(n_f / np.float32(max_exact)) / np.float32(math.log(MAX_DIST / max_exact))
                         * (half - max_exact)).astype(np.int32)
    large = np.minimum(large, half - 1)
    return ret + np.where(n < max_exact, n, large)


def _near_bias_tables(rel_bias, tq, tk):
    kk = np.arange(tk)[:, None]
    qq = np.arange(tq)[None, :]
    b_diag = _t5_bucket_np(kk - qq)
    b_prev = _t5_bucket_np(kk - qq - tk)
    far = int(_t5_bucket_np(np.array(-(tk + 1))))
    assert far == int(_t5_bucket_np(np.array(-(1 << 24)))), "far keys must share one bucket"
    tab = rel_bias.astype(F32)
    one_hot = jax.nn.one_hot(np.stack([b_diag, b_prev]), N_BUCKETS, dtype=F32)
    near = jnp.einsum("wkqb,bh->hwkq", one_hot, tab - tab[far][None, :], precision=lax.Precision.HIGHEST)
    return near * LOG2E


def _sortable(x):
    b = pltpu.bitcast(x, I32)
    return b ^ ((b >> 31) & 0x7FFFFFFF)


def _pair_lanes(h):
    return slice((h // 2) * 2 * HEAD_DIM, (h // 2 + 1) * 2 * HEAD_DIM)


def _dsa_kernel(qit_ref, wi_ref, q_ref, bound_ref, kidx_ref, k_ref, vt_ref, near_ref, o_ref,
                key_scr, k16_scr, mb_scr, qh_scr, m_scr, p_scr, acc_scr, out_scr, *, tq, tk, n_sel):
    i = pl.program_id(1)
    n_tiles = i + 1
    tile_rows = lambda t: pl.ds(pl.multiple_of(t * tk, tk), tk)

    def score_tile(t):
        kid = kidx_ref[tile_rows(t), :]
        acc = jnp.zeros((tk, tq), F32)
        for h in range(IDX_HEADS):
            s = jnp.dot(kid, qit_ref[h * IDX_DIM:(h + 1) * IDX_DIM, :], preferred_element_type=F32)
            acc = acc + wi_ref[h:h + 1, :] * jnp.maximum(s, 0.0)
        return acc

    def store_keys(t, score):
        key = _sortable(score)
        key_scr[tile_rows(t), :] = key
        k16_scr[tile_rows(t), :] = (key >> 16).astype(I16)

    def for_tiles(n, tile_fn):
        def pair_body(j, carry):
            tile_fn(2 * j)
            tile_fn(2 * j + 1)
            return carry

        lax.fori_loop(0, n // 2, pair_body, 0)

        @pl.when(n % 2 == 1)
        def _():
            tile_fn(n - 1)

    for_tiles(i, lambda t: store_keys(t, score_tile(t) + 0.0))
    kpos = lax.broadcasted_iota(I32, (tk, tq), 0)
    qpos = lax.broadcasted_iota(I32, (tk, tq), 1)
    admissible = (kpos // CHUNK) <= (qpos // CHUNK)
    store_keys(i, jnp.where(admissible, score_tile(i) + 0.0, -jnp.inf))
    for extra in range(COUNT_TILES - 1):
        k16_scr[tile_rows(n_tiles + extra), :] = jnp.full((tk, tq), -HALF16, I16)

    def count16(cand):
        c16 = cand.astype(I16)

        def group_body(j, c):
            rows = pl.ds(pl.multiple_of(j * (COUNT_TILES * tk), COUNT_TILES * tk), COUNT_TILES * tk)
            hit = jnp.where(k16_scr[rows, :] >= c16, jnp.int16(1), jnp.int16(0))
            parts = [hit[r * 16:(r + 1) * 16] for r in range(COUNT_TILES * tk // 16)]
            while len(parts) > 1:
                parts = [a + b for a, b in zip(parts[0::2], parts[1::2])]
            return c + parts[0]

        n_groups = (n_tiles + COUNT_TILES - 1) // COUNT_TILES
        part = lax.fori_loop(0, n_groups, group_body, jnp.zeros((16, tq), I16))
        return jnp.sum(part.astype(I32), axis=0, keepdims=True)

    def radix16(c_low):
        digit = jnp.zeros((1, tq), I32)
        for b in range(15, -1, -1):
            cand = digit | (1 << b)
            cnt = count16(cand - HALF16)
            accept = cnt >= n_sel
            digit = jnp.where(accept, cand, digit)
            c_low = jnp.where(accept, cnt, c_low)
        return digit, c_low

    q_in_blk = lax.broadcasted_iota(I32, (1, tq), 1)
    n_adm = (i * tq + q_in_blk) // CHUNK * CHUNK + CHUNK
    take_all = n_adm <= n_sel

    hi_digit, c_low = radix16(jnp.full((1, tq), 1, I32) * (n_tiles * tk))
    hi16 = hi_digit - HALF16

    def low_digit_tile(t):
        key = key_scr[tile_rows(t), :]
        hi = key >> 16
        low = (key & 0xFFFF) - HALF16
        k16_scr[tile_rows(t), :] = jnp.where(hi > hi16, HALF16 - 1, jnp.where(hi < hi16, -HALF16, low)).astype(I16)

    for_tiles(n_tiles, low_digit_tile)
    lo_digit, c_low = radix16(c_low)
    thr = jnp.where(take_all, KEY_NEG_INF + 1, jnp.left_shift(hi16, 16) | lo_digit)
    excess = jnp.where(take_all, 0, c_low - n_sel)
    any_excess = jnp.max(excess) > 0

    @pl.when(jnp.logical_not(any_excess))
    def _():
        def mask_tile(t):
            mb_scr[tile_rows(t), :] = jnp.where(key_scr[tile_rows(t), :] >= thr, 0.0, MASK_NEG)

        for_tiles(n_tiles, mask_tile)

    @pl.when(any_excess)
    def _():
        def gt_body(t, c):
            hit = jnp.where(key_scr[tile_rows(t), :] > thr, 1, 0).astype(I32)
            return c + hit.reshape(tk // 8, 8, tq).sum(axis=0)

        n_gt = jnp.sum(lax.fori_loop(0, n_tiles, gt_body, jnp.zeros((8, tq), I32)), axis=0, keepdims=True)
        need = jnp.where(take_all, 2 * tk * n_tiles, n_sel - n_gt).astype(F32)
        r_i = lax.broadcasted_iota(I32, (tk, tk), 0)
        c_i = lax.broadcasted_iota(I32, (tk, tk), 1)
        strict_lower = jnp.where(c_i < r_i, 1.0, 0.0).astype(BF16)

        def mask_body(t, ties_before):
            kt = key_scr[tile_rows(t), :]
            eq = kt == thr
            eq_f = jnp.where(eq, 1.0, 0.0)
            rank = jnp.dot(strict_lower, eq_f.astype(BF16), preferred_element_type=F32) + ties_before
            sel = jnp.logical_or(kt > thr, jnp.logical_and(eq, rank < need))
            mb_scr[tile_rows(t), :] = jnp.where(sel, 0.0, MASK_NEG)
            return ties_before + jnp.sum(eq_f.reshape(tk // 8, 8, tq).sum(axis=0), axis=0, keepdims=True)

        lax.fori_loop(0, n_tiles, mask_body, jnp.zeros((1, tq), F32))

    lane = lax.broadcasted_iota(I32, (tq, 2 * HEAD_DIM), 1)
    for h in range(N_HEADS):
        q_pair = q_ref[:, _pair_lanes(h)]
        own = (lane < HEAD_DIM) if h % 2 == 0 else (lane >= HEAD_DIM)
        qh_scr[h] = jnp.where(own, q_pair, jnp.zeros_like(q_pair))

    def masked_logits(t, h, near_which):
        lg = lax.dot_general(k_ref[tile_rows(t), _pair_lanes(h)], qh_scr[h], NT_DIMS,
                             preferred_element_type=F32) + mb_scr[tile_rows(t), :]
        if near_which is not None:
            lg = lg + near_ref[h, near_which]
        return lg

    def over_tiles(tile_fn):
        for_tiles(jnp.maximum(i - 1, 0), lambda t: tile_fn(t, None))

        @pl.when(i >= 1)
        def _():
            tile_fn(i - 1, 1)

        tile_fn(i, 0)

    def pv_tile_against(shift_row):
        def tile_fn(t, near_which):
            for h in range(N_HEADS):
                p_scr[h] = jnp.exp2(masked_logits(t, h, near_which) - shift_row(h)).astype(BF16)
            for h in range(N_HEADS):
                vrows = slice(h * V_ROWS, (h + 1) * V_ROWS)
                acc_scr[vrows, :] += jnp.dot(vt_ref[vrows, tile_rows(t)], p_scr[h], preferred_element_type=F32)
        return tile_fn

    den_row = lambda h: acc_scr[h * V_ROWS + HEAD_DIM:h * V_ROWS + HEAD_DIM + 1, :]

    acc_scr[...] = jnp.zeros(acc_scr.shape, F32)
    over_tiles(pv_tile_against(lambda h: bound_ref[h:h + 1, :]))
    den_min = den_row(0)
    for h in range(1, N_HEADS):
        den_min = jnp.minimum(den_min, den_row(h))
    bound_too_loose = jnp.logical_not(jnp.min(den_min) >= MIN_SAFE_DEN)

    @pl.when(bound_too_loose)
    def _():
        m_scr[...] = jnp.full(m_scr.shape, MASK_NEG, F32)

        def max_tile(t, near_which):
            for h in range(N_HEADS):
                part = masked_logits(t, h, near_which).reshape(tk // 8, 8, tq).max(axis=0)
                m_scr[h * 8:(h + 1) * 8, :] = jnp.maximum(m_scr[h * 8:(h + 1) * 8, :], part)

        over_tiles(max_tile)
        for h in range(N_HEADS):
            m_scr[h * 8:(h + 1) * 8, :] = jnp.broadcast_to(
                jnp.max(m_scr[h * 8:(h + 1) * 8, :], axis=0, keepdims=True), (8, tq))
        acc_scr[...] = jnp.zeros(acc_scr.shape, F32)
        over_tiles(pv_tile_against(lambda h: m_scr[h * 8:h * 8 + 1, :]))

    for h in range(N_HEADS):
        out_scr[h * HEAD_DIM:(h + 1) * HEAD_DIM, :] = acc_scr[h * V_ROWS:h * V_ROWS + HEAD_DIM, :] / den_row(h)
    o_ref[...] = out_scr[...].T


def _logit_bound(q, k, near, bsz, seq):
    n = q.shape[0]
    qn = jnp.sqrt(jnp.sum(jnp.square(q.astype(F32)).reshape(n, N_HEADS, HEAD_DIM), axis=-1))
    kn = jnp.sqrt(jnp.sum(jnp.square(k.astype(F32)).reshape(bsz, seq, N_HEADS, HEAD_DIM), axis=-1))
    k_max = jnp.max(kn, axis=1)
    near_max = jnp.maximum(jnp.max(near, axis=(1, 2, 3)), 0.0)
    bound = qn.reshape(bsz, seq, N_HEADS) * k_max[:, None, :] * BOUND_SLACK + near_max + 1.0
    return bound.reshape(n, N_HEADS).T


def _dsa(q, k, kidx, vt, qit, kwt, near, bsz, seq):
    d_attn = q.shape[1]
    tq, tk = DSA_TQ, DSA_TK
    assert tq == tk and seq % tq == 0 and tq % CHUNK == 0 and IDX_DIM % IDX_HEADS == 0
    n_sel = min(TOPK_MAX, seq // 4)
    nq = seq // tq
    bound = _logit_bound(q, k, near, bsz, seq)
    once = dict(pipeline_mode=pl.Buffered(1))
    return pl.pallas_call(
        functools.partial(_dsa_kernel, tq=tq, tk=tk, n_sel=n_sel),
        out_shape=jax.ShapeDtypeStruct((bsz * seq, d_attn), F32),
        grid=(bsz, nq),
        in_specs=[pl.BlockSpec((IDX_HEADS * IDX_DIM, tq), lambda b, i: (0, b * nq + i)),
                  pl.BlockSpec((IDX_HEADS, tq), lambda b, i: (IDX_DIM // IDX_HEADS, b * nq + i)),
                  pl.BlockSpec((tq, d_attn), lambda b, i: (b * nq + i, 0)),
                  pl.BlockSpec((N_HEADS, tq), lambda b, i: (0, b * nq + i)),
                  pl.BlockSpec((seq, IDX_DIM), lambda b, i: (b, 0), **once),
                  pl.BlockSpec((seq, d_attn), lambda b, i: (b, 0), **once),
                  pl.BlockSpec((N_HEADS * V_ROWS, seq), lambda b, i: (0, b), **once),
                  _resident(near.shape)],
        out_specs=pl.BlockSpec((tq, d_attn), lambda b, i: (b * nq + i, 0)),
        scratch_shapes=[pltpu.VMEM((seq, tq), I32), pltpu.VMEM((seq + (COUNT_TILES - 1) * tk, tq), I16), pltpu.VMEM((seq, tq), F32),
                        pltpu.VMEM((N_HEADS, tq, 2 * HEAD_DIM), BF16), pltpu.VMEM((N_HEADS * 8, tq), F32),
                        pltpu.VMEM((N_HEADS, tk, tq), BF16),
                        pltpu.VMEM((N_HEADS * V_ROWS, tq), F32), pltpu.VMEM((d_attn, tq), F32)],
        compiler_params=pltpu.CompilerParams(dimension_semantics=("parallel", "arbitrary"),
                                             vmem_limit_bytes=V7X_VMEM_LIMIT_BYTES),
        name="dsa",
    )(qit, kwt, q, bound, kidx, k, vt, near)


def _merge_ln_kernel(x_ref, ys_ref, ya_ref, gs_ref, ga_ref, wglu_ref, wb0_ref, wb1_ref, wout_ref,
                     g_ref, b_ref, o_ref):
    y = jax.nn.gelu(ys_ref[...])
    glu = y * jax.nn.sigmoid(jnp.dot(y.astype(BF16), wglu_ref[...], preferred_element_type=F32))
    p_ssm = jnp.dot(glu.astype(BF16), wb0_ref[...], preferred_element_type=F32)
    p_att = jnp.dot(ya_ref[...].astype(BF16), wb1_ref[...], preferred_element_type=F32)
    merged = gs_ref[...].astype(F32) * p_ssm + ga_ref[...].astype(F32) * p_att
    mix = jnp.dot(merged.astype(BF16), wout_ref[...], preferred_element_type=F32)
    o_ref[...] = _layer_norm(ALPHA * x_ref[...] + mix, g_ref[...], b_ref[...])


def _merge_ln(x2, y_ssm, y_attn, gs, ga, w_glu, wb0, wb1, w_out, g, b):
    n, d = x2.shape
    tm = min(MERGE_TM, n)
    assert n % tm == 0
    tile = lambda a: pl.BlockSpec((tm, a.shape[1]), lambda i: (i, 0))
    return pl.pallas_call(
        _merge_ln_kernel,
        out_shape=jax.ShapeDtypeStruct((n, d), F32),
        grid=(n // tm,),
        in_specs=[tile(x2), tile(y_ssm), tile(y_attn), tile(gs), tile(ga),
                  _resident(w_glu.shape), _resident(wb0.shape), _resident(wb1.shape), _resident(w_out.shape),
                  _resident((1, d)), _resident((1, d))],
        out_specs=pl.BlockSpec((tm, d), lambda i: (i, 0)),
        compiler_params=pltpu.CompilerParams(dimension_semantics=("parallel",),
                                             vmem_limit_bytes=V7X_VMEM_LIMIT_BYTES),
        name="merge_ln",
    )(x2, y_ssm, y_attn, gs, ga, w_glu, wb0, wb1, w_out, g.reshape(1, d), b.reshape(1, d))


def _pack_cols(parts):
    offs, pos = {}, 0
    for name, w in parts.items():
        offs[name] = (pos, pos + w.shape[1])
        pos += w.shape[1]
    return jnp.concatenate(list(parts.values()), axis=1), offs


def _split_w_in(w_in, d_model, d_ssm, d_attn):
    sizes = (d_ssm, d_attn, d_attn, d_attn, IDX_HEADS * IDX_DIM, IDX_DIM, IDX_HEADS, d_model, d_model)
    assert w_in.shape[1] == sum(sizes)
    edges = np.concatenate([[0], np.cumsum(sizes)])
    u, q, k, v, qi, ki, wi, gs, ga = [w_in[:, int(edges[j]):int(edges[j + 1])] for j in range(len(sizes))]
    zeros = lambda c: jnp.zeros((w_in.shape[0], c), w_in.dtype)
    w_nat, offs = _pack_cols(dict(u=u, q=q, k=k, kidx=jnp.concatenate([ki, zeros(LANES - IDX_DIM)], axis=1),
                                  gs=gs, ga=ga))
    kw = jnp.concatenate([ki, wi, zeros(LANES - IDX_DIM - IDX_HEADS)], axis=1)
    w_t, offs_t = _pack_cols(dict(v=v, qi=qi, kw=kw))
    return w_nat.astype(BF16), offs, w_t.T.astype(BF16), offs_t


def kernel(x, ffn1_w_up, ffn1_w_down, ln1_g, ln1_b, w_in, ssm_lam_re, ssm_lam_im, ssm_log_dt, ssm_b_re, ssm_b_im, ssm_c_re, ssm_c_im, ssm_d, ssm_w_glu, w_branch, w_out, ln2_g, ln2_b, ffn2_w_up, ffn2_w_down, ln3_g, ln3_b, rel_bias):
    bsz, seq, d_model = x.shape
    d_ssm = ssm_w_glu.shape[1]
    d_attn = w_branch.shape[2]
    x2 = x.reshape(bsz * seq, d_model)
    near = _near_bias_tables(rel_bias, DSA_TQ, DSA_TK)
    for l in range(ffn1_w_up.shape[0]):
        x2 = _ffn_ln(x2, ffn1_w_up[l].astype(BF16), ffn1_w_down[l].astype(BF16), ln1_g[l], ln1_b[l])
        u, q, k, kidx, gs, ga, vt, qit, kwt = _in_proj(x2, *_split_w_in(w_in[l], d_model, d_ssm, d_attn))
        tables = _ssm_tables(ssm_lam_re[l], ssm_lam_im[l], ssm_log_dt[l], ssm_b_re[l], ssm_b_im[l],
                             ssm_c_re[l], ssm_c_im[l], ssm_d[l], SSM_T)
        y_ssm = _ssm(u, tables, bsz, seq)
        y_attn = _dsa(q, k, kidx, vt, qit, kwt, near, bsz, seq)
        x2 = _merge_ln(x2, y_ssm, y_attn, gs, ga, ssm_w_glu[l].astype(BF16), w_branch[l, 0].astype(BF16),
                       w_branch[l, 1].astype(BF16), w_out[l].astype(BF16), ln2_g[l], ln2_b[l])
        x2 = _ffn_ln(x2, ffn2_w_up[l].astype(BF16), ffn2_w_down[l].astype(BF16), ln3_g[l], ln3_b[l])
    return x2.reshape(bsz, seq, d_model)
```

```python
import functools
import math

import numpy as np
import jax
import jax.numpy as jnp
from jax import lax
from jax.experimental import pallas as pl
from jax.experimental.pallas import tpu as pltpu

F32 = jnp.float32
BF16 = jnp.bfloat16
I32 = jnp.int32
I16 = jnp.int16

DEPTH = 2
CHUNK = 64
SSM_GROUP = 16
STATE = 64
N_HEADS = 8
HEAD_DIM = 64
IDX_HEADS = 8
IDX_DIM = 32
TOPK_MAX = 256
ATTN_SCALE = HEAD_DIM ** -0.5
LOG2E = math.log2(math.e)
N_BUCKETS = 32
MAX_DIST = 128
ALPHA = (2 * DEPTH) ** 0.25
LN_EPS = 1e-5

V7X_VMEM_LIMIT_BYTES = 58 * 1024 * 1024
LANES = 128
FFN_TM = 512
FFN_FC = 256
PROJ_TM = 512
MERGE_TM = 512
SSM_T = 64
DSA_TQ = 256
DSA_TK = 256
COUNT_TILES = 2
MASK_NEG = -1e30
V_ROWS = HEAD_DIM + 16
KEY_NEG_INF = int(np.int32(np.uint32(0x807FFFFF)))
HALF16 = 1 << 15
MIN_SAFE_DEN = 2.0 ** -80
BOUND_SLACK = 1.02


def _resident(shape):
    nd = len(shape)
    return pl.BlockSpec(shape, lambda *_: (0,) * nd, pipeline_mode=pl.Buffered(1))


def _layer_norm(y, g, b):
    mu = jnp.mean(y, axis=-1, keepdims=True)
    yc = y - mu
    var = jnp.mean(yc * yc, axis=-1, keepdims=True)
    return yc * lax.rsqrt(var + LN_EPS) * g + b


NT_DIMS = (((1,), (1,)), ((), ()))


def _ffn_ln_kernel(x_ref, wup_ref, wdn_ref, g_ref, b_ref, o_ref, *, d_ff, fc):
    x = x_ref[...]
    xb = x.astype(BF16)
    acc = jnp.zeros(x.shape, F32)
    for j in range(d_ff // fc):
        gate = jnp.dot(xb, wup_ref[:, j * fc:(j + 1) * fc], preferred_element_type=F32)
        up = jnp.dot(xb, wup_ref[:, d_ff + j * fc:d_ff + (j + 1) * fc], preferred_element_type=F32)
        a = (gate * jax.nn.sigmoid(gate)) * up
        acc = acc + jnp.dot(a.astype(BF16), wdn_ref[j * fc:(j + 1) * fc, :], preferred_element_type=F32)
    o_ref[...] = _layer_norm(ALPHA * x + 0.5 * acc, g_ref[...], b_ref[...])


def _ffn_ln(x2, w_up, w_down, g, b):
    n, d = x2.shape
    d_ff = w_down.shape[0]
    tm = min(FFN_TM, n)
    assert n % tm == 0 and d_ff % FFN_FC == 0
    return pl.pallas_call(
        functools.partial(_ffn_ln_kernel, d_ff=d_ff, fc=FFN_FC),
        out_shape=jax.ShapeDtypeStruct((n, d), F32),
        grid=(n // tm,),
        in_specs=[pl.BlockSpec((tm, d), lambda i: (i, 0)),
                  _resident(w_up.shape), _resident(w_down.shape),
                  _resident((1, d)), _resident((1, d))],
        out_specs=pl.BlockSpec((tm, d), lambda i: (i, 0)),
        compiler_params=pltpu.CompilerParams(dimension_semantics=("parallel",),
                                             vmem_limit_bytes=V7X_VMEM_LIMIT_BYTES),
        name="ffn_ln",
    )(x2, w_up, w_down, g.reshape(1, d), b.reshape(1, d))


def _in_proj_kernel(x_ref, w_ref, wt_ref, q_ref, k_ref, kidx_ref, gs_ref, ga_ref,
                    ut_ref, vt_ref, qit_ref, kwt_ref, *, offs, offs_t):
    xb = x_ref[...].astype(BF16)

    def proj(name):
        lo, hi = offs[name]
        return jnp.dot(xb, w_ref[:, lo:hi], preferred_element_type=F32)

    def proj_t(name):
        lo, hi = offs_t[name]
        return lax.dot_general(wt_ref[lo:hi, :], xb, NT_DIMS, preferred_element_type=F32)

    ut_ref[...] = proj_t("u")
    q_ref[...] = (proj("q") * (ATTN_SCALE * LOG2E)).astype(BF16)
    k_ref[...] = proj("k").astype(BF16)
    kidx_ref[...] = proj("kidx")[:, :IDX_DIM].astype(BF16)
    gs_ref[...] = jax.nn.sigmoid(proj("gs")).astype(BF16)
    ga_ref[...] = jax.nn.sigmoid(proj("ga")).astype(BF16)
    qit_ref[...] = proj_t("qi").astype(BF16)
    kwt_ref[...] = proj_t("kw")
    v_t = proj_t("v").astype(BF16)
    tm = v_t.shape[1]
    pad_rows = lax.broadcasted_iota(I32, (V_ROWS - HEAD_DIM, tm), 0)
    pad = jnp.where(pad_rows == 0, 1.0, 0.0).astype(BF16)
    for h in range(N_HEADS):
        vt_ref[h * V_ROWS:h * V_ROWS + HEAD_DIM, :] = v_t[h * HEAD_DIM:(h + 1) * HEAD_DIM, :]
        vt_ref[h * V_ROWS + HEAD_DIM:(h + 1) * V_ROWS, :] = pad


def _in_proj(x2, w_nat, offs, w_t, offs_t):
    n, d = x2.shape
    tm = min(PROJ_TM, n)
    assert n % tm == 0
    width = lambda name: offs[name][1] - offs[name][0]
    rows_t = lambda name: offs_t[name][1] - offs_t[name][0]
    tok = lambda w, dt: (jax.ShapeDtypeStruct((n, w), dt), pl.BlockSpec((tm, w), lambda i: (i, 0)))
    chan = lambda r, dt: (jax.ShapeDtypeStruct((r, n), dt), pl.BlockSpec((r, tm), lambda i: (0, i)))
    outs = [tok(width("q"), BF16), tok(width("k"), BF16), tok(IDX_DIM, BF16),
            tok(width("gs"), BF16), tok(width("ga"), BF16),
            chan(rows_t("u"), F32), chan(N_HEADS * V_ROWS, BF16), chan(rows_t("qi"), BF16), chan(rows_t("kw"), F32)]
    return pl.pallas_call(
        functools.partial(_in_proj_kernel, offs=offs, offs_t=offs_t),
        out_shape=[o[0] for o in outs],
        grid=(n // tm,),
        in_specs=[pl.BlockSpec((tm, d), lambda i: (i, 0)), _resident(w_nat.shape), _resident(w_t.shape)],
        out_specs=[o[1] for o in outs],
        compiler_params=pltpu.CompilerParams(dimension_semantics=("parallel",),
                                             vmem_limit_bytes=V7X_VMEM_LIMIT_BYTES),
        name="in_proj",
    )(x2, w_nat, w_t)


def _ssm_kernel(u_ref, m_ref, w_ref, v_ref, apow_ref, d_ref, y_ref, u_scr, y_scr, *, chunks_per_seq):
    t_len = u_ref.shape[2]
    for h in range(SSM_GROUP):
        u_scr[:, h * t_len:(h + 1) * t_len] = u_ref[h]
    u = u_scr[...]
    ub = u.astype(BF16)
    x = jnp.dot(ub, w_ref[0], preferred_element_type=F32)
    rows = x.shape[0]
    chunk = lax.broadcasted_iota(I32, (rows, 2 * STATE), 0) % chunks_per_seq

    def shifted(val, by):
        return jnp.where(chunk >= by, pltpu.roll(val, by, 0), 0.0)

    for k in range(chunks_per_seq.bit_length() - 1):
        prev = shifted(x, 1 << k)
        a_same = apow_ref[0, 2 * k:2 * k + 1, :]
        a_cross = apow_ref[0, 2 * k + 1:2 * k + 2, :]
        x = x + prev * a_same + pltpu.roll(prev, STATE, 1) * a_cross
    h_in = shifted(x, 1)
    y = jnp.dot(ub, m_ref[0], preferred_element_type=F32)
    y = y + jnp.dot(h_in.astype(BF16), v_ref[0], preferred_element_type=F32)
    y_scr[...] = y + d_ref[0] * u
    for h in range(SSM_GROUP):
        y_ref[h] = y_scr[:, h * t_len:(h + 1) * t_len]


def _ssm_tables(lam_re, lam_im, log_dt, b_re, b_im, c_re, c_im, d_skip, t_len, chunks_per_seq):
    g = lam_re.shape[0]
    lam = lax.complex(jnp.minimum(lam_re.astype(F32), -1e-4), lam_im.astype(F32))
    dt = jnp.exp(log_dt.astype(F32))[:, None]
    a_bar = jnp.exp(lam * dt)
    b_bar = ((a_bar - 1.0) / lam)[:, :, None] * lax.complex(b_re.astype(F32), b_im.astype(F32))
    c_mat = lax.complex(c_re.astype(F32), c_im.astype(F32))
    taus = jnp.arange(t_len + 1, dtype=F32)
    apow = jnp.exp((lam * dt)[:, None, :] * taus[None, :, None])
    hp = lax.Precision.HIGHEST
    kern = jnp.einsum("ghp,gtp,gpk->gkht", c_mat, apow[:, :t_len], b_bar, precision=hp).real
    padded = jnp.concatenate([jnp.zeros_like(kern), kern, jnp.zeros_like(kern[..., :1])], axis=-1).astype(BF16)
    skewed = jnp.tile(padded, (1, 1, 1, t_len))[..., :2 * t_len * t_len]
    skewed = skewed.reshape(g, SSM_GROUP, SSM_GROUP, t_len, 2 * t_len)[..., t_len:]
    m_tab = skewed.transpose(0, 1, 3, 2, 4).reshape(g, SSM_GROUP * t_len, SSM_GROUP * t_len)
    w_c = b_bar.transpose(0, 2, 1)[:, :, None, :] * apow[:, :t_len][:, None, ::-1, :]
    w_c = w_c.reshape(g, SSM_GROUP * t_len, STATE)
    w_tab = jnp.concatenate([w_c.real, w_c.imag], axis=-1)
    v_c = c_mat.transpose(0, 2, 1)[:, :, :, None] * apow[:, 1:t_len + 1].transpose(0, 2, 1)[:, :, None, :]
    v_c = v_c.reshape(g, STATE, SSM_GROUP * t_len)
    v_tab = jnp.concatenate([v_c.real, -v_c.imag], axis=1)
    n_steps = chunks_per_seq.bit_length() - 1
    steps = (t_len * (1 << jnp.arange(n_steps))).astype(F32)
    a_step = jnp.exp((lam * dt)[:, None, :] * steps[None, :, None])
    a_rows = jnp.stack([jnp.concatenate([a_step.real, a_step.real], axis=-1),
                        jnp.concatenate([-a_step.imag, a_step.imag], axis=-1)], axis=2)
    a_rows = a_rows.reshape(g, 2 * n_steps, 2 * STATE)
    d_tab = jnp.repeat(d_skip.astype(F32).reshape(g, 1, SSM_GROUP), t_len, axis=2)
    return m_tab, w_tab.astype(BF16), v_tab.astype(BF16), a_rows, d_tab


def _ssm(u_t, tables, seq):
    m_tab, w_tab, v_tab, a_rows, d_tab = tables
    g = m_tab.shape[0]
    d_ssm, n = u_t.shape
    t_len = SSM_T
    chunks_per_seq = seq // t_len
    assert seq % t_len == 0 and chunks_per_seq & (chunks_per_seq - 1) == 0 and d_ssm == g * SSM_GROUP
    rows, cols = n // t_len, t_len * SSM_GROUP
    per_group = lambda shape: pl.BlockSpec((1,) + shape, lambda i: (i, 0, 0))
    channels = pl.BlockSpec((SSM_GROUP, rows, t_len), lambda i: (i, 0, 0))
    y_t = pl.pallas_call(
        functools.partial(_ssm_kernel, chunks_per_seq=chunks_per_seq),
        out_shape=jax.ShapeDtypeStruct((d_ssm, rows, t_len), F32),
        grid=(g,),
        in_specs=[channels, per_group((cols, cols)), per_group((cols, 2 * STATE)), per_group((2 * STATE, cols)),
                  per_group(a_rows.shape[1:]), per_group((1, cols))],
        out_specs=channels,
        scratch_shapes=[pltpu.VMEM((rows, cols), F32), pltpu.VMEM((rows, cols), F32)],
        compiler_params=pltpu.CompilerParams(dimension_semantics=("parallel",),
                                             vmem_limit_bytes=V7X_VMEM_LIMIT_BYTES),
        name="ssm",
    )(u_t.reshape(d_ssm, rows, t_len), m_tab, w_tab, v_tab, a_rows, d_tab)
    return y_t.reshape(d_ssm, n)


def _t5_bucket_np(rel):
    half = N_BUCKETS // 2
    max_exact = half // 2
    ret = np.where(rel > 0, half, 0)
    n = np.abs(rel)
    n_f = np.maximum(n, 1).astype(np.float32)
    large = max_exact + (np.log(n_f / np.float32(max_exact)) / np.float32(math.log(MAX_DIST / max_exact))
                         * (half - max_exact)).astype(np.int32)
    large = np.minimum(large, half - 1)
    return ret + np.where(n < max_exact, n, large)


def _near_bias_tables(rel_bias, tq, tk):
    kk = np.arange(tk)[:, None]
    qq = np.arange(tq)[None, :]
    b_diag = _t5_bucket_np(kk - qq)
    b_prev = _t5_bucket_np(kk - qq - tk)
    far = int(_t5_bucket_np(np.array(-(tk + 1))))
    assert far == int(_t5_bucket_np(np.array(-(1 << 24)))), "far keys must share one bucket"
    tab = rel_bias.astype(F32)
    one_hot = jax.nn.one_hot(np.stack([b_diag, b_prev]), N_BUCKETS, dtype=F32)
    near = jnp.einsum("wkqb,bh->hwkq", one_hot, tab - tab[far][None, :], precision=lax.Precision.HIGHEST)
    return near * LOG2E


def _sortable(x):
    b = pltpu.bitcast(x, I32)
    return b ^ ((b >> 31) & 0x7FFFFFFF)


def _pair_lanes(h):
    return slice((h // 2) * 2 * HEAD_DIM, (h // 2 + 1) * 2 * HEAD_DIM)


def _dsa_kernel(qit_ref, wi_ref, q_ref, bound_ref, kidx_ref, k_ref, vt_ref, near_ref, o_ref,
                key_scr, k16_scr, mb_scr, qh_scr, m_scr, p_scr, acc_scr, out_scr, *, tq, tk, n_sel):
    i = pl.program_id(1)
    n_tiles = i + 1
    tile_rows = lambda t: pl.ds(pl.multiple_of(t * tk, tk), tk)

    def score_tile(t):
        kid = kidx_ref[tile_rows(t), :]
        acc = jnp.zeros((tk, tq), F32)
        for h in range(IDX_HEADS):
            s = jnp.dot(kid, qit_ref[h * IDX_DIM:(h + 1) * IDX_DIM, :], preferred_element_type=F32)
            acc = acc + wi_ref[h:h + 1, :] * jnp.maximum(s, 0.0)
        return acc

    def store_keys(t, score):
        key = _sortable(score)
        key_scr[tile_rows(t), :] = key
        k16_scr[tile_rows(t), :] = (key >> 16).astype(I16)

    def for_tiles(n, tile_fn):
        def pair_body(j, carry):
            tile_fn(2 * j)
            tile_fn(2 * j + 1)
            return carry

        lax.fori_loop(0, n // 2, pair_body, 0)

        @pl.when(n % 2 == 1)
        def _():
            tile_fn(n - 1)

    for_tiles(i, lambda t: store_keys(t, score_tile(t) + 0.0))
    kpos = lax.broadcasted_iota(I32, (tk, tq), 0)
    qpos = lax.broadcasted_iota(I32, (tk, tq), 1)
    admissible = (kpos // CHUNK) <= (qpos // CHUNK)
    store_keys(i, jnp.where(admissible, score_tile(i) + 0.0, -jnp.inf))
    for extra in range(COUNT_TILES - 1):
        k16_scr[tile_rows(n_tiles + extra), :] = jnp.full((tk, tq), -HALF16, I16)

    def count16(cand):
        c16 = cand.astype(I16)

        def group_body(j, c):
            rows = pl.ds(pl.multiple_of(j * (COUNT_TILES * tk), COUNT_TILES * tk), COUNT_TILES * tk)
            hit = jnp.where(k16_scr[rows, :] >= c16, jnp.int16(1), jnp.int16(0))
            parts = [hit[r * 16:(r + 1) * 16] for r in range(COUNT_TILES * tk // 16)]
            while len(parts) > 1:
                parts = [a + b for a, b in zip(parts[0::2], parts[1::2])]
            return c + parts[0]

        n_groups = (n_tiles + COUNT_TILES - 1) // COUNT_TILES
        part = lax.fori_loop(0, n_groups, group_body, jnp.zeros((16, tq), I16))
        return jnp.sum(part.astype(I32), axis=0, keepdims=True)

    def radix16(c_low):
        digit = jnp.zeros((1, tq), I32)
        for b in range(15, -1, -1):
            cand = digit | (1 << b)
            cnt = count16(cand - HALF16)
            accept = cnt >= n_sel
            digit = jnp.where(accept, cand, digit)
            c_low = jnp.where(accept, cnt, c_low)
        return digit, c_low

    q_in_blk = lax.broadcasted_iota(I32, (1, tq), 1)
    n_adm = (i * tq + q_in_blk) // CHUNK * CHUNK + CHUNK
    take_all = n_adm <= n_sel

    hi_digit, c_low = radix16(jnp.full((1, tq), 1, I32) * (n_tiles * tk))
    hi16 = hi_digit - HALF16

    def low_digit_tile(t):
        key = key_scr[tile_rows(t), :]
        hi = key >> 16
        low = (key & 0xFFFF) - HALF16
        k16_scr[tile_rows(t), :] = jnp.where(hi > hi16, HALF16 - 1, jnp.where(hi < hi16, -HALF16, low)).astype(I16)

    for_tiles(n_tiles, low_digit_tile)
    lo_digit, c_low = radix16(c_low)
    thr = jnp.where(take_all, KEY_NEG_INF + 1, jnp.left_shift(hi16, 16) | lo_digit)
    excess = jnp.where(take_all, 0, c_low - n_sel)
    any_excess = jnp.max(excess) > 0

    @pl.when(jnp.logical_not(any_excess))
    def _():
        def mask_tile(t):
            mb_scr[tile_rows(t), :] = jnp.where(key_scr[tile_rows(t), :] >= thr, 0.0, MASK_NEG)

        for_tiles(n_tiles, mask_tile)

    @pl.when(any_excess)
    def _():
        def gt_body(t, c):
            hit = jnp.where(key_scr[tile_rows(t), :] > thr, 1, 0).astype(I32)
            return c + hit.reshape(tk // 8, 8, tq).sum(axis=0)

        n_gt = jnp.sum(lax.fori_loop(0, n_tiles, gt_body, jnp.zeros((8, tq), I32)), axis=0, keepdims=True)
        need = jnp.where(take_all, 2 * tk * n_tiles, n_sel - n_gt).astype(F32)
        r_i = lax.broadcasted_iota(I32, (tk, tk), 0)
        c_i = lax.broadcasted_iota(I32, (tk, tk), 1)
        strict_lower = jnp.where(c_i < r_i, 1.0, 0.0).astype(BF16)

        def mask_body(t, ties_before):
            kt = key_scr[tile_rows(t), :]
            eq = kt == thr
            eq_f = jnp.where(eq, 1.0, 0.0)
            rank = jnp.dot(strict_lower, eq_f.astype(BF16), preferred_element_type=F32) + ties_before
            sel = jnp.logical_or(kt > thr, jnp.logical_and(eq, rank < need))
            mb_scr[tile_rows(t), :] = jnp.where(sel, 0.0, MASK_NEG)
            return ties_before + jnp.sum(eq_f.reshape(tk // 8, 8, tq).sum(axis=0), axis=0, keepdims=True)

        lax.fori_loop(0, n_tiles, mask_body, jnp.zeros((1, tq), F32))

    lane = lax.broadcasted_iota(I32, (tq, 2 * HEAD_DIM), 1)
    for h in range(N_HEADS):
        q_pair = q_ref[:, _pair_lanes(h)]
        own = (lane < HEAD_DIM) if h % 2 == 0 else (lane >= HEAD_DIM)
        qh_scr[h] = jnp.where(own, q_pair, jnp.zeros_like(q_pair))

    def masked_logits(t, h, near_which):
        lg = lax.dot_general(k_ref[tile_rows(t), _pair_lanes(h)], qh_scr[h], NT_DIMS,
                             preferred_element_type=F32) + mb_scr[tile_rows(t), :]
        if near_which is not None:
            lg = lg + near_ref[h, near_which]
        return lg

    def over_tiles(tile_fn):
        for_tiles(jnp.maximum(i - 1, 0), lambda t: tile_fn(t, None))

        @pl.when(i >= 1)
        def _():
            tile_fn(i - 1, 1)

        tile_fn(i, 0)

    def pv_tile_against(shift_row):
        def tile_fn(t, near_which):
            for h in range(N_HEADS):
                p_scr[h] = jnp.exp2(masked_logits(t, h, near_which) - shift_row(h)).astype(BF16)
            for h in range(N_HEADS):
                vrows = slice(h * V_ROWS, (h + 1) * V_ROWS)
                acc_scr[vrows, :] += jnp.dot(vt_ref[vrows, tile_rows(t)], p_scr[h], preferred_element_type=F32)
        return tile_fn

    den_row = lambda h: acc_scr[h * V_ROWS + HEAD_DIM:h * V_ROWS + HEAD_DIM + 1, :]

    acc_scr[...] = jnp.zeros(acc_scr.shape, F32)
    over_tiles(pv_tile_against(lambda h: bound_ref[h:h + 1, :]))
    den_min = den_row(0)
    for h in range(1, N_HEADS):
        den_min = jnp.minimum(den_min, den_row(h))
    bound_too_loose = jnp.logical_not(jnp.min(den_min) >= MIN_SAFE_DEN)

    @pl.when(bound_too_loose)
    def _():
        m_scr[...] = jnp.full(m_scr.shape, MASK_NEG, F32)

        def max_tile(t, near_which):
            for h in range(N_HEADS):
                part = masked_logits(t, h, near_which).reshape(tk // 8, 8, tq).max(axis=0)
                m_scr[h * 8:(h + 1) * 8, :] = jnp.maximum(m_scr[h * 8:(h + 1) * 8, :], part)

        over_tiles(max_tile)
        for h in range(N_HEADS):
            m_scr[h * 8:(h + 1) * 8, :] = jnp.broadcast_to(
                jnp.max(m_scr[h * 8:(h + 1) * 8, :], axis=0, keepdims=True), (8, tq))
        acc_scr[...] = jnp.zeros(acc_scr.shape, F32)
        over_tiles(pv_tile_against(lambda h: m_scr[h * 8:h * 8 + 1, :]))

    for h in range(N_HEADS):
        out_scr[h * HEAD_DIM:(h + 1) * HEAD_DIM, :] = acc_scr[h * V_ROWS:h * V_ROWS + HEAD_DIM, :] / den_row(h)
    o_ref[...] = out_scr[...].T


def _logit_bound(q, k, near, bsz, seq):
    n = q.shape[0]
    qn = jnp.sqrt(jnp.sum(jnp.square(q.astype(F32)).reshape(n, N_HEADS, HEAD_DIM), axis=-1))
    kn = jnp.sqrt(jnp.sum(jnp.square(k.astype(F32)).reshape(bsz, seq, N_HEADS, HEAD_DIM), axis=-1))
    k_max = jnp.max(kn, axis=1)
    near_max = jnp.maximum(jnp.max(near, axis=(1, 2, 3)), 0.0)
    bound = qn.reshape(bsz, seq, N_HEADS) * k_max[:, None, :] * BOUND_SLACK + near_max + 1.0
    return bound.reshape(n, N_HEADS).T


def _dsa(q, k, kidx, vt, qit, kwt, near, bsz, seq):
    d_attn = q.shape[1]
    tq, tk = DSA_TQ, DSA_TK
    assert tq == tk and seq % tq == 0 and tq % CHUNK == 0 and IDX_DIM % IDX_HEADS == 0
    n_sel = min(TOPK_MAX, seq // 4)
    nq = seq // tq
    bound = _logit_bound(q, k, near, bsz, seq)
    once = dict(pipeline_mode=pl.Buffered(1))
    return pl.pallas_call(
        functools.partial(_dsa_kernel, tq=tq, tk=tk, n_sel=n_sel),
        out_shape=jax.ShapeDtypeStruct((bsz * seq, d_attn), F32),
        grid=(bsz, nq),
        in_specs=[pl.BlockSpec((IDX_HEADS * IDX_DIM, tq), lambda b, i: (0, b * nq + i)),
                  pl.BlockSpec((IDX_HEADS, tq), lambda b, i: (IDX_DIM // IDX_HEADS, b * nq + i)),
                  pl.BlockSpec((tq, d_attn), lambda b, i: (b * nq + i, 0)),
                  pl.BlockSpec((N_HEADS, tq), lambda b, i: (0, b * nq + i)),
                  pl.BlockSpec((seq, IDX_DIM), lambda b, i: (b, 0), **once),
                  pl.BlockSpec((seq, d_attn), lambda b, i: (b, 0), **once),
                  pl.BlockSpec((N_HEADS * V_ROWS, seq), lambda b, i: (0, b), **once),
                  _resident(near.shape)],
        out_specs=pl.BlockSpec((tq, d_attn), lambda b, i: (b * nq + i, 0)),
        scratch_shapes=[pltpu.VMEM((seq, tq), I32), pltpu.VMEM((seq + (COUNT_TILES - 1) * tk, tq), I16), pltpu.VMEM((seq, tq), F32),
                        pltpu.VMEM((N_HEADS, tq, 2 * HEAD_DIM), BF16), pltpu.VMEM((N_HEADS * 8, tq), F32),
                        pltpu.VMEM((N_HEADS, tk, tq), BF16),
                        pltpu.VMEM((N_HEADS * V_ROWS, tq), F32), pltpu.VMEM((d_attn, tq), F32)],
        compiler_params=pltpu.CompilerParams(dimension_semantics=("parallel", "arbitrary"),
                                             vmem_limit_bytes=V7X_VMEM_LIMIT_BYTES),
        name="dsa",
    )(qit, kwt, q, bound, kidx, k, vt, near)


def _merge_ln_kernel(x_ref, ys_ref, ya_ref, gs_ref, ga_ref, wglu_ref, wb0_ref, wb1_ref, wout_ref,
                     g_ref, b_ref, o_ref):
    y = jax.nn.gelu(ys_ref[...].T)
    glu = y * jax.nn.sigmoid(jnp.dot(y.astype(BF16), wglu_ref[...], preferred_element_type=F32))
    p_ssm = jnp.dot(glu.astype(BF16), wb0_ref[...], preferred_element_type=F32)
    p_att = jnp.dot(ya_ref[...].astype(BF16), wb1_ref[...], preferred_element_type=F32)
    merged = gs_ref[...].astype(F32) * p_ssm + ga_ref[...].astype(F32) * p_att
    mix = jnp.dot(merged.astype(BF16), wout_ref[...], preferred_element_type=F32)
    o_ref[...] = _layer_norm(ALPHA * x_ref[...] + mix, g_ref[...], b_ref[...])


def _merge_ln(x2, y_ssm_t, y_attn, gs, ga, w_glu, wb0, wb1, w_out, g, b):
    n, d = x2.shape
    tm = min(MERGE_TM, n)
    assert n % tm == 0
    tile = lambda a: pl.BlockSpec((tm, a.shape[1]), lambda i: (i, 0))
    return pl.pallas_call(
        _merge_ln_kernel,
        out_shape=jax.ShapeDtypeStruct((n, d), F32),
        grid=(n // tm,),
        in_specs=[tile(x2), pl.BlockSpec((y_ssm_t.shape[0], tm), lambda i: (0, i)), tile(y_attn), tile(gs), tile(ga),
                  _resident(w_glu.shape), _resident(wb0.shape), _resident(wb1.shape), _resident(w_out.shape),
                  _resident((1, d)), _resident((1, d))],
        out_specs=pl.BlockSpec((tm, d), lambda i: (i, 0)),
        compiler_params=pltpu.CompilerParams(dimension_semantics=("parallel",),
                                             vmem_limit_bytes=V7X_VMEM_LIMIT_BYTES),
        name="merge_ln",
    )(x2, y_ssm_t, y_attn, gs, ga, w_glu, wb0, wb1, w_out, g.reshape(1, d), b.reshape(1, d))


def _pack_cols(parts):
    offs, pos = {}, 0
    for name, w in parts.items():
        offs[name] = (pos, pos + w.shape[1])
        pos += w.shape[1]
    return jnp.concatenate(list(parts.values()), axis=1), offs


def _split_w_in(w_in, d_model, d_ssm, d_attn):
    sizes = (d_ssm, d_attn, d_attn, d_attn, IDX_HEADS * IDX_DIM, IDX_DIM, IDX_HEADS, d_model, d_model)
    assert w_in.shape[1] == sum(sizes)
    edges = np.concatenate([[0], np.cumsum(sizes)])
    u, q, k, v, qi, ki, wi, gs, ga = [w_in[:, int(edges[j]):int(edges[j + 1])] for j in range(len(sizes))]
    zeros = lambda c: jnp.zeros((w_in.shape[0], c), w_in.dtype)
    w_nat, offs = _pack_cols(dict(q=q, k=k, kidx=jnp.concatenate([ki, zeros(LANES - IDX_DIM)], axis=1),
                                  gs=gs, ga=ga))
    kw = jnp.concatenate([ki, wi, zeros(LANES - IDX_DIM - IDX_HEADS)], axis=1)
    w_t, offs_t = _pack_cols(dict(u=u, v=v, qi=qi, kw=kw))
    return w_nat.astype(BF16), offs, w_t.T.astype(BF16), offs_t


def kernel(x, ffn1_w_up, ffn1_w_down, ln1_g, ln1_b, w_in, ssm_lam_re, ssm_lam_im, ssm_log_dt, ssm_b_re, ssm_b_im, ssm_c_re, ssm_c_im, ssm_d, ssm_w_glu, w_branch, w_out, ln2_g, ln2_b, ffn2_w_up, ffn2_w_down, ln3_g, ln3_b, rel_bias):
    bsz, seq, d_model = x.shape
    d_ssm = ssm_w_glu.shape[1]
    d_attn = w_branch.shape[2]
    x2 = x.reshape(bsz * seq, d_model)
    near = _near_bias_tables(rel_bias, DSA_TQ, DSA_TK)
    for l in range(ffn1_w_up.shape[0]):
        x2 = _ffn_ln(x2, ffn1_w_up[l].astype(BF16), ffn1_w_down[l].astype(BF16), ln1_g[l], ln1_b[l])
        q, k, kidx, gs, ga, ut, vt, qit, kwt = _in_proj(x2, *_split_w_in(w_in[l], d_model, d_ssm, d_attn))
        tables = _ssm_tables(ssm_lam_re[l], ssm_lam_im[l], ssm_log_dt[l], ssm_b_re[l], ssm_b_im[l],
                             ssm_c_re[l], ssm_c_im[l], ssm_d[l], SSM_T, seq // SSM_T)
        y_ssm = _ssm(ut, tables, seq)
        y_attn = _dsa(q, k, kidx, vt, qit, kwt, near, bsz, seq)
        x2 = _merge_ln(x2, y_ssm, y_attn, gs, ga, ssm_w_glu[l].astype(BF16), w_branch[l, 0].astype(BF16),
                       w_branch[l, 1].astype(BF16), w_out[l].astype(BF16), ln2_g[l], ln2_b[l])
        x2 = _ffn_ln(x2, ffn2_w_up[l].astype(BF16), ffn2_w_down[l].astype(BF16), ln3_g[l], ln3_b[l])
    return x2.reshape(bsz, seq, d_model)
```

```python
import functools
import math

import numpy as np
import jax
import jax.numpy as jnp
from jax import lax
from jax.experimental import pallas as pl
from jax.experimental.pallas import tpu as pltpu

F32 = jnp.float32
BF16 = jnp.bfloat16
I32 = jnp.int32
I16 = jnp.int16

DEPTH = 2
CHUNK = 64
SSM_GROUP = 16
STATE = 64
N_HEADS = 8
HEAD_DIM = 64
IDX_HEADS = 8
IDX_DIM = 32
TOPK_MAX = 256
ATTN_SCALE = HEAD_DIM ** -0.5
LOG2E = math.log2(math.e)
N_BUCKETS = 32
MAX_DIST = 128
ALPHA = (2 * DEPTH) ** 0.25
LN_EPS = 1e-5

V7X_VMEM_LIMIT_BYTES = 58 * 1024 * 1024
LANES = 128
FFN_TM = 512
FFN_FC = 256
PROJ_TM = 512
MERGE_TM = 512
SSM_T = 64
DSA_TQ = 256
DSA_TK = 256
COUNT_TILES = 2
MASK_NEG = -1e30
V_ROWS = HEAD_DIM + 16
KEY_NEG_INF = int(np.int32(np.uint32(0x807FFFFF)))
HALF16 = 1 << 15
MIN_SAFE_DEN = 2.0 ** -80
BOUND_SLACK = 1.02


def _resident(shape):
    nd = len(shape)
    return pl.BlockSpec(shape, lambda *_: (0,) * nd, pipeline_mode=pl.Buffered(1))


def _layer_norm(y, g, b):
    mu = jnp.mean(y, axis=-1, keepdims=True)
    yc = y - mu
    var = jnp.mean(yc * yc, axis=-1, keepdims=True)
    return yc * lax.rsqrt(var + LN_EPS) * g + b


NT_DIMS = (((1,), (1,)), ((), ()))


def _ffn_ln_kernel(x_ref, wup_ref, wdn_ref, g_ref, b_ref, o_ref, *, d_ff, fc):
    x = x_ref[...]
    xb = x.astype(BF16)
    acc = jnp.zeros(x.shape, F32)
    for j in range(d_ff // fc):
        gate = jnp.dot(xb, wup_ref[:, j * fc:(j + 1) * fc], preferred_element_type=F32)
        up = jnp.dot(xb, wup_ref[:, d_ff + j * fc:d_ff + (j + 1) * fc], preferred_element_type=F32)
        a = (gate * jax.nn.sigmoid(gate)) * up
        acc = acc + jnp.dot(a.astype(BF16), wdn_ref[j * fc:(j + 1) * fc, :], preferred_element_type=F32)
    o_ref[...] = _layer_norm(ALPHA * x + 0.5 * acc, g_ref[...], b_ref[...])


def _ffn_ln(x2, w_up, w_down, g, b):
    n, d = x2.shape
    d_ff = w_down.shape[0]
    tm = min(FFN_TM, n)
    assert n % tm == 0 and d_ff % FFN_FC == 0
    return pl.pallas_call(
        functools.partial(_ffn_ln_kernel, d_ff=d_ff, fc=FFN_FC),
        out_shape=jax.ShapeDtypeStruct((n, d), F32),
        grid=(n // tm,),
        in_specs=[pl.BlockSpec((tm, d), lambda i: (i, 0)),
                  _resident(w_up.shape), _resident(w_down.shape),
                  _resident((1, d)), _resident((1, d))],
        out_specs=pl.BlockSpec((tm, d), lambda i: (i, 0)),
        compiler_params=pltpu.CompilerParams(dimension_semantics=("parallel",),
                                             vmem_limit_bytes=V7X_VMEM_LIMIT_BYTES),
        name="ffn_ln",
    )(x2, w_up, w_down, g.reshape(1, d), b.reshape(1, d))


def _in_proj_kernel(x_ref, w_ref, wt_ref, head_ind_ref, q_ref, k_ref, kidx_ref, gs_ref, ga_ref, stats_ref,
                    ut_ref, vt_ref, qit_ref, kwt_ref, *, offs, offs_t):
    xb = x_ref[...].astype(BF16)

    def proj(name):
        lo, hi = offs[name]
        return jnp.dot(xb, w_ref[:, lo:hi], preferred_element_type=F32)

    def proj_t(name):
        lo, hi = offs_t[name]
        return lax.dot_general(wt_ref[lo:hi, :], xb, NT_DIMS, preferred_element_type=F32)

    ut_ref[...] = proj_t("u")
    q_b = (proj("q") * (ATTN_SCALE * LOG2E)).astype(BF16)
    k_b = proj("k").astype(BF16)
    q_ref[...] = q_b
    k_ref[...] = k_b

    def head_sq_max(v_b):
        per_head = jnp.dot(jnp.square(v_b.astype(F32)).astype(BF16), head_ind_ref[...], preferred_element_type=F32)
        return jnp.max(per_head, axis=0, keepdims=True)

    stats_ref[0] = jnp.concatenate([head_sq_max(q_b), head_sq_max(k_b), jnp.zeros((6, LANES), F32)], axis=0)
    kidx_ref[...] = proj("kidx")[:, :IDX_DIM].astype(BF16)
    gs_ref[...] = jax.nn.sigmoid(proj("gs")).astype(BF16)
    ga_ref[...] = jax.nn.sigmoid(proj("ga")).astype(BF16)
    qit_ref[...] = proj_t("qi").astype(BF16)
    kwt_ref[...] = proj_t("kw")
    v_t = proj_t("v").astype(BF16)
    tm = v_t.shape[1]
    pad_rows = lax.broadcasted_iota(I32, (V_ROWS - HEAD_DIM, tm), 0)
    pad = jnp.where(pad_rows == 0, 1.0, 0.0).astype(BF16)
    for h in range(N_HEADS):
        vt_ref[h * V_ROWS:h * V_ROWS + HEAD_DIM, :] = v_t[h * HEAD_DIM:(h + 1) * HEAD_DIM, :]
        vt_ref[h * V_ROWS + HEAD_DIM:(h + 1) * V_ROWS, :] = pad


def _in_proj(x2, w_nat, offs, w_t, offs_t, seq):
    n, d = x2.shape
    tm = min(PROJ_TM, seq)
    assert seq % tm == 0 and n % seq == 0
    d_attn = offs["q"][1] - offs["q"][0]
    head_ind = (np.arange(d_attn)[:, None] // HEAD_DIM == np.arange(LANES)[None, :]).astype(np.float32)
    width = lambda name: offs[name][1] - offs[name][0]
    rows_t = lambda name: offs_t[name][1] - offs_t[name][0]
    tok = lambda w, dt: (jax.ShapeDtypeStruct((n, w), dt), pl.BlockSpec((tm, w), lambda i: (i, 0)))
    chan = lambda r, dt: (jax.ShapeDtypeStruct((r, n), dt), pl.BlockSpec((r, tm), lambda i: (0, i)))
    outs = [tok(width("q"), BF16), tok(width("k"), BF16), tok(IDX_DIM, BF16),
            tok(width("gs"), BF16), tok(width("ga"), BF16),
            (jax.ShapeDtypeStruct((n // tm, 8, LANES), F32), pl.BlockSpec((1, 8, LANES), lambda i: (i, 0, 0))),
            chan(rows_t("u"), F32), chan(N_HEADS * V_ROWS, BF16), chan(rows_t("qi"), BF16), chan(rows_t("kw"), F32)]
    return pl.pallas_call(
        functools.partial(_in_proj_kernel, offs=offs, offs_t=offs_t),
        out_shape=[o[0] for o in outs],
        grid=(n // tm,),
        in_specs=[pl.BlockSpec((tm, d), lambda i: (i, 0)), _resident(w_nat.shape), _resident(w_t.shape),
                  _resident(head_ind.shape)],
        out_specs=[o[1] for o in outs],
        compiler_params=pltpu.CompilerParams(dimension_semantics=("parallel",),
                                             vmem_limit_bytes=V7X_VMEM_LIMIT_BYTES),
        name="in_proj",
    )(x2, w_nat, w_t, jnp.asarray(head_ind, BF16))


def _ssm_kernel(u_ref, m_ref, w_ref, v_ref, apow_ref, d_ref, y_ref, u_scr, y_scr, *, chunks_per_seq):
    t_len = u_ref.shape[2]
    for h in range(SSM_GROUP):
        u_scr[:, h * t_len:(h + 1) * t_len] = u_ref[h]
    u = u_scr[...]
    ub = u.astype(BF16)
    x = jnp.dot(ub, w_ref[0], preferred_element_type=F32)
    rows = x.shape[0]
    chunk = lax.broadcasted_iota(I32, (rows, 2 * STATE), 0) % chunks_per_seq

    def shifted(val, by):
        return jnp.where(chunk >= by, pltpu.roll(val, by, 0), 0.0)

    for k in range(chunks_per_seq.bit_length() - 1):
        prev = shifted(x, 1 << k)
        a_same = apow_ref[0, 2 * k:2 * k + 1, :]
        a_cross = apow_ref[0, 2 * k + 1:2 * k + 2, :]
        x = x + prev * a_same + pltpu.roll(prev, STATE, 1) * a_cross
    h_in = shifted(x, 1)
    y = jnp.dot(ub, m_ref[0], preferred_element_type=F32)
    y = y + jnp.dot(h_in.astype(BF16), v_ref[0], preferred_element_type=F32)
    y_scr[...] = y + d_ref[0] * u
    for h in range(SSM_GROUP):
        y_ref[h] = y_scr[:, h * t_len:(h + 1) * t_len]


def _ssm_tables(lam_re, lam_im, log_dt, b_re, b_im, c_re, c_im, d_skip, t_len, chunks_per_seq):
    g = lam_re.shape[0]
    lam = lax.complex(jnp.minimum(lam_re.astype(F32), -1e-4), lam_im.astype(F32))
    dt = jnp.exp(log_dt.astype(F32))[:, None]
    a_bar = jnp.exp(lam * dt)
    b_bar = ((a_bar - 1.0) / lam)[:, :, None] * lax.complex(b_re.astype(F32), b_im.astype(F32))
    c_mat = lax.complex(c_re.astype(F32), c_im.astype(F32))
    taus = jnp.arange(t_len + 1, dtype=F32)
    apow = jnp.exp((lam * dt)[:, None, :] * taus[None, :, None])
    hp = lax.Precision.HIGHEST
    kern = jnp.einsum("ghp,gtp,gpk->gkht", c_mat, apow[:, :t_len], b_bar, precision=hp).real
    pos = np.arange(t_len)
    shift = (pos[None, None, :] - pos[None, :, None]) == pos[:, None, None]
    m_tab = jnp.einsum("gkhl,lst->gksht", kern.astype(BF16), jnp.asarray(shift, BF16), preferred_element_type=BF16)
    m_tab = m_tab.reshape(g, SSM_GROUP * t_len, SSM_GROUP * t_len)
    w_c = b_bar.transpose(0, 2, 1)[:, :, None, :] * apow[:, :t_len][:, None, ::-1, :]
    w_c = w_c.reshape(g, SSM_GROUP * t_len, STATE)
    w_tab = jnp.concatenate([w_c.real, w_c.imag], axis=-1)
    v_c = c_mat.transpose(0, 2, 1)[:, :, :, None] * apow[:, 1:t_len + 1].transpose(0, 2, 1)[:, :, None, :]
    v_c = v_c.reshape(g, STATE, SSM_GROUP * t_len)
    v_tab = jnp.concatenate([v_c.real, -v_c.imag], axis=1)
    n_steps = chunks_per_seq.bit_length() - 1
    steps = (t_len * (1 << jnp.arange(n_steps))).astype(F32)
    a_step = jnp.exp((lam * dt)[:, None, :] * steps[None, :, None])
    a_rows = jnp.stack([jnp.concatenate([a_step.real, a_step.real], axis=-1),
                        jnp.concatenate([-a_step.imag, a_step.imag], axis=-1)], axis=2)
    a_rows = a_rows.reshape(g, 2 * n_steps, 2 * STATE)
    d_tab = jnp.repeat(d_skip.astype(F32).reshape(g, 1, SSM_GROUP), t_len, axis=2)
    return m_tab, w_tab.astype(BF16), v_tab.astype(BF16), a_rows, d_tab


def _ssm(u_t, tables, seq):
    m_tab, w_tab, v_tab, a_rows, d_tab = tables
    g = m_tab.shape[0]
    d_ssm, n = u_t.shape
    t_len = SSM_T
    chunks_per_seq = seq // t_len
    assert seq % t_len == 0 and chunks_per_seq & (chunks_per_seq - 1) == 0 and d_ssm == g * SSM_GROUP
    rows, cols = n // t_len, t_len * SSM_GROUP
    per_group = lambda shape: pl.BlockSpec((1,) + shape, lambda i: (i, 0, 0))
    channels = pl.BlockSpec((SSM_GROUP, rows, t_len), lambda i: (i, 0, 0))
    y_t = pl.pallas_call(
        functools.partial(_ssm_kernel, chunks_per_seq=chunks_per_seq),
        out_shape=jax.ShapeDtypeStruct((d_ssm, rows, t_len), F32),
        grid=(g,),
        in_specs=[channels, per_group((cols, cols)), per_group((cols, 2 * STATE)), per_group((2 * STATE, cols)),
                  per_group(a_rows.shape[1:]), per_group((1, cols))],
        out_specs=channels,
        scratch_shapes=[pltpu.VMEM((rows, cols), F32), pltpu.VMEM((rows, cols), F32)],
        compiler_params=pltpu.CompilerParams(dimension_semantics=("parallel",),
                                             vmem_limit_bytes=V7X_VMEM_LIMIT_BYTES),
        name="ssm",
    )(u_t.reshape(d_ssm, rows, t_len), m_tab, w_tab, v_tab, a_rows, d_tab)
    return y_t.reshape(d_ssm, n)


def _t5_bucket_np(rel):
    half = N_BUCKETS // 2
    max_exact = half // 2
    ret = np.where(rel > 0, half, 0)
    n = np.abs(rel)
    n_f = np.maximum(n, 1).astype(np.float32)
    large = max_exact + (np.log(n_f / np.float32(max_exact)) / np.float32(math.log(MAX_DIST / max_exact))
                         * (half - max_exact)).astype(np.int32)
    large = np.minimum(large, half - 1)
    return ret + np.where(n < max_exact, n, large)


def _near_bias_tables(rel_bias, tq, tk):
    kk = np.arange(tk)[:, None]
    qq = np.arange(tq)[None, :]
    b_diag = _t5_bucket_np(kk - qq)
    b_prev = _t5_bucket_np(kk - qq - tk)
    far = int(_t5_bucket_np(np.array(-(tk + 1))))
    assert far == int(_t5_bucket_np(np.array(-(1 << 24)))), "far keys must share one bucket"
    tab = rel_bias.astype(F32)
    one_hot = jax.nn.one_hot(np.stack([b_diag, b_prev]), N_BUCKETS, dtype=F32)
    near = jnp.einsum("wkqb,bh->hwkq", one_hot, tab - tab[far][None, :], precision=lax.Precision.HIGHEST)
    return near * LOG2E


def _sortable(x):
    b = pltpu.bitcast(x, I32)
    return b ^ ((b >> 31) & 0x7FFFFFFF)


def _pair_lanes(h):
    return slice((h // 2) * 2 * HEAD_DIM, (h // 2 + 1) * 2 * HEAD_DIM)


def _dsa_kernel(qit_ref, wi_ref, q_ref, bound_ref, kidx_ref, k_ref, vt_ref, near_ref, o_ref,
                key_scr, k16_scr, mb_scr, qh_scr, m_scr, p_scr, acc_scr, out_scr, *, tq, tk, n_sel):
    i = pl.program_id(1)
    n_tiles = i + 1
    tile_rows = lambda t: pl.ds(pl.multiple_of(t * tk, tk), tk)

    def score_tile(t):
        kid = kidx_ref[tile_rows(t), :]
        acc = jnp.zeros((tk, tq), F32)
        for h in range(IDX_HEADS):
            s = jnp.dot(kid, qit_ref[h * IDX_DIM:(h + 1) * IDX_DIM, :], preferred_element_type=F32)
            acc = acc + wi_ref[h:h + 1, :] * jnp.maximum(s, 0.0)
        return acc

    def store_keys(t, score):
        key = _sortable(score)
        key_scr[tile_rows(t), :] = key
        k16_scr[tile_rows(t), :] = (key >> 16).astype(I16)

    def for_tiles(n, tile_fn):
        def pair_body(j, carry):
            tile_fn(2 * j)
            tile_fn(2 * j + 1)
            return carry

        lax.fori_loop(0, n // 2, pair_body, 0)

        @pl.when(n % 2 == 1)
        def _():
            tile_fn(n - 1)

    for_tiles(i, lambda t: store_keys(t, score_tile(t) + 0.0))
    kpos = lax.broadcasted_iota(I32, (tk, tq), 0)
    qpos = lax.broadcasted_iota(I32, (tk, tq), 1)
    admissible = (kpos // CHUNK) <= (qpos // CHUNK)
    store_keys(i, jnp.where(admissible, score_tile(i) + 0.0, -jnp.inf))
    for extra in range(COUNT_TILES - 1):
        k16_scr[tile_rows(n_tiles + extra), :] = jnp.full((tk, tq), -HALF16, I16)

    def count16(cand):
        c16 = cand.astype(I16)

        def group_body(j, c):
            rows = pl.ds(pl.multiple_of(j * (COUNT_TILES * tk), COUNT_TILES * tk), COUNT_TILES * tk)
            hit = jnp.where(k16_scr[rows, :] >= c16, jnp.int16(1), jnp.int16(0))
            parts = [hit[r * 16:(r + 1) * 16] for r in range(COUNT_TILES * tk // 16)]
            while len(parts) > 1:
                parts = [a + b for a, b in zip(parts[0::2], parts[1::2])]
            return c + parts[0]

        n_groups = (n_tiles + COUNT_TILES - 1) // COUNT_TILES
        part = lax.fori_loop(0, n_groups, group_body, jnp.zeros((16, tq), I16))
        return jnp.sum(part.astype(I32), axis=0, keepdims=True)

    def radix16(c_low):
        digit = jnp.zeros((1, tq), I32)
        for b in range(15, -1, -1):
            cand = digit | (1 << b)
            cnt = count16(cand - HALF16)
            accept = cnt >= n_sel
            digit = jnp.where(accept, cand, digit)
            c_low = jnp.where(accept, cnt, c_low)
        return digit, c_low

    q_in_blk = lax.broadcasted_iota(I32, (1, tq), 1)
    n_adm = (i * tq + q_in_blk) // CHUNK * CHUNK + CHUNK
    take_all = n_adm <= n_sel

    hi_digit, c_low = radix16(jnp.full((1, tq), 1, I32) * (n_tiles * tk))
    hi16 = hi_digit - HALF16

    def low_digit_tile(t):
        key = key_scr[tile_rows(t), :]
        hi = key >> 16
        low = (key & 0xFFFF) - HALF16
        k16_scr[tile_rows(t), :] = jnp.where(hi > hi16, HALF16 - 1, jnp.where(hi < hi16, -HALF16, low)).astype(I16)

    for_tiles(n_tiles, low_digit_tile)
    lo_digit, c_low = radix16(c_low)
    thr = jnp.where(take_all, KEY_NEG_INF + 1, jnp.left_shift(hi16, 16) | lo_digit)
    excess = jnp.where(take_all, 0, c_low - n_sel)
    any_excess = jnp.max(excess) > 0

    @pl.when(jnp.logical_not(any_excess))
    def _():
        def mask_tile(t):
            mb_scr[tile_rows(t), :] = jnp.where(key_scr[tile_rows(t), :] >= thr, 0.0, MASK_NEG)

        for_tiles(n_tiles, mask_tile)

    @pl.when(any_excess)
    def _():
        def gt_body(t, c):
            hit = jnp.where(key_scr[tile_rows(t), :] > thr, 1, 0).astype(I32)
            return c + hit.reshape(tk // 8, 8, tq).sum(axis=0)

        n_gt = jnp.sum(lax.fori_loop(0, n_tiles, gt_body, jnp.zeros((8, tq), I32)), axis=0, keepdims=True)
        need = jnp.where(take_all, 2 * tk * n_tiles, n_sel - n_gt).astype(F32)
        r_i = lax.broadcasted_iota(I32, (tk, tk), 0)
        c_i = lax.broadcasted_iota(I32, (tk, tk), 1)
        strict_lower = jnp.where(c_i < r_i, 1.0, 0.0).astype(BF16)

        def mask_body(t, ties_before):
            kt = key_scr[tile_rows(t), :]
            eq = kt == thr
            eq_f = jnp.where(eq, 1.0, 0.0)
            rank = jnp.dot(strict_lower, eq_f.astype(BF16), preferred_element_type=F32) + ties_before
            sel = jnp.logical_or(kt > thr, jnp.logical_and(eq, rank < need))
            mb_scr[tile_rows(t), :] = jnp.where(sel, 0.0, MASK_NEG)
            return ties_before + jnp.sum(eq_f.reshape(tk // 8, 8, tq).sum(axis=0), axis=0, keepdims=True)

        lax.fori_loop(0, n_tiles, mask_body, jnp.zeros((1, tq), F32))

    lane = lax.broadcasted_iota(I32, (tq, 2 * HEAD_DIM), 1)
    for h in range(N_HEADS):
        q_pair = q_ref[:, _pair_lanes(h)]
        own = (lane < HEAD_DIM) if h % 2 == 0 else (lane >= HEAD_DIM)
        qh_scr[h] = jnp.where(own, q_pair, jnp.zeros_like(q_pair))

    def masked_logits(t, h, near_which):
        lg = lax.dot_general(k_ref[tile_rows(t), _pair_lanes(h)], qh_scr[h], NT_DIMS,
                             preferred_element_type=F32) + mb_scr[tile_rows(t), :]
        if near_which is not None:
            lg = lg + near_ref[h, near_which]
        return lg

    def over_tiles(tile_fn):
        for_tiles(jnp.maximum(i - 1, 0), lambda t: tile_fn(t, None))

        @pl.when(i >= 1)
        def _():
            tile_fn(i - 1, 1)

        tile_fn(i, 0)

    def pv_tile_against(shift_row):
        def tile_fn(t, near_which):
            for h in range(N_HEADS):
                p_scr[h] = jnp.exp2(masked_logits(t, h, near_which) - shift_row(h)).astype(BF16)
            for h in range(N_HEADS):
                vrows = slice(h * V_ROWS, (h + 1) * V_ROWS)
                acc_scr[vrows, :] += jnp.dot(vt_ref[vrows, tile_rows(t)], p_scr[h], preferred_element_type=F32)
        return tile_fn

    den_row = lambda h: acc_scr[h * V_ROWS + HEAD_DIM:h * V_ROWS + HEAD_DIM + 1, :]

    acc_scr[...] = jnp.zeros(acc_scr.shape, F32)
    over_tiles(pv_tile_against(lambda h: bound_ref[h:h + 1, :]))
    den_min = den_row(0)
    for h in range(1, N_HEADS):
        den_min = jnp.minimum(den_min, den_row(h))
    bound_too_loose = jnp.logical_not(jnp.min(den_min) >= MIN_SAFE_DEN)

    @pl.when(bound_too_loose)
    def _():
        m_scr[...] = jnp.full(m_scr.shape, MASK_NEG, F32)

        def max_tile(t, near_which):
            for h in range(N_HEADS):
                part = masked_logits(t, h, near_which).reshape(tk // 8, 8, tq).max(axis=0)
                m_scr[h * 8:(h + 1) * 8, :] = jnp.maximum(m_scr[h * 8:(h + 1) * 8, :], part)

        over_tiles(max_tile)
        for h in range(N_HEADS):
            m_scr[h * 8:(h + 1) * 8, :] = jnp.broadcast_to(
                jnp.max(m_scr[h * 8:(h + 1) * 8, :], axis=0, keepdims=True), (8, tq))
        acc_scr[...] = jnp.zeros(acc_scr.shape, F32)
        over_tiles(pv_tile_against(lambda h: m_scr[h * 8:h * 8 + 1, :]))

    for h in range(N_HEADS):
        out_scr[h * HEAD_DIM:(h + 1) * HEAD_DIM, :] = acc_scr[h * V_ROWS:h * V_ROWS + HEAD_DIM, :] / den_row(h)
    o_ref[...] = out_scr[...].T


def _logit_bound(stats, near, bsz, tq):
    sq = jnp.max(stats.reshape(bsz, -1, *stats.shape[1:]), axis=1)[:, :2, :N_HEADS]
    norms = jnp.sqrt(sq)
    near_max = jnp.maximum(jnp.max(near, axis=(1, 2, 3)), 0.0)
    bound = norms[:, 0] * norms[:, 1] * BOUND_SLACK + near_max + 1.0
    return jnp.broadcast_to(bound[:, :, None], (bsz, N_HEADS, tq))


def _dsa(q, k, kidx, vt, qit, kwt, stats, near, bsz, seq):
    d_attn = q.shape[1]
    tq, tk = DSA_TQ, DSA_TK
    assert tq == tk and seq % tq == 0 and tq % CHUNK == 0 and IDX_DIM % IDX_HEADS == 0
    n_sel = min(TOPK_MAX, seq // 4)
    nq = seq // tq
    bound = _logit_bound(stats, near, bsz, tq)
    once = dict(pipeline_mode=pl.Buffered(1))
    return pl.pallas_call(
        functools.partial(_dsa_kernel, tq=tq, tk=tk, n_sel=n_sel),
        out_shape=jax.ShapeDtypeStruct((bsz * seq, d_attn), F32),
        grid=(bsz, nq),
        in_specs=[pl.BlockSpec((IDX_HEADS * IDX_DIM, tq), lambda b, i: (0, b * nq + i)),
                  pl.BlockSpec((IDX_HEADS, tq), lambda b, i: (IDX_DIM // IDX_HEADS, b * nq + i)),
                  pl.BlockSpec((tq, d_attn), lambda b, i: (b * nq + i, 0)),
                  pl.BlockSpec((None, N_HEADS, tq), lambda b, i: (b, 0, 0)),
                  pl.BlockSpec((seq, IDX_DIM), lambda b, i: (b, 0), **once),
                  pl.BlockSpec((seq, d_attn), lambda b, i: (b, 0), **once),
                  pl.BlockSpec((N_HEADS * V_ROWS, seq), lambda b, i: (0, b), **once),
                  _resident(near.shape)],
        out_specs=pl.BlockSpec((tq, d_attn), lambda b, i: (b * nq + i, 0)),
        scratch_shapes=[pltpu.VMEM((seq, tq), I32), pltpu.VMEM((seq + (COUNT_TILES - 1) * tk, tq), I16), pltpu.VMEM((seq, tq), F32),
                        pltpu.VMEM((N_HEADS, tq, 2 * HEAD_DIM), BF16), pltpu.VMEM((N_HEADS * 8, tq), F32),
                        pltpu.VMEM((N_HEADS, tk, tq), BF16),
                        pltpu.VMEM((N_HEADS * V_ROWS, tq), F32), pltpu.VMEM((d_attn, tq), F32)],
        compiler_params=pltpu.CompilerParams(dimension_semantics=("parallel", "arbitrary"),
                                             vmem_limit_bytes=V7X_VMEM_LIMIT_BYTES),
        name="dsa",
    )(qit, kwt, q, bound, kidx, k, vt, near)


def _merge_ln_kernel(x_ref, ys_ref, ya_ref, gs_ref, ga_ref, wglu_ref, wb0_ref, wb1_ref, wout_ref,
                     g_ref, b_ref, o_ref):
    y = jax.nn.gelu(ys_ref[...].T)
    glu = y * jax.nn.sigmoid(jnp.dot(y.astype(BF16), wglu_ref[...], preferred_element_type=F32))
    p_ssm = jnp.dot(glu.astype(BF16), wb0_ref[...], preferred_element_type=F32)
    p_att = jnp.dot(ya_ref[...].astype(BF16), wb1_ref[...], preferred_element_type=F32)
    merged = gs_ref[...].astype(F32) * p_ssm + ga_ref[...].astype(F32) * p_att
    mix = jnp.dot(merged.astype(BF16), wout_ref[...], preferred_element_type=F32)
    o_ref[...] = _layer_norm(ALPHA * x_ref[...] + mix, g_ref[...], b_ref[...])


def _merge_ln(x2, y_ssm_t, y_attn, gs, ga, w_glu, wb0, wb1, w_out, g, b):
    n, d = x2.shape
    tm = min(MERGE_TM, n)
    assert n % tm == 0
    tile = lambda a: pl.BlockSpec((tm, a.shape[1]), lambda i: (i, 0))
    return pl.pallas_call(
        _merge_ln_kernel,
        out_shape=jax.ShapeDtypeStruct((n, d), F32),
        grid=(n // tm,),
        in_specs=[tile(x2), pl.BlockSpec((y_ssm_t.shape[0], tm), lambda i: (0, i)), tile(y_attn), tile(gs), tile(ga),
                  _resident(w_glu.shape), _resident(wb0.shape), _resident(wb1.shape), _resident(w_out.shape),
                  _resident((1, d)), _resident((1, d))],
        out_specs=pl.BlockSpec((tm, d), lambda i: (i, 0)),
        compiler_params=pltpu.CompilerParams(dimension_semantics=("parallel",),
                                             vmem_limit_bytes=V7X_VMEM_LIMIT_BYTES),
        name="merge_ln",
    )(x2, y_ssm_t, y_attn, gs, ga, w_glu, wb0, wb1, w_out, g.reshape(1, d), b.reshape(1, d))


def _pack_cols(parts):
    offs, pos = {}, 0
    for name, w in parts.items():
        offs[name] = (pos, pos + w.shape[1])
        pos += w.shape[1]
    return jnp.concatenate(list(parts.values()), axis=1), offs


def _split_w_in(w_in, d_model, d_ssm, d_attn):
    sizes = (d_ssm, d_attn, d_attn, d_attn, IDX_HEADS * IDX_DIM, IDX_DIM, IDX_HEADS, d_model, d_model)
    assert w_in.shape[1] == sum(sizes)
    edges = np.concatenate([[0], np.cumsum(sizes)])
    u, q, k, v, qi, ki, wi, gs, ga = [w_in[:, int(edges[j]):int(edges[j + 1])] for j in range(len(sizes))]
    zeros = lambda c: jnp.zeros((w_in.shape[0], c), w_in.dtype)
    w_nat, offs = _pack_cols(dict(q=q, k=k, kidx=jnp.concatenate([ki, zeros(LANES - IDX_DIM)], axis=1),
                                  gs=gs, ga=ga))
    kw = jnp.concatenate([ki, wi, zeros(LANES - IDX_DIM - IDX_HEADS)], axis=1)
    w_t, offs_t = _pack_cols(dict(u=u, v=v, qi=qi, kw=kw))
    return w_nat.astype(BF16), offs, w_t.T.astype(BF16), offs_t


def kernel(x, ffn1_w_up, ffn1_w_down, ln1_g, ln1_b, w_in, ssm_lam_re, ssm_lam_im, ssm_log_dt, ssm_b_re, ssm_b_im, ssm_c_re, ssm_c_im, ssm_d, ssm_w_glu, w_branch, w_out, ln2_g, ln2_b, ffn2_w_up, ffn2_w_down, ln3_g, ln3_b, rel_bias):
    bsz, seq, d_model = x.shape
    d_ssm = ssm_w_glu.shape[1]
    d_attn = w_branch.shape[2]
    x2 = x.reshape(bsz * seq, d_model)
    near = _near_bias_tables(rel_bias, DSA_TQ, DSA_TK)
    for l in range(ffn1_w_up.shape[0]):
        x2 = _ffn_ln(x2, ffn1_w_up[l].astype(BF16), ffn1_w_down[l].astype(BF16), ln1_g[l], ln1_b[l])
        q, k, kidx, gs, ga, stats, ut, vt, qit, kwt = _in_proj(x2, *_split_w_in(w_in[l], d_model, d_ssm, d_attn), seq)
        tables = _ssm_tables(ssm_lam_re[l], ssm_lam_im[l], ssm_log_dt[l], ssm_b_re[l], ssm_b_im[l],
                             ssm_c_re[l], ssm_c_im[l], ssm_d[l], SSM_T, seq // SSM_T)
        y_ssm = _ssm(ut, tables, seq)
        y_attn = _dsa(q, k, kidx, vt, qit, kwt, stats, near, bsz, seq)
        x2 = _merge_ln(x2, y_ssm, y_attn, gs, ga, ssm_w_glu[l].astype(BF16), w_branch[l, 0].astype(BF16),
                       w_branch[l, 1].astype(BF16), w_out[l].astype(BF16), ln2_g[l], ln2_b[l])
        x2 = _ffn_ln(x2, ffn2_w_up[l].astype(BF16), ffn2_w_down[l].astype(BF16), ln3_g[l], ln3_b[l])
    return x2.reshape(bsz, seq, d_model)
```

```python
import functools
import math

import numpy as np
import jax
import jax.numpy as jnp
from jax import lax
from jax.experimental import pallas as pl
from jax.experimental.pallas import tpu as pltpu

F32 = jnp.float32
BF16 = jnp.bfloat16
I32 = jnp.int32
I16 = jnp.int16

DEPTH = 2
CHUNK = 64
SSM_GROUP = 16
STATE = 64
N_HEADS = 8
HEAD_DIM = 64
IDX_HEADS = 8
IDX_DIM = 32
TOPK_MAX = 256
ATTN_SCALE = HEAD_DIM ** -0.5
LOG2E = math.log2(math.e)
N_BUCKETS = 32
MAX_DIST = 128
ALPHA = (2 * DEPTH) ** 0.25
LN_EPS = 1e-5

V7X_VMEM_LIMIT_BYTES = 58 * 1024 * 1024
LANES = 128
FFN_TM = 512
FFN_FC = 256
PROJ_TM = 512
MERGE_TM = 512
SSM_T = 64
DSA_TQ = 256
DSA_TK = 256
COUNT_TILES = 2
MASK_NEG = -1e30
V_ROWS = HEAD_DIM + 16
KEY_NEG_INF = int(np.int32(np.uint32(0x807FFFFF)))
HALF16 = 1 << 15
MIN_SAFE_DEN = 2.0 ** -80
BOUND_SLACK = 1.02


def _resident(shape):
    nd = len(shape)
    return pl.BlockSpec(shape, lambda *_: (0,) * nd, pipeline_mode=pl.Buffered(1))


def _layer_norm(y, g, b):
    mu = jnp.mean(y, axis=-1, keepdims=True)
    yc = y - mu
    var = jnp.mean(yc * yc, axis=-1, keepdims=True)
    return yc * lax.rsqrt(var + LN_EPS) * g + b


NT_DIMS = (((1,), (1,)), ((), ()))


def _ffn_ln_kernel(x_ref, wup_ref, wdn_ref, g_ref, b_ref, o_ref, *, d_ff, fc):
    x = x_ref[...]
    xb = x.astype(BF16)
    acc = jnp.zeros(x.shape, F32)
    for j in range(d_ff // fc):
        gate = jnp.dot(xb, wup_ref[:, j * fc:(j + 1) * fc], preferred_element_type=F32)
        up = jnp.dot(xb, wup_ref[:, d_ff + j * fc:d_ff + (j + 1) * fc], preferred_element_type=F32)
        a = (gate * jax.nn.sigmoid(gate)) * up
        acc = acc + jnp.dot(a.astype(BF16), wdn_ref[j * fc:(j + 1) * fc, :], preferred_element_type=F32)
    o_ref[...] = _layer_norm(ALPHA * x + 0.5 * acc, g_ref[...], b_ref[...])


def _ffn_ln(x2, w_up, w_down, g, b):
    n, d = x2.shape
    d_ff = w_down.shape[0]
    tm = min(FFN_TM, n)
    assert n % tm == 0 and d_ff % FFN_FC == 0
    return pl.pallas_call(
        functools.partial(_ffn_ln_kernel, d_ff=d_ff, fc=FFN_FC),
        out_shape=jax.ShapeDtypeStruct((n, d), F32),
        grid=(n // tm,),
        in_specs=[pl.BlockSpec((tm, d), lambda i: (i, 0)),
                  _resident(w_up.shape), _resident(w_down.shape),
                  _resident((1, d)), _resident((1, d))],
        out_specs=pl.BlockSpec((tm, d), lambda i: (i, 0)),
        compiler_params=pltpu.CompilerParams(dimension_semantics=("parallel",),
                                             vmem_limit_bytes=V7X_VMEM_LIMIT_BYTES),
        name="ffn_ln",
    )(x2, w_up, w_down, g.reshape(1, d), b.reshape(1, d))


def _in_proj_kernel(x_ref, w_ref, wt_ref, head_ind_ref, q_ref, k_ref, kidx_ref, gs_ref, ga_ref, stats_ref,
                    ut_ref, vt_ref, qit_ref, kwt_ref, *, offs, offs_t):
    xb = x_ref[...].astype(BF16)

    def proj(name):
        lo, hi = offs[name]
        return jnp.dot(xb, w_ref[:, lo:hi], preferred_element_type=F32)

    def proj_t(name):
        lo, hi = offs_t[name]
        return lax.dot_general(wt_ref[lo:hi, :], xb, NT_DIMS, preferred_element_type=F32)

    ut_ref[...] = proj_t("u")
    q_b = (proj("q") * (ATTN_SCALE * LOG2E)).astype(BF16)
    k_b = proj("k").astype(BF16)
    q_ref[...] = q_b
    k_ref[...] = k_b

    def head_sq_max(v_b):
        per_head = jnp.dot(jnp.square(v_b.astype(F32)).astype(BF16), head_ind_ref[...], preferred_element_type=F32)
        return jnp.max(per_head, axis=0, keepdims=True)

    stats_ref[0] = jnp.concatenate([head_sq_max(q_b), head_sq_max(k_b), jnp.zeros((6, LANES), F32)], axis=0)
    kidx_ref[...] = proj("kidx")[:, :IDX_DIM].astype(BF16)
    gs_ref[...] = jax.nn.sigmoid(proj("gs")).astype(BF16)
    ga_ref[...] = jax.nn.sigmoid(proj("ga")).astype(BF16)
    qit_ref[...] = proj_t("qi").astype(BF16)
    kwt_ref[...] = proj_t("kw")
    v_t = proj_t("v").astype(BF16)
    tm = v_t.shape[1]
    pad_rows = lax.broadcasted_iota(I32, (V_ROWS - HEAD_DIM, tm), 0)
    pad = jnp.where(pad_rows == 0, 1.0, 0.0).astype(BF16)
    for h in range(N_HEADS):
        vt_ref[h * V_ROWS:h * V_ROWS + HEAD_DIM, :] = v_t[h * HEAD_DIM:(h + 1) * HEAD_DIM, :]
        vt_ref[h * V_ROWS + HEAD_DIM:(h + 1) * V_ROWS, :] = pad


def _in_proj(x2, w_nat, offs, w_t, offs_t, seq):
    n, d = x2.shape
    tm = min(PROJ_TM, seq)
    assert seq % tm == 0 and n % seq == 0
    d_attn = offs["q"][1] - offs["q"][0]
    head_ind = (np.arange(d_attn)[:, None] // HEAD_DIM == np.arange(LANES)[None, :]).astype(np.float32)
    width = lambda name: offs[name][1] - offs[name][0]
    rows_t = lambda name: offs_t[name][1] - offs_t[name][0]
    tok = lambda w, dt: (jax.ShapeDtypeStruct((n, w), dt), pl.BlockSpec((tm, w), lambda i: (i, 0)))
    chan = lambda r, dt: (jax.ShapeDtypeStruct((r, n), dt), pl.BlockSpec((r, tm), lambda i: (0, i)))
    outs = [tok(width("q"), BF16), tok(width("k"), BF16), tok(IDX_DIM, BF16),
            tok(width("gs"), BF16), tok(width("ga"), BF16),
            (jax.ShapeDtypeStruct((n // tm, 8, LANES), F32), pl.BlockSpec((1, 8, LANES), lambda i: (i, 0, 0))),
            chan(rows_t("u"), F32), chan(N_HEADS * V_ROWS, BF16), chan(rows_t("qi"), BF16), chan(rows_t("kw"), F32)]
    return pl.pallas_call(
        functools.partial(_in_proj_kernel, offs=offs, offs_t=offs_t),
        out_shape=[o[0] for o in outs],
        grid=(n // tm,),
        in_specs=[pl.BlockSpec((tm, d), lambda i: (i, 0)), _resident(w_nat.shape), _resident(w_t.shape),
                  _resident(head_ind.shape)],
        out_specs=[o[1] for o in outs],
        compiler_params=pltpu.CompilerParams(dimension_semantics=("parallel",),
                                             vmem_limit_bytes=V7X_VMEM_LIMIT_BYTES),
        name="in_proj",
    )(x2, w_nat, w_t, jnp.asarray(head_ind, BF16))


def _ssm_kernel(u_ref, m_ref, w_ref, v_ref, apow_ref, d_ref, y_ref, u_scr, y_scr, *, rows_per_seq):
    t_len = u_ref.shape[2] // 2
    n_rows = u_ref.shape[1]
    even, odd = slice(0, n_rows), slice(n_rows, 2 * n_rows)
    for h in range(SSM_GROUP):
        cols = slice(h * t_len, (h + 1) * t_len)
        u_scr[even, cols] = u_ref[h, :, :t_len]
        u_scr[odd, cols] = u_ref[h, :, t_len:]
    u = u_scr[...]
    ub = u.astype(BF16)
    s = jnp.dot(ub, w_ref[0], preferred_element_type=F32)
    s_even, s_odd = s[even], s[odd]
    row = lax.broadcasted_iota(I32, (n_rows, 2 * STATE), 0) % rows_per_seq

    def times_a(val, k):
        a_same = apow_ref[0, 2 * k:2 * k + 1, :]
        a_cross = apow_ref[0, 2 * k + 1:2 * k + 2, :]
        return val * a_same + pltpu.roll(val, STATE, 1) * a_cross

    def shifted(val, by):
        return jnp.where(row >= by, pltpu.roll(val, by, 0), 0.0)

    x = times_a(s_even, 0) + s_odd
    for k in range(1, rows_per_seq.bit_length()):
        x = x + times_a(shifted(x, 1 << (k - 1)), k)
    h_even = shifted(x, 1)
    h_odd = times_a(h_even, 0) + s_even
    h_in = jnp.concatenate([h_even, h_odd], axis=0)
    y = jnp.dot(ub, m_ref[0], preferred_element_type=F32)
    y = y + jnp.dot(h_in.astype(BF16), v_ref[0], preferred_element_type=F32)
    y_scr[...] = y + d_ref[0] * u
    for h in range(SSM_GROUP):
        cols = slice(h * t_len, (h + 1) * t_len)
        y_ref[h, :, :t_len] = y_scr[even, cols]
        y_ref[h, :, t_len:] = y_scr[odd, cols]


def _ssm_tables(lam_re, lam_im, log_dt, b_re, b_im, c_re, c_im, d_skip, t_len, chunks_per_seq):
    g = lam_re.shape[0]
    lam = lax.complex(jnp.minimum(lam_re.astype(F32), -1e-4), lam_im.astype(F32))
    dt = jnp.exp(log_dt.astype(F32))[:, None]
    a_bar = jnp.exp(lam * dt)
    b_bar = ((a_bar - 1.0) / lam)[:, :, None] * lax.complex(b_re.astype(F32), b_im.astype(F32))
    c_mat = lax.complex(c_re.astype(F32), c_im.astype(F32))
    taus = jnp.arange(t_len + 1, dtype=F32)
    apow = jnp.exp((lam * dt)[:, None, :] * taus[None, :, None])
    hp = lax.Precision.HIGHEST
    kern = jnp.einsum("ghp,gtp,gpk->gkht", c_mat, apow[:, :t_len], b_bar, precision=hp).real
    pos = np.arange(t_len)
    shift = (pos[None, None, :] - pos[None, :, None]) == pos[:, None, None]
    m_tab = jnp.einsum("gkhl,lst->gksht", kern.astype(BF16), jnp.asarray(shift, BF16), preferred_element_type=BF16)
    m_tab = m_tab.reshape(g, SSM_GROUP * t_len, SSM_GROUP * t_len)
    w_c = b_bar.transpose(0, 2, 1)[:, :, None, :] * apow[:, :t_len][:, None, ::-1, :]
    w_c = w_c.reshape(g, SSM_GROUP * t_len, STATE)
    w_tab = jnp.concatenate([w_c.real, w_c.imag], axis=-1)
    v_c = c_mat.transpose(0, 2, 1)[:, :, :, None] * apow[:, 1:t_len + 1].transpose(0, 2, 1)[:, :, None, :]
    v_c = v_c.reshape(g, STATE, SSM_GROUP * t_len)
    v_tab = jnp.concatenate([v_c.real, -v_c.imag], axis=1)
    n_steps = chunks_per_seq.bit_length() - 1
    steps = (t_len * (1 << jnp.arange(n_steps))).astype(F32)
    a_step = jnp.exp((lam * dt)[:, None, :] * steps[None, :, None])
    a_rows = jnp.stack([jnp.concatenate([a_step.real, a_step.real], axis=-1),
                        jnp.concatenate([-a_step.imag, a_step.imag], axis=-1)], axis=2)
    a_rows = a_rows.reshape(g, 2 * n_steps, 2 * STATE)
    d_tab = jnp.repeat(d_skip.astype(F32).reshape(g, 1, SSM_GROUP), t_len, axis=2)
    return m_tab, w_tab.astype(BF16), v_tab.astype(BF16), a_rows, d_tab


def _ssm(u_t, tables, seq):
    m_tab, w_tab, v_tab, a_rows, d_tab = tables
    g = m_tab.shape[0]
    d_ssm, n = u_t.shape
    t_len = SSM_T
    rows_per_seq = seq // (2 * t_len)
    assert 2 * t_len == LANES and seq % LANES == 0 and rows_per_seq & (rows_per_seq - 1) == 0
    assert d_ssm == g * SSM_GROUP
    rows, cols = n // LANES, t_len * SSM_GROUP
    per_group = lambda shape: pl.BlockSpec((1,) + shape, lambda i: (i, 0, 0))
    channels = pl.BlockSpec((SSM_GROUP, rows, LANES), lambda i: (i, 0, 0))
    y_t = pl.pallas_call(
        functools.partial(_ssm_kernel, rows_per_seq=rows_per_seq),
        out_shape=jax.ShapeDtypeStruct((d_ssm, rows, LANES), F32),
        grid=(g,),
        in_specs=[channels, per_group((cols, cols)), per_group((cols, 2 * STATE)), per_group((2 * STATE, cols)),
                  per_group(a_rows.shape[1:]), per_group((1, cols))],
        out_specs=channels,
        scratch_shapes=[pltpu.VMEM((2 * rows, cols), F32), pltpu.VMEM((2 * rows, cols), F32)],
        compiler_params=pltpu.CompilerParams(dimension_semantics=("parallel",),
                                             vmem_limit_bytes=V7X_VMEM_LIMIT_BYTES),
        name="ssm",
    )(u_t.reshape(d_ssm, rows, LANES), m_tab, w_tab, v_tab, a_rows, d_tab)
    return y_t.reshape(d_ssm, n)


def _t5_bucket_np(rel):
    half = N_BUCKETS // 2
    max_exact = half // 2
    ret = np.where(rel > 0, half, 0)
    n = np.abs(rel)
    n_f = np.maximum(n, 1).astype(np.float32)
    large = max_exact + (np.log(n_f / np.float32(max_exact)) / np.float32(math.log(MAX_DIST / max_exact))
                         * (half - max_exact)).astype(np.int32)
    large = np.minimum(large, half - 1)
    return ret + np.where(n < max_exact, n, large)


def _near_bias_tables(rel_bias, tq, tk):
    kk = np.arange(tk)[:, None]
    qq = np.arange(tq)[None, :]
    b_diag = _t5_bucket_np(kk - qq)
    b_prev = _t5_bucket_np(kk - qq - tk)
    far = int(_t5_bucket_np(np.array(-(tk + 1))))
    assert far == int(_t5_bucket_np(np.array(-(1 << 24)))), "far keys must share one bucket"
    tab = rel_bias.astype(F32)
    one_hot = jax.nn.one_hot(np.stack([b_diag, b_prev]), N_BUCKETS, dtype=F32)
    near = jnp.einsum("wkqb,bh->hwkq", one_hot, tab - tab[far][None, :], precision=lax.Precision.HIGHEST)
    return near * LOG2E


def _sortable(x):
    b = pltpu.bitcast(x, I32)
    return b ^ ((b >> 31) & 0x7FFFFFFF)


def _pair_lanes(h):
    return slice((h // 2) * 2 * HEAD_DIM, (h // 2 + 1) * 2 * HEAD_DIM)


def _dsa_kernel(qit_ref, wi_ref, q_ref, bound_ref, kidx_ref, k_ref, vt_ref, near_ref, o_ref,
                key_scr, k16_scr, mb_scr, qh_scr, m_scr, p_scr, acc_scr, out_scr, *, tq, tk, n_sel):
    i = pl.program_id(1)
    n_tiles = i + 1
    tile_rows = lambda t: pl.ds(pl.multiple_of(t * tk, tk), tk)

    def score_tile(t):
        kid = kidx_ref[tile_rows(t), :]
        acc = jnp.zeros((tk, tq), F32)
        for h in range(IDX_HEADS):
            s = jnp.dot(kid, qit_ref[h * IDX_DIM:(h + 1) * IDX_DIM, :], preferred_element_type=F32)
            acc = acc + wi_ref[h:h + 1, :] * jnp.maximum(s, 0.0)
        return acc

    def store_keys(t, score):
        key = _sortable(score)
        key_scr[tile_rows(t), :] = key
        k16_scr[tile_rows(t), :] = (key >> 16).astype(I16)

    def for_tiles(n, tile_fn):
        def pair_body(j, carry):
            tile_fn(2 * j)
            tile_fn(2 * j + 1)
            return carry

        lax.fori_loop(0, n // 2, pair_body, 0)

        @pl.when(n % 2 == 1)
        def _():
            tile_fn(n - 1)

    for_tiles(i, lambda t: store_keys(t, score_tile(t) + 0.0))
    kpos = lax.broadcasted_iota(I32, (tk, tq), 0)
    qpos = lax.broadcasted_iota(I32, (tk, tq), 1)
    admissible = (kpos // CHUNK) <= (qpos // CHUNK)
    store_keys(i, jnp.where(admissible, score_tile(i) + 0.0, -jnp.inf))
    for extra in range(COUNT_TILES - 1):
        k16_scr[tile_rows(n_tiles + extra), :] = jnp.full((tk, tq), -HALF16, I16)

    def count16(cand):
        c16 = cand.astype(I16)

        def group_body(j, c):
            rows = pl.ds(pl.multiple_of(j * (COUNT_TILES * tk), COUNT_TILES * tk), COUNT_TILES * tk)
            hit = jnp.where(k16_scr[rows, :] >= c16, jnp.int16(1), jnp.int16(0))
            parts = [hit[r * 16:(r + 1) * 16] for r in range(COUNT_TILES * tk // 16)]
            while len(parts) > 1:
                parts = [a + b for a, b in zip(parts[0::2], parts[1::2])]
            return c + parts[0]

        n_groups = (n_tiles + COUNT_TILES - 1) // COUNT_TILES
        part = lax.fori_loop(0, n_groups, group_body, jnp.zeros((16, tq), I16))
        return jnp.sum(part.astype(I32), axis=0, keepdims=True)

    def radix16(c_low):
        digit = jnp.zeros((1, tq), I32)
        for b in range(15, -1, -1):
            cand = digit | (1 << b)
            cnt = count16(cand - HALF16)
            accept = cnt >= n_sel
            digit = jnp.where(accept, cand, digit)
            c_low = jnp.where(accept, cnt, c_low)
        return digit, c_low

    q_in_blk = lax.broadcasted_iota(I32, (1, tq), 1)
    n_adm = (i * tq + q_in_blk) // CHUNK * CHUNK + CHUNK
    take_all = n_adm <= n_sel

    hi_digit, c_low = radix16(jnp.full((1, tq), 1, I32) * (n_tiles * tk))
    hi16 = hi_digit - HALF16

    def low_digit_tile(t):
        key = key_scr[tile_rows(t), :]
        hi = key >> 16
        low = (key & 0xFFFF) - HALF16
        k16_scr[tile_rows(t), :] = jnp.where(hi > hi16, HALF16 - 1, jnp.where(hi < hi16, -HALF16, low)).astype(I16)

    for_tiles(n_tiles, low_digit_tile)
    lo_digit, c_low = radix16(c_low)
    thr = jnp.where(take_all, KEY_NEG_INF + 1, jnp.left_shift(hi16, 16) | lo_digit)
    excess = jnp.where(take_all, 0, c_low - n_sel)
    any_excess = jnp.max(excess) > 0

    @pl.when(jnp.logical_not(any_excess))
    def _():
        def mask_tile(t):
            mb_scr[tile_rows(t), :] = jnp.where(key_scr[tile_rows(t), :] >= thr, 0.0, MASK_NEG)

        for_tiles(n_tiles, mask_tile)

    @pl.when(any_excess)
    def _():
        def gt_body(t, c):
            hit = jnp.where(key_scr[tile_rows(t), :] > thr, 1, 0).astype(I32)
            return c + hit.reshape(tk // 8, 8, tq).sum(axis=0)

        n_gt = jnp.sum(lax.fori_loop(0, n_tiles, gt_body, jnp.zeros((8, tq), I32)), axis=0, keepdims=True)
        need = jnp.where(take_all, 2 * tk * n_tiles, n_sel - n_gt).astype(F32)
        r_i = lax.broadcasted_iota(I32, (tk, tk), 0)
        c_i = lax.broadcasted_iota(I32, (tk, tk), 1)
        strict_lower = jnp.where(c_i < r_i, 1.0, 0.0).astype(BF16)

        def mask_body(t, ties_before):
            kt = key_scr[tile_rows(t), :]
            eq = kt == thr
            eq_f = jnp.where(eq, 1.0, 0.0)
            rank = jnp.dot(strict_lower, eq_f.astype(BF16), preferred_element_type=F32) + ties_before
            sel = jnp.logical_or(kt > thr, jnp.logical_and(eq, rank < need))
            mb_scr[tile_rows(t), :] = jnp.where(sel, 0.0, MASK_NEG)
            return ties_before + jnp.sum(eq_f.reshape(tk // 8, 8, tq).sum(axis=0), axis=0, keepdims=True)

        lax.fori_loop(0, n_tiles, mask_body, jnp.zeros((1, tq), F32))

    lane = lax.broadcasted_iota(I32, (tq, 2 * HEAD_DIM), 1)
    for h in range(N_HEADS):
        q_pair = q_ref[:, _pair_lanes(h)]
        own = (lane < HEAD_DIM) if h % 2 == 0 else (lane >= HEAD_DIM)
        qh_scr[h] = jnp.where(own, q_pair, jnp.zeros_like(q_pair))

    def masked_logits(t, h, near_which):
        lg = lax.dot_general(k_ref[tile_rows(t), _pair_lanes(h)], qh_scr[h], NT_DIMS,
                             preferred_element_type=F32) + mb_scr[tile_rows(t), :]
        if near_which is not None:
            lg = lg + near_ref[h, near_which]
        return lg

    def over_tiles(tile_fn):
        for_tiles(jnp.maximum(i - 1, 0), lambda t: tile_fn(t, None))

        @pl.when(i >= 1)
        def _():
            tile_fn(i - 1, 1)

        tile_fn(i, 0)

    def pv_tile_against(shift_row):
        def tile_fn(t, near_which):
            for h in range(N_HEADS):
                p_scr[h] = jnp.exp2(masked_logits(t, h, near_which) - shift_row(h)).astype(BF16)
            for h in range(N_HEADS):
                vrows = slice(h * V_ROWS, (h + 1) * V_ROWS)
                acc_scr[vrows, :] += jnp.dot(vt_ref[vrows, tile_rows(t)], p_scr[h], preferred_element_type=F32)
        return tile_fn

    den_row = lambda h: acc_scr[h * V_ROWS + HEAD_DIM:h * V_ROWS + HEAD_DIM + 1, :]

    acc_scr[...] = jnp.zeros(acc_scr.shape, F32)
    over_tiles(pv_tile_against(lambda h: bound_ref[h:h + 1, :]))
    den_min = den_row(0)
    for h in range(1, N_HEADS):
        den_min = jnp.minimum(den_min, den_row(h))
    bound_too_loose = jnp.logical_not(jnp.min(den_min) >= MIN_SAFE_DEN)

    @pl.when(bound_too_loose)
    def _():
        m_scr[...] = jnp.full(m_scr.shape, MASK_NEG, F32)

        def max_tile(t, near_which):
            for h in range(N_HEADS):
                part = masked_logits(t, h, near_which).reshape(tk // 8, 8, tq).max(axis=0)
                m_scr[h * 8:(h + 1) * 8, :] = jnp.maximum(m_scr[h * 8:(h + 1) * 8, :], part)

        over_tiles(max_tile)
        for h in range(N_HEADS):
            m_scr[h * 8:(h + 1) * 8, :] = jnp.broadcast_to(
                jnp.max(m_scr[h * 8:(h + 1) * 8, :], axis=0, keepdims=True), (8, tq))
        acc_scr[...] = jnp.zeros(acc_scr.shape, F32)
        over_tiles(pv_tile_against(lambda h: m_scr[h * 8:h * 8 + 1, :]))

    for h in range(N_HEADS):
        out_scr[h * HEAD_DIM:(h + 1) * HEAD_DIM, :] = acc_scr[h * V_ROWS:h * V_ROWS + HEAD_DIM, :] / den_row(h)
    o_ref[...] = out_scr[...].T


def _logit_bound(stats, near, bsz, tq):
    sq = jnp.max(stats.reshape(bsz, -1, *stats.shape[1:]), axis=1)[:, :2, :N_HEADS]
    norms = jnp.sqrt(sq)
    near_max = jnp.maximum(jnp.max(near, axis=(1, 2, 3)), 0.0)
    bound = norms[:, 0] * norms[:, 1] * BOUND_SLACK + near_max + 1.0
    return jnp.broadcast_to(bound[:, :, None], (bsz, N_HEADS, tq))


def _dsa(q, k, kidx, vt, qit, kwt, stats, near, bsz, seq):
    d_attn = q.shape[1]
    tq, tk = DSA_TQ, DSA_TK
    assert tq == tk and seq % tq == 0 and tq % CHUNK == 0 and IDX_DIM % IDX_HEADS == 0
    n_sel = min(TOPK_MAX, seq // 4)
    nq = seq // tq
    bound = _logit_bound(stats, near, bsz, tq)
    once = dict(pipeline_mode=pl.Buffered(1))
    return pl.pallas_call(
        functools.partial(_dsa_kernel, tq=tq, tk=tk, n_sel=n_sel),
        out_shape=jax.ShapeDtypeStruct((bsz * seq, d_attn), F32),
        grid=(bsz, nq),
        in_specs=[pl.BlockSpec((IDX_HEADS * IDX_DIM, tq), lambda b, i: (0, b * nq + i)),
                  pl.BlockSpec((IDX_HEADS, tq), lambda b, i: (IDX_DIM // IDX_HEADS, b * nq + i)),
                  pl.BlockSpec((tq, d_attn), lambda b, i: (b * nq + i, 0)),
                  pl.BlockSpec((None, N_HEADS, tq), lambda b, i: (b, 0, 0)),
                  pl.BlockSpec((seq, IDX_DIM), lambda b, i: (b, 0), **once),
                  pl.BlockSpec((seq, d_attn), lambda b, i: (b, 0), **once),
                  pl.BlockSpec((N_HEADS * V_ROWS, seq), lambda b, i: (0, b), **once),
                  _resident(near.shape)],
        out_specs=pl.BlockSpec((tq, d_attn), lambda b, i: (b * nq + i, 0)),
        scratch_shapes=[pltpu.VMEM((seq, tq), I32), pltpu.VMEM((seq + (COUNT_TILES - 1) * tk, tq), I16), pltpu.VMEM((seq, tq), F32),
                        pltpu.VMEM((N_HEADS, tq, 2 * HEAD_DIM), BF16), pltpu.VMEM((N_HEADS * 8, tq), F32),
                        pltpu.VMEM((N_HEADS, tk, tq), BF16),
                        pltpu.VMEM((N_HEADS * V_ROWS, tq), F32), pltpu.VMEM((d_attn, tq), F32)],
        compiler_params=pltpu.CompilerParams(dimension_semantics=("parallel", "arbitrary"),
                                             vmem_limit_bytes=V7X_VMEM_LIMIT_BYTES),
        name="dsa",
    )(qit, kwt, q, bound, kidx, k, vt, near)


def _merge_ln_kernel(x_ref, ys_ref, ya_ref, gs_ref, ga_ref, wglu_ref, wb0_ref, wb1_ref, wout_ref,
                     g_ref, b_ref, o_ref):
    y = jax.nn.gelu(ys_ref[...].T)
    glu = y * jax.nn.sigmoid(jnp.dot(y.astype(BF16), wglu_ref[...], preferred_element_type=F32))
    p_ssm = jnp.dot(glu.astype(BF16), wb0_ref[...], preferred_element_type=F32)
    p_att = jnp.dot(ya_ref[...].astype(BF16), wb1_ref[...], preferred_element_type=F32)
    merged = gs_ref[...].astype(F32) * p_ssm + ga_ref[...].astype(F32) * p_att
    mix = jnp.dot(merged.astype(BF16), wout_ref[...], preferred_element_type=F32)
    o_ref[...] = _layer_norm(ALPHA * x_ref[...] + mix, g_ref[...], b_ref[...])


def _merge_ln(x2, y_ssm_t, y_attn, gs, ga, w_glu, wb0, wb1, w_out, g, b):
    n, d = x2.shape
    tm = min(MERGE_TM, n)
    assert n % tm == 0
    tile = lambda a: pl.BlockSpec((tm, a.shape[1]), lambda i: (i, 0))
    return pl.pallas_call(
        _merge_ln_kernel,
        out_shape=jax.ShapeDtypeStruct((n, d), F32),
        grid=(n // tm,),
        in_specs=[tile(x2), pl.BlockSpec((y_ssm_t.shape[0], tm), lambda i: (0, i)), tile(y_attn), tile(gs), tile(ga),
                  _resident(w_glu.shape), _resident(wb0.shape), _resident(wb1.shape), _resident(w_out.shape),
                  _resident((1, d)), _resident((1, d))],
        out_specs=pl.BlockSpec((tm, d), lambda i: (i, 0)),
        compiler_params=pltpu.CompilerParams(dimension_semantics=("parallel",),
                                             vmem_limit_bytes=V7X_VMEM_LIMIT_BYTES),
        name="merge_ln",
    )(x2, y_ssm_t, y_attn, gs, ga, w_glu, wb0, wb1, w_out, g.reshape(1, d), b.reshape(1, d))


def _pack_cols(parts):
    offs, pos = {}, 0
    for name, w in parts.items():
        offs[name] = (pos, pos + w.shape[1])
        pos += w.shape[1]
    return jnp.concatenate(list(parts.values()), axis=1), offs


def _split_w_in(w_in, d_model, d_ssm, d_attn):
    sizes = (d_ssm, d_attn, d_attn, d_attn, IDX_HEADS * IDX_DIM, IDX_DIM, IDX_HEADS, d_model, d_model)
    assert w_in.shape[1] == sum(sizes)
    edges = np.concatenate([[0], np.cumsum(sizes)])
    u, q, k, v, qi, ki, wi, gs, ga = [w_in[:, int(edges[j]):int(edges[j + 1])] for j in range(len(sizes))]
    zeros = lambda c: jnp.zeros((w_in.shape[0], c), w_in.dtype)
    w_nat, offs = _pack_cols(dict(q=q, k=k, kidx=jnp.concatenate([ki, zeros(LANES - IDX_DIM)], axis=1),
                                  gs=gs, ga=ga))
    kw = jnp.concatenate([ki, wi, zeros(LANES - IDX_DIM - IDX_HEADS)], axis=1)
    w_t, offs_t = _pack_cols(dict(u=u, v=v, qi=qi, kw=kw))
    return w_nat.astype(BF16), offs, w_t.T.astype(BF16), offs_t


def kernel(x, ffn1_w_up, ffn1_w_down, ln1_g, ln1_b, w_in, ssm_lam_re, ssm_lam_im, ssm_log_dt, ssm_b_re, ssm_b_im, ssm_c_re, ssm_c_im, ssm_d, ssm_w_glu, w_branch, w_out, ln2_g, ln2_b, ffn2_w_up, ffn2_w_down, ln3_g, ln3_b, rel_bias):
    bsz, seq, d_model = x.shape
    d_ssm = ssm_w_glu.shape[1]
    d_attn = w_branch.shape[2]
    x2 = x.reshape(bsz * seq, d_model)
    near = _near_bias_tables(rel_bias, DSA_TQ, DSA_TK)
    for l in range(ffn1_w_up.shape[0]):
        x2 = _ffn_ln(x2, ffn1_w_up[l].astype(BF16), ffn1_w_down[l].astype(BF16), ln1_g[l], ln1_b[l])
        q, k, kidx, gs, ga, stats, ut, vt, qit, kwt = _in_proj(x2, *_split_w_in(w_in[l], d_model, d_ssm, d_attn), seq)
        tables = _ssm_tables(ssm_lam_re[l], ssm_lam_im[l], ssm_log_dt[l], ssm_b_re[l], ssm_b_im[l],
                             ssm_c_re[l], ssm_c_im[l], ssm_d[l], SSM_T, seq // SSM_T)
        y_ssm = _ssm(ut, tables, seq)
        y_attn = _dsa(q, k, kidx, vt, qit, kwt, stats, near, bsz, seq)
        x2 = _merge_ln(x2, y_ssm, y_attn, gs, ga, ssm_w_glu[l].astype(BF16), w_branch[l, 0].astype(BF16),
                       w_branch[l, 1].astype(BF16), w_out[l].astype(BF16), ln2_g[l], ln2_b[l])
        x2 = _ffn_ln(x2, ffn2_w_up[l].astype(BF16), ffn2_w_down[l].astype(BF16), ln3_g[l], ln3_b[l])
    return x2.reshape(bsz, seq, d_model)
```

```python
import functools
import math

import numpy as np
import jax
import jax.numpy as jnp
from jax import lax
from jax.experimental import pallas as pl
from jax.experimental.pallas import tpu as pltpu

F32 = jnp.float32
BF16 = jnp.bfloat16
I32 = jnp.int32
I16 = jnp.int16

DEPTH = 2
CHUNK = 64
SSM_GROUP = 16
STATE = 64
N_HEADS = 8
HEAD_DIM = 64
IDX_HEADS = 8
IDX_DIM = 32
TOPK_MAX = 256
ATTN_SCALE = HEAD_DIM ** -0.5
LOG2E = math.log2(math.e)
N_BUCKETS = 32
MAX_DIST = 128
ALPHA = (2 * DEPTH) ** 0.25
LN_EPS = 1e-5

V7X_VMEM_LIMIT_BYTES = 58 * 1024 * 1024
LANES = 128
FFN_TM = 512
FFN_FC = 256
PROJ_TM = 512
MERGE_TM = 512
SSM_T = 64
DSA_TQ = 256
DSA_TK = 256
COUNT_TILES = 2
MASK_NEG = -1e30
V_ROWS = HEAD_DIM + 16
KEY_NEG_INF = int(np.int32(np.uint32(0x807FFFFF)))
HALF16 = 1 << 15
MIN_SAFE_DEN = 2.0 ** -80
BOUND_SLACK = 1.02


def _resident(shape):
    nd = len(shape)
    return pl.BlockSpec(shape, lambda *_: (0,) * nd, pipeline_mode=pl.Buffered(1))


def _layer_norm(y, g, b):
    mu = jnp.mean(y, axis=-1, keepdims=True)
    yc = y - mu
    var = jnp.mean(yc * yc, axis=-1, keepdims=True)
    return yc * lax.rsqrt(var + LN_EPS) * g + b


NT_DIMS = (((1,), (1,)), ((), ()))


def _ffn_ln_kernel(x_ref, wup_ref, wdn_ref, g_ref, b_ref, o_ref, *, d_ff, fc):
    x = x_ref[...]
    xb = x.astype(BF16)
    acc = jnp.zeros(x.shape, F32)
    for j in range(d_ff // fc):
        gate = jnp.dot(xb, wup_ref[:, j * fc:(j + 1) * fc], preferred_element_type=F32)
        up = jnp.dot(xb, wup_ref[:, d_ff + j * fc:d_ff + (j + 1) * fc], preferred_element_type=F32)
        a = (gate * jax.nn.sigmoid(gate)) * up
        acc = acc + jnp.dot(a.astype(BF16), wdn_ref[j * fc:(j + 1) * fc, :], preferred_element_type=F32)
    o_ref[...] = _layer_norm(ALPHA * x + 0.5 * acc, g_ref[...], b_ref[...])


def _ffn_ln(x2, w_up, w_down, g, b):
    n, d = x2.shape
    d_ff = w_down.shape[0]
    tm = min(FFN_TM, n)
    assert n % tm == 0 and d_ff % FFN_FC == 0
    return pl.pallas_call(
        functools.partial(_ffn_ln_kernel, d_ff=d_ff, fc=FFN_FC),
        out_shape=jax.ShapeDtypeStruct((n, d), F32),
        grid=(n // tm,),
        in_specs=[pl.BlockSpec((tm, d), lambda i: (i, 0)),
                  _resident(w_up.shape), _resident(w_down.shape),
                  _resident((1, d)), _resident((1, d))],
        out_specs=pl.BlockSpec((tm, d), lambda i: (i, 0)),
        compiler_params=pltpu.CompilerParams(dimension_semantics=("parallel",),
                                             vmem_limit_bytes=V7X_VMEM_LIMIT_BYTES),
        name="ffn_ln",
    )(x2, w_up, w_down, g.reshape(1, d), b.reshape(1, d))


def _in_proj_kernel(x_ref, w_ref, wt_ref, head_ind_ref, q_ref, k_ref, kidx_ref, gs_ref, ga_ref, stats_ref,
                    ut_ref, vt_ref, qit_ref, kwt_ref, *, offs, offs_t):
    xb = x_ref[...].astype(BF16)

    def proj(name):
        lo, hi = offs[name]
        return jnp.dot(xb, w_ref[:, lo:hi], preferred_element_type=F32)

    def proj_t(name):
        lo, hi = offs_t[name]
        return lax.dot_general(wt_ref[lo:hi, :], xb, NT_DIMS, preferred_element_type=F32)

    ut_ref[...] = proj_t("u")
    q_b = (proj("q") * (ATTN_SCALE * LOG2E)).astype(BF16)
    k_b = proj("k").astype(BF16)
    q_ref[...] = q_b
    k_ref[...] = k_b

    def head_sq_max(v_b):
        per_head = jnp.dot(jnp.square(v_b.astype(F32)).astype(BF16), head_ind_ref[...], preferred_element_type=F32)
        return jnp.max(per_head, axis=0, keepdims=True)

    stats_ref[0] = jnp.concatenate([head_sq_max(q_b), head_sq_max(k_b), jnp.zeros((6, LANES), F32)], axis=0)
    kidx_ref[...] = proj("kidx")[:, :IDX_DIM].astype(BF16)
    gs_ref[...] = jax.nn.sigmoid(proj("gs")).astype(BF16)
    ga_ref[...] = jax.nn.sigmoid(proj("ga")).astype(BF16)
    qit_ref[...] = proj_t("qi").astype(BF16)
    kwt_ref[...] = proj_t("kw")
    v_t = proj_t("v").astype(BF16)
    tm = v_t.shape[1]
    pad_rows = lax.broadcasted_iota(I32, (V_ROWS - HEAD_DIM, tm), 0)
    pad = jnp.where(pad_rows == 0, 1.0, 0.0).astype(BF16)
    for h in range(N_HEADS):
        vt_ref[h * V_ROWS:h * V_ROWS + HEAD_DIM, :] = v_t[h * HEAD_DIM:(h + 1) * HEAD_DIM, :]
        vt_ref[h * V_ROWS + HEAD_DIM:(h + 1) * V_ROWS, :] = pad


def _in_proj(x2, w_nat, offs, w_t, offs_t, seq):
    n, d = x2.shape
    tm = min(PROJ_TM, seq)
    assert seq % tm == 0 and n % seq == 0
    d_attn = offs["q"][1] - offs["q"][0]
    head_ind = (np.arange(d_attn)[:, None] // HEAD_DIM == np.arange(LANES)[None, :]).astype(np.float32)
    width = lambda name: offs[name][1] - offs[name][0]
    rows_t = lambda name: offs_t[name][1] - offs_t[name][0]
    tok = lambda w, dt: (jax.ShapeDtypeStruct((n, w), dt), pl.BlockSpec((tm, w), lambda i: (i, 0)))
    chan = lambda r, dt: (jax.ShapeDtypeStruct((r, n), dt), pl.BlockSpec((r, tm), lambda i: (0, i)))
    outs = [tok(width("q"), BF16), tok(width("k"), BF16), tok(IDX_DIM, BF16),
            tok(width("gs"), BF16), tok(width("ga"), BF16),
            (jax.ShapeDtypeStruct((n // tm, 8, LANES), F32), pl.BlockSpec((1, 8, LANES), lambda i: (i, 0, 0))),
            chan(rows_t("u"), F32), chan(N_HEADS * V_ROWS, BF16), chan(rows_t("qi"), BF16), chan(rows_t("kw"), F32)]
    return pl.pallas_call(
        functools.partial(_in_proj_kernel, offs=offs, offs_t=offs_t),
        out_shape=[o[0] for o in outs],
        grid=(n // tm,),
        in_specs=[pl.BlockSpec((tm, d), lambda i: (i, 0)), _resident(w_nat.shape), _resident(w_t.shape),
                  _resident(head_ind.shape)],
        out_specs=[o[1] for o in outs],
        compiler_params=pltpu.CompilerParams(dimension_semantics=("parallel",),
                                             vmem_limit_bytes=V7X_VMEM_LIMIT_BYTES),
        name="in_proj",
    )(x2, w_nat, w_t, jnp.asarray(head_ind, BF16))


def _ssm_kernel(u_ref, m_ref, w_ref, v_ref, apow_ref, d_ref, y_ref, u_scr, y_scr, m_scr, *, rows_per_seq):
    t_len = u_ref.shape[2] // 2
    n_rows = u_ref.shape[1]
    even, odd = slice(0, n_rows), slice(n_rows, 2 * n_rows)
    for h in range(SSM_GROUP):
        cols = slice(h * t_len, (h + 1) * t_len)
        u_scr[even, cols] = u_ref[h, :, :t_len]
        u_scr[odd, cols] = u_ref[h, :, t_len:]
    u = u_scr[...]
    ub = u.astype(BF16)
    s = jnp.dot(ub, w_ref[0], preferred_element_type=F32)
    s_even, s_odd = s[even], s[odd]
    row = lax.broadcasted_iota(I32, (n_rows, 2 * STATE), 0) % rows_per_seq

    def times_a(val, k):
        a_same = apow_ref[0, 2 * k:2 * k + 1, :]
        a_cross = apow_ref[0, 2 * k + 1:2 * k + 2, :]
        return val * a_same + pltpu.roll(val, STATE, 1) * a_cross

    def shifted(val, by):
        return jnp.where(row >= by, pltpu.roll(val, by, 0), 0.0)

    x = times_a(s_even, 0) + s_odd
    for k in range(1, rows_per_seq.bit_length()):
        x = x + times_a(shifted(x, 1 << (k - 1)), k)
    h_even = shifted(x, 1)
    h_odd = times_a(h_even, 0) + s_even
    h_in = jnp.concatenate([h_even, h_odd], axis=0)
    for k in range(SSM_GROUP):
        for j in range(SSM_GROUP // 2):
            m_scr[k * t_len:(k + 1) * t_len, j * LANES:(j + 1) * LANES] = m_ref[0, k, j]
    y = jnp.dot(ub, m_scr[...], preferred_element_type=F32)
    y = y + jnp.dot(h_in.astype(BF16), v_ref[0], preferred_element_type=F32)
    y_scr[...] = y + d_ref[0] * u
    for h in range(SSM_GROUP):
        cols = slice(h * t_len, (h + 1) * t_len)
        y_ref[h, :, :t_len] = y_scr[even, cols]
        y_ref[h, :, t_len:] = y_scr[odd, cols]


def _ssm_tables(lam_re, lam_im, log_dt, b_re, b_im, c_re, c_im, d_skip, t_len, chunks_per_seq):
    g = lam_re.shape[0]
    lam = lax.complex(jnp.minimum(lam_re.astype(F32), -1e-4), lam_im.astype(F32))
    dt = jnp.exp(log_dt.astype(F32))[:, None]
    a_bar = jnp.exp(lam * dt)
    b_bar = ((a_bar - 1.0) / lam)[:, :, None] * lax.complex(b_re.astype(F32), b_im.astype(F32))
    c_mat = lax.complex(c_re.astype(F32), c_im.astype(F32))
    taus = jnp.arange(t_len + 1, dtype=F32)
    apow = jnp.exp((lam * dt)[:, None, :] * taus[None, :, None])
    hp = lax.Precision.HIGHEST
    kern = jnp.einsum("ghp,gtp,gpk->gkht", c_mat, apow[:, :t_len], b_bar, precision=hp).real
    pos = np.arange(t_len)
    shift = (pos[None, None, :] - pos[None, :, None]) == pos[:, None, None]
    pair_shift = np.einsum("ab,lst->alsbt", np.eye(2, dtype=bool), shift).reshape(2 * t_len, t_len, 2 * t_len)
    kern_pairs = kern.astype(BF16).reshape(g, SSM_GROUP, SSM_GROUP // 2, 2 * t_len)
    m_tab = jnp.einsum("gkjc,csd->gkjsd", kern_pairs, jnp.asarray(pair_shift, BF16), preferred_element_type=BF16)
    w_c = b_bar.transpose(0, 2, 1)[:, :, None, :] * apow[:, :t_len][:, None, ::-1, :]
    w_c = w_c.reshape(g, SSM_GROUP * t_len, STATE)
    w_tab = jnp.concatenate([w_c.real, w_c.imag], axis=-1)
    v_c = c_mat.transpose(0, 2, 1)[:, :, :, None] * apow[:, 1:t_len + 1].transpose(0, 2, 1)[:, :, None, :]
    v_c = v_c.reshape(g, STATE, SSM_GROUP * t_len)
    v_tab = jnp.concatenate([v_c.real, -v_c.imag], axis=1)
    n_steps = chunks_per_seq.bit_length() - 1
    steps = (t_len * (1 << jnp.arange(n_steps))).astype(F32)
    a_step = jnp.exp((lam * dt)[:, None, :] * steps[None, :, None])
    a_rows = jnp.stack([jnp.concatenate([a_step.real, a_step.real], axis=-1),
                        jnp.concatenate([-a_step.imag, a_step.imag], axis=-1)], axis=2)
    a_rows = a_rows.reshape(g, 2 * n_steps, 2 * STATE)
    d_tab = jnp.repeat(d_skip.astype(F32).reshape(g, 1, SSM_GROUP), t_len, axis=2)
    return m_tab, w_tab.astype(BF16), v_tab.astype(BF16), a_rows, d_tab


def _ssm(u_t, tables, seq):
    m_tab, w_tab, v_tab, a_rows, d_tab = tables
    g = m_tab.shape[0]
    d_ssm, n = u_t.shape
    t_len = SSM_T
    rows_per_seq = seq // (2 * t_len)
    assert 2 * t_len == LANES and seq % LANES == 0 and rows_per_seq & (rows_per_seq - 1) == 0
    assert d_ssm == g * SSM_GROUP
    rows, cols = n // LANES, t_len * SSM_GROUP
    per_group = lambda shape: pl.BlockSpec((1,) + shape, lambda i: (i, 0, 0))
    channels = pl.BlockSpec((SSM_GROUP, rows, LANES), lambda i: (i, 0, 0))
    y_t = pl.pallas_call(
        functools.partial(_ssm_kernel, rows_per_seq=rows_per_seq),
        out_shape=jax.ShapeDtypeStruct((d_ssm, rows, LANES), F32),
        grid=(g,),
        in_specs=[channels, pl.BlockSpec((1,) + m_tab.shape[1:], lambda i: (i, 0, 0, 0, 0)),
                  per_group((cols, 2 * STATE)), per_group((2 * STATE, cols)),
                  per_group(a_rows.shape[1:]), per_group((1, cols))],
        out_specs=channels,
        scratch_shapes=[pltpu.VMEM((2 * rows, cols), F32), pltpu.VMEM((2 * rows, cols), F32),
                        pltpu.VMEM((cols, cols), BF16)],
        compiler_params=pltpu.CompilerParams(dimension_semantics=("parallel",),
                                             vmem_limit_bytes=V7X_VMEM_LIMIT_BYTES),
        name="ssm",
    )(u_t.reshape(d_ssm, rows, LANES), m_tab, w_tab, v_tab, a_rows, d_tab)
    return y_t.reshape(d_ssm, n)


def _t5_bucket_np(rel):
    half = N_BUCKETS // 2
    max_exact = half // 2
    ret = np.where(rel > 0, half, 0)
    n = np.abs(rel)
    n_f = np.maximum(n, 1).astype(np.float32)
    large = max_exact + (np.log(n_f / np.float32(max_exact)) / np.float32(math.log(MAX_DIST / max_exact))
                         * (half - max_exact)).astype(np.int32)
    large = np.minimum(large, half - 1)
    return ret + np.where(n < max_exact, n, large)


def _near_bias_tables(rel_bias, tq, tk):
    kk = np.arange(tk)[:, None]
    qq = np.arange(tq)[None, :]
    b_diag = _t5_bucket_np(kk - qq)
    b_prev = _t5_bucket_np(kk - qq - tk)
    far = int(_t5_bucket_np(np.array(-(tk + 1))))
    assert far == int(_t5_bucket_np(np.array(-(1 << 24)))), "far keys must share one bucket"
    tab = rel_bias.astype(F32)
    one_hot = jax.nn.one_hot(np.stack([b_diag, b_prev]), N_BUCKETS, dtype=F32)
    near = jnp.einsum("wkqb,bh->hwkq", one_hot, tab - tab[far][None, :], precision=lax.Precision.HIGHEST)
    return near * LOG2E


def _sortable(x):
    b = pltpu.bitcast(x, I32)
    return b ^ ((b >> 31) & 0x7FFFFFFF)


def _pair_lanes(h):
    return slice((h // 2) * 2 * HEAD_DIM, (h // 2 + 1) * 2 * HEAD_DIM)


def _dsa_kernel(qit_ref, wi_ref, q_ref, bound_ref, kidx_ref, k_ref, vt_ref, near_ref, o_ref,
                key_scr, k16_scr, mb_scr, qh_scr, m_scr, p_scr, acc_scr, out_scr, *, tq, tk, n_sel):
    i = pl.program_id(1)
    n_tiles = i + 1
    tile_rows = lambda t: pl.ds(pl.multiple_of(t * tk, tk), tk)

    def score_tile(t):
        kid = kidx_ref[tile_rows(t), :]
        acc = jnp.zeros((tk, tq), F32)
        for h in range(IDX_HEADS):
            s = jnp.dot(kid, qit_ref[h * IDX_DIM:(h + 1) * IDX_DIM, :], preferred_element_type=F32)
            acc = acc + wi_ref[h:h + 1, :] * jnp.maximum(s, 0.0)
        return acc

    def store_keys(t, score):
        key = _sortable(score)
        key_scr[tile_rows(t), :] = key
        k16_scr[tile_rows(t), :] = (key >> 16).astype(I16)

    def for_tiles(n, tile_fn):
        def pair_body(j, carry):
            tile_fn(2 * j)
            tile_fn(2 * j + 1)
            return carry

        lax.fori_loop(0, n // 2, pair_body, 0)

        @pl.when(n % 2 == 1)
        def _():
            tile_fn(n - 1)

    for_tiles(i, lambda t: store_keys(t, score_tile(t) + 0.0))
    kpos = lax.broadcasted_iota(I32, (tk, tq), 0)
    qpos = lax.broadcasted_iota(I32, (tk, tq), 1)
    admissible = (kpos // CHUNK) <= (qpos // CHUNK)
    store_keys(i, jnp.where(admissible, score_tile(i) + 0.0, -jnp.inf))
    for extra in range(COUNT_TILES - 1):
        k16_scr[tile_rows(n_tiles + extra), :] = jnp.full((tk, tq), -HALF16, I16)

    def count16(cand):
        c16 = cand.astype(I16)

        def group_body(j, c):
            rows = pl.ds(pl.multiple_of(j * (COUNT_TILES * tk), COUNT_TILES * tk), COUNT_TILES * tk)
            hit = jnp.where(k16_scr[rows, :] >= c16, jnp.int16(1), jnp.int16(0))
            parts = [hit[r * 16:(r + 1) * 16] for r in range(COUNT_TILES * tk // 16)]
            while len(parts) > 1:
                parts = [a + b for a, b in zip(parts[0::2], parts[1::2])]
            return c + parts[0]

        n_groups = (n_tiles + COUNT_TILES - 1) // COUNT_TILES
        part = lax.fori_loop(0, n_groups, group_body, jnp.zeros((16, tq), I16))
        return jnp.sum(part.astype(I32), axis=0, keepdims=True)

    def radix16(c_low):
        digit = jnp.zeros((1, tq), I32)
        for b in range(15, -1, -1):
            cand = digit | (1 << b)
            cnt = count16(cand - HALF16)
            accept = cnt >= n_sel
            digit = jnp.where(accept, cand, digit)
            c_low = jnp.where(accept, cnt, c_low)
        return digit, c_low

    q_in_blk = lax.broadcasted_iota(I32, (1, tq), 1)
    n_adm = (i * tq + q_in_blk) // CHUNK * CHUNK + CHUNK
    take_all = n_adm <= n_sel

    hi_digit, c_low = radix16(jnp.full((1, tq), 1, I32) * (n_tiles * tk))
    hi16 = hi_digit - HALF16

    def low_digit_tile(t):
        key = key_scr[tile_rows(t), :]
        hi = key >> 16
        low = (key & 0xFFFF) - HALF16
        k16_scr[tile_rows(t), :] = jnp.where(hi > hi16, HALF16 - 1, jnp.where(hi < hi16, -HALF16, low)).astype(I16)

    for_tiles(n_tiles, low_digit_tile)
    lo_digit, c_low = radix16(c_low)
    thr = jnp.where(take_all, KEY_NEG_INF + 1, jnp.left_shift(hi16, 16) | lo_digit)
    excess = jnp.where(take_all, 0, c_low - n_sel)
    any_excess = jnp.max(excess) > 0

    neg_shift = -bound_ref[0:1, :]
    @pl.when(jnp.logical_not(any_excess))
    def _():
        def mask_tile(t):
            mb_scr[tile_rows(t), :] = jnp.where(key_scr[tile_rows(t), :] >= thr, neg_shift, MASK_NEG)

        for_tiles(n_tiles, mask_tile)

    @pl.when(any_excess)
    def _():
        def gt_body(t, c):
            hit = jnp.where(key_scr[tile_rows(t), :] > thr, 1, 0).astype(I32)
            return c + hit.reshape(tk // 8, 8, tq).sum(axis=0)

        n_gt = jnp.sum(lax.fori_loop(0, n_tiles, gt_body, jnp.zeros((8, tq), I32)), axis=0, keepdims=True)
        need = jnp.where(take_all, 2 * tk * n_tiles, n_sel - n_gt).astype(F32)
        r_i = lax.broadcasted_iota(I32, (tk, tk), 0)
        c_i = lax.broadcasted_iota(I32, (tk, tk), 1)
        strict_lower = jnp.where(c_i < r_i, 1.0, 0.0).astype(BF16)

        def mask_body(t, ties_before):
            kt = key_scr[tile_rows(t), :]
            eq = kt == thr
            eq_f = jnp.where(eq, 1.0, 0.0)
            rank = jnp.dot(strict_lower, eq_f.astype(BF16), preferred_element_type=F32) + ties_before
            sel = jnp.logical_or(kt > thr, jnp.logical_and(eq, rank < need))
            mb_scr[tile_rows(t), :] = jnp.where(sel, neg_shift, MASK_NEG)
            return ties_before + jnp.sum(eq_f.reshape(tk // 8, 8, tq).sum(axis=0), axis=0, keepdims=True)

        lax.fori_loop(0, n_tiles, mask_body, jnp.zeros((1, tq), F32))

    lane = lax.broadcasted_iota(I32, (tq, 2 * HEAD_DIM), 1)
    for h in range(N_HEADS):
        q_pair = q_ref[:, _pair_lanes(h)]
        own = (lane < HEAD_DIM) if h % 2 == 0 else (lane >= HEAD_DIM)
        qh_scr[h] = jnp.where(own, q_pair, jnp.zeros_like(q_pair))

    def masked_logits(t, h, near_which):
        lg = lax.dot_general(k_ref[tile_rows(t), _pair_lanes(h)], qh_scr[h], NT_DIMS,
                             preferred_element_type=F32) + mb_scr[tile_rows(t), :]
        if near_which is not None:
            lg = lg + near_ref[h, near_which]
        return lg

    def over_tiles(tile_fn):
        for_tiles(jnp.maximum(i - 1, 0), lambda t: tile_fn(t, None))

        @pl.when(i >= 1)
        def _():
            tile_fn(i - 1, 1)

        tile_fn(i, 0)

    def pv_tile_against(shift_row):
        def tile_fn(t, near_which):
            for h in range(N_HEADS):
                lg = masked_logits(t, h, near_which)
                if shift_row is not None:
                    lg = lg - shift_row(h)
                p_scr[h] = jnp.exp2(lg).astype(BF16)
            for h in range(N_HEADS):
                vrows = slice(h * V_ROWS, (h + 1) * V_ROWS)
                acc_scr[vrows, :] += jnp.dot(vt_ref[vrows, tile_rows(t)], p_scr[h], preferred_element_type=F32)
        return tile_fn

    den_row = lambda h: acc_scr[h * V_ROWS + HEAD_DIM:h * V_ROWS + HEAD_DIM + 1, :]

    acc_scr[...] = jnp.zeros(acc_scr.shape, F32)
    over_tiles(pv_tile_against(None))
    den_min = den_row(0)
    for h in range(1, N_HEADS):
        den_min = jnp.minimum(den_min, den_row(h))
    bound_too_loose = jnp.logical_not(jnp.min(den_min) >= MIN_SAFE_DEN)

    @pl.when(bound_too_loose)
    def _():
        m_scr[...] = jnp.full(m_scr.shape, MASK_NEG, F32)

        def max_tile(t, near_which):
            for h in range(N_HEADS):
                part = masked_logits(t, h, near_which).reshape(tk // 8, 8, tq).max(axis=0)
                m_scr[h * 8:(h + 1) * 8, :] = jnp.maximum(m_scr[h * 8:(h + 1) * 8, :], part)

        over_tiles(max_tile)
        for h in range(N_HEADS):
            m_scr[h * 8:(h + 1) * 8, :] = jnp.broadcast_to(
                jnp.max(m_scr[h * 8:(h + 1) * 8, :], axis=0, keepdims=True), (8, tq))
        acc_scr[...] = jnp.zeros(acc_scr.shape, F32)
        over_tiles(pv_tile_against(lambda h: m_scr[h * 8:h * 8 + 1, :]))

    for h in range(N_HEADS):
        out_scr[h * HEAD_DIM:(h + 1) * HEAD_DIM, :] = acc_scr[h * V_ROWS:h * V_ROWS + HEAD_DIM, :] / den_row(h)
    o_ref[...] = out_scr[...].T


def _logit_bound(stats, near, bsz, tq):
    sq = jnp.max(stats.reshape(bsz, -1, *stats.shape[1:]), axis=1)[:, :2, :N_HEADS]
    norms = jnp.sqrt(sq)
    near_max = jnp.maximum(jnp.max(near, axis=(1, 2, 3)), 0.0)
    bound = jnp.max(norms[:, 0] * norms[:, 1] * BOUND_SLACK + near_max + 1.0, axis=1)
    return jnp.broadcast_to(bound[:, None, None], (bsz, 8, tq))


def _dsa(q, k, kidx, vt, qit, kwt, stats, near, bsz, seq):
    d_attn = q.shape[1]
    tq, tk = DSA_TQ, DSA_TK
    assert tq == tk and seq % tq == 0 and tq % CHUNK == 0 and IDX_DIM % IDX_HEADS == 0
    n_sel = min(TOPK_MAX, seq // 4)
    nq = seq // tq
    bound = _logit_bound(stats, near, bsz, tq)
    once = dict(pipeline_mode=pl.Buffered(1))
    return pl.pallas_call(
        functools.partial(_dsa_kernel, tq=tq, tk=tk, n_sel=n_sel),
        out_shape=jax.ShapeDtypeStruct((bsz * seq, d_attn), F32),
        grid=(bsz, nq),
        in_specs=[pl.BlockSpec((IDX_HEADS * IDX_DIM, tq), lambda b, i: (0, b * nq + i)),
                  pl.BlockSpec((IDX_HEADS, tq), lambda b, i: (IDX_DIM // IDX_HEADS, b * nq + i)),
                  pl.BlockSpec((tq, d_attn), lambda b, i: (b * nq + i, 0)),
                  pl.BlockSpec((None, 8, tq), lambda b, i: (b, 0, 0)),
                  pl.BlockSpec((seq, IDX_DIM), lambda b, i: (b, 0), **once),
                  pl.BlockSpec((seq, d_attn), lambda b, i: (b, 0), **once),
                  pl.BlockSpec((N_HEADS * V_ROWS, seq), lambda b, i: (0, b), **once),
                  _resident(near.shape)],
        out_specs=pl.BlockSpec((tq, d_attn), lambda b, i: (b * nq + i, 0)),
        scratch_shapes=[pltpu.VMEM((seq, tq), I32), pltpu.VMEM((seq + (COUNT_TILES - 1) * tk, tq), I16), pltpu.VMEM((seq, tq), F32),
                        pltpu.VMEM((N_HEADS, tq, 2 * HEAD_DIM), BF16), pltpu.VMEM((N_HEADS * 8, tq), F32),
                        pltpu.VMEM((N_HEADS, tk, tq), BF16),
                        pltpu.VMEM((N_HEADS * V_ROWS, tq), F32), pltpu.VMEM((d_attn, tq), F32)],
        compiler_params=pltpu.CompilerParams(dimension_semantics=("parallel", "arbitrary"),
                                             vmem_limit_bytes=V7X_VMEM_LIMIT_BYTES),
        name="dsa",
    )(qit, kwt, q, bound, kidx, k, vt, near)


def _merge_ln_kernel(x_ref, ys_ref, ya_ref, gs_ref, ga_ref, wglu_ref, wb0_ref, wb1_ref, wout_ref,
                     g_ref, b_ref, o_ref):
    y = jax.nn.gelu(ys_ref[...].T)
    glu = y * jax.nn.sigmoid(jnp.dot(y.astype(BF16), wglu_ref[...], preferred_element_type=F32))
    p_ssm = jnp.dot(glu.astype(BF16), wb0_ref[...], preferred_element_type=F32)
    p_att = jnp.dot(ya_ref[...].astype(BF16), wb1_ref[...], preferred_element_type=F32)
    merged = gs_ref[...].astype(F32) * p_ssm + ga_ref[...].astype(F32) * p_att
    mix = jnp.dot(merged.astype(BF16), wout_ref[...], preferred_element_type=F32)
    o_ref[...] = _layer_norm(ALPHA * x_ref[...] + mix, g_ref[...], b_ref[...])


def _merge_ln(x2, y_ssm_t, y_attn, gs, ga, w_glu, wb0, wb1, w_out, g, b):
    n, d = x2.shape
    tm = min(MERGE_TM, n)
    assert n % tm == 0
    tile = lambda a: pl.BlockSpec((tm, a.shape[1]), lambda i: (i, 0))
    return pl.pallas_call(
        _merge_ln_kernel,
        out_shape=jax.ShapeDtypeStruct((n, d), F32),
        grid=(n // tm,),
        in_specs=[tile(x2), pl.BlockSpec((y_ssm_t.shape[0], tm), lambda i: (0, i)), tile(y_attn), tile(gs), tile(ga),
                  _resident(w_glu.shape), _resident(wb0.shape), _resident(wb1.shape), _resident(w_out.shape),
                  _resident((1, d)), _resident((1, d))],
        out_specs=pl.BlockSpec((tm, d), lambda i: (i, 0)),
        compiler_params=pltpu.CompilerParams(dimension_semantics=("parallel",),
                                             vmem_limit_bytes=V7X_VMEM_LIMIT_BYTES),
        name="merge_ln",
    )(x2, y_ssm_t, y_attn, gs, ga, w_glu, wb0, wb1, w_out, g.reshape(1, d), b.reshape(1, d))


def _pack_cols(parts):
    offs, pos = {}, 0
    for name, w in parts.items():
        offs[name] = (pos, pos + w.shape[1])
        pos += w.shape[1]
    return jnp.concatenate(list(parts.values()), axis=1), offs


def _split_w_in(w_in, d_model, d_ssm, d_attn):
    sizes = (d_ssm, d_attn, d_attn, d_attn, IDX_HEADS * IDX_DIM, IDX_DIM, IDX_HEADS, d_model, d_model)
    assert w_in.shape[1] == sum(sizes)
    edges = np.concatenate([[0], np.cumsum(sizes)])
    u, q, k, v, qi, ki, wi, gs, ga = [w_in[:, int(edges[j]):int(edges[j + 1])] for j in range(len(sizes))]
    zeros = lambda c: jnp.zeros((w_in.shape[0], c), w_in.dtype)
    w_nat, offs = _pack_cols(dict(q=q, k=k, kidx=jnp.concatenate([ki, zeros(LANES - IDX_DIM)], axis=1),
                                  gs=gs, ga=ga))
    kw = jnp.concatenate([ki, wi, zeros(LANES - IDX_DIM - IDX_HEADS)], axis=1)
    w_t, offs_t = _pack_cols(dict(u=u, v=v, qi=qi, kw=kw))
    return w_nat.astype(BF16), offs, w_t.T.astype(BF16), offs_t


def kernel(x, ffn1_w_up, ffn1_w_down, ln1_g, ln1_b, w_in, ssm_lam_re, ssm_lam_im, ssm_log_dt, ssm_b_re, ssm_b_im, ssm_c_re, ssm_c_im, ssm_d, ssm_w_glu, w_branch, w_out, ln2_g, ln2_b, ffn2_w_up, ffn2_w_down, ln3_g, ln3_b, rel_bias):
    bsz, seq, d_model = x.shape
    d_ssm = ssm_w_glu.shape[1]
    d_attn = w_branch.shape[2]
    x2 = x.reshape(bsz * seq, d_model)
    near = _near_bias_tables(rel_bias, DSA_TQ, DSA_TK)
    for l in range(ffn1_w_up.shape[0]):
        x2 = _ffn_ln(x2, ffn1_w_up[l].astype(BF16), ffn1_w_down[l].astype(BF16), ln1_g[l], ln1_b[l])
        q, k, kidx, gs, ga, stats, ut, vt, qit, kwt = _in_proj(x2, *_split_w_in(w_in[l], d_model, d_ssm, d_attn), seq)
        tables = _ssm_tables(ssm_lam_re[l], ssm_lam_im[l], ssm_log_dt[l], ssm_b_re[l], ssm_b_im[l],
                             ssm_c_re[l], ssm_c_im[l], ssm_d[l], SSM_T, seq // SSM_T)
        y_ssm = _ssm(ut, tables, seq)
        y_attn = _dsa(q, k, kidx, vt, qit, kwt, stats, near, bsz, seq)
        x2 = _merge_ln(x2, y_ssm, y_attn, gs, ga, ssm_w_glu[l].astype(BF16), w_branch[l, 0].astype(BF16),
                       w_branch[l, 1].astype(BF16), w_out[l].astype(BF16), ln2_g[l], ln2_b[l])
        x2 = _ffn_ln(x2, ffn2_w_up[l].astype(BF16), ffn2_w_down[l].astype(BF16), ln3_g[l], ln3_b[l])
    return x2.reshape(bsz, seq, d_model)
```

```python
import functools
import math

import numpy as np
import jax
import jax.numpy as jnp
from jax import lax
from jax.experimental import pallas as pl
from jax.experimental.pallas import tpu as pltpu

F32 = jnp.float32
BF16 = jnp.bfloat16
I32 = jnp.int32
I16 = jnp.int16

DEPTH = 2
CHUNK = 64
SSM_GROUP = 16
STATE = 64
N_HEADS = 8
HEAD_DIM = 64
IDX_HEADS = 8
IDX_DIM = 32
TOPK_MAX = 256
ATTN_SCALE = HEAD_DIM ** -0.5
LOG2E = math.log2(math.e)
N_BUCKETS = 32
MAX_DIST = 128
ALPHA = (2 * DEPTH) ** 0.25
LN_EPS = 1e-5

V7X_VMEM_LIMIT_BYTES = 58 * 1024 * 1024
LANES = 128
FFN_TM = 512
FFN_FC = 256
PROJ_TM = 512
MERGE_TM = 512
SSM_T = 64
DSA_TQ = 256
DSA_TK = 256
COUNT_TILES = 2
MASK_NEG = -1e30
V_ROWS = HEAD_DIM + 16
KEY_NEG_INF = int(np.int32(np.uint32(0x807FFFFF)))
HALF16 = 1 << 15
MIN_SAFE_DEN = 2.0 ** -80
BOUND_SLACK = 1.02


def _resident(shape):
    nd = len(shape)
    return pl.BlockSpec(shape, lambda *_: (0,) * nd, pipeline_mode=pl.Buffered(1))


def _layer_norm(y, g, b):
    mu = jnp.mean(y, axis=-1, keepdims=True)
    yc = y - mu
    var = jnp.mean(yc * yc, axis=-1, keepdims=True)
    return yc * lax.rsqrt(var + LN_EPS) * g + b


NT_DIMS = (((1,), (1,)), ((), ()))


def _ffn_ln_kernel(x_ref, wup_ref, wdn_ref, g_ref, b_ref, o_ref, *, d_ff, fc):
    x = x_ref[...]
    xb = x.astype(BF16)
    acc = jnp.zeros(x.shape, F32)
    for j in range(d_ff // fc):
        gate = jnp.dot(xb, wup_ref[:, j * fc:(j + 1) * fc], preferred_element_type=F32)
        up = jnp.dot(xb, wup_ref[:, d_ff + j * fc:d_ff + (j + 1) * fc], preferred_element_type=F32)
        a = (gate * jax.nn.sigmoid(gate)) * up
        acc = acc + jnp.dot(a.astype(BF16), wdn_ref[j * fc:(j + 1) * fc, :], preferred_element_type=F32)
    o_ref[...] = _layer_norm(ALPHA * x + 0.5 * acc, g_ref[...], b_ref[...])


def _ffn_ln(x2, w_up, w_down, g, b):
    n, d = x2.shape
    d_ff = w_down.shape[0]
    tm = min(FFN_TM, n)
    assert n % tm == 0 and d_ff % FFN_FC == 0
    return pl.pallas_call(
        functools.partial(_ffn_ln_kernel, d_ff=d_ff, fc=FFN_FC),
        out_shape=jax.ShapeDtypeStruct((n, d), F32),
        grid=(n // tm,),
        in_specs=[pl.BlockSpec((tm, d), lambda i: (i, 0)),
                  _resident(w_up.shape), _resident(w_down.shape),
                  _resident((1, d)), _resident((1, d))],
        out_specs=pl.BlockSpec((tm, d), lambda i: (i, 0)),
        compiler_params=pltpu.CompilerParams(dimension_semantics=("parallel",),
                                             vmem_limit_bytes=V7X_VMEM_LIMIT_BYTES),
        name="ffn_ln",
    )(x2, w_up, w_down, g.reshape(1, d), b.reshape(1, d))


def _in_proj_kernel(x_ref, w_ref, wt_ref, head_ind_ref, q_ref, k_ref, kidx_ref, gs_ref, ga_ref, stats_ref,
                    ut_ref, vt_ref, qit_ref, kwt_ref, *, offs, offs_t):
    xb = x_ref[...].astype(BF16)

    def proj(name):
        lo, hi = offs[name]
        return jnp.dot(xb, w_ref[:, lo:hi], preferred_element_type=F32)

    def proj_t(name):
        lo, hi = offs_t[name]
        return lax.dot_general(wt_ref[lo:hi, :], xb, NT_DIMS, preferred_element_type=F32)

    ut_ref[...] = proj_t("u")
    q_b = (proj("q") * (ATTN_SCALE * LOG2E)).astype(BF16)
    k_b = proj("k").astype(BF16)
    q_ref[...] = q_b
    k_ref[...] = k_b

    def head_sq_max(v_b):
        per_head = jnp.dot(jnp.square(v_b.astype(F32)).astype(BF16), head_ind_ref[...], preferred_element_type=F32)
        return jnp.max(per_head, axis=0, keepdims=True)

    stats_ref[0] = jnp.concatenate([head_sq_max(q_b), head_sq_max(k_b), jnp.zeros((6, LANES), F32)], axis=0)
    kidx_ref[...] = proj("kidx")[:, :IDX_DIM].astype(BF16)
    gs_ref[...] = jax.nn.sigmoid(proj("gs")).astype(BF16)
    ga_ref[...] = jax.nn.sigmoid(proj("ga")).astype(BF16)
    qit_ref[...] = proj_t("qi").astype(BF16)
    kwt_ref[...] = proj_t("kw")
    v_t = proj_t("v").astype(BF16)
    tm = v_t.shape[1]
    pad_rows = lax.broadcasted_iota(I32, (V_ROWS - HEAD_DIM, tm), 0)
    pad = jnp.where(pad_rows == 0, 1.0, 0.0).astype(BF16)
    for h in range(N_HEADS):
        vt_ref[h * V_ROWS:h * V_ROWS + HEAD_DIM, :] = v_t[h * HEAD_DIM:(h + 1) * HEAD_DIM, :]
        vt_ref[h * V_ROWS + HEAD_DIM:(h + 1) * V_ROWS, :] = pad


def _in_proj(x2, w_nat, offs, w_t, offs_t, seq):
    n, d = x2.shape
    tm = min(PROJ_TM, seq)
    assert seq % tm == 0 and n % seq == 0
    d_attn = offs["q"][1] - offs["q"][0]
    head_ind = (np.arange(d_attn)[:, None] // HEAD_DIM == np.arange(LANES)[None, :]).astype(np.float32)
    width = lambda name: offs[name][1] - offs[name][0]
    rows_t = lambda name: offs_t[name][1] - offs_t[name][0]
    tok = lambda w, dt: (jax.ShapeDtypeStruct((n, w), dt), pl.BlockSpec((tm, w), lambda i: (i, 0)))
    chan = lambda r, dt: (jax.ShapeDtypeStruct((r, n), dt), pl.BlockSpec((r, tm), lambda i: (0, i)))
    outs = [tok(width("q"), BF16), tok(width("k"), BF16), tok(IDX_DIM, BF16),
            tok(width("gs"), BF16), tok(width("ga"), BF16),
            (jax.ShapeDtypeStruct((n // tm, 8, LANES), F32), pl.BlockSpec((1, 8, LANES), lambda i: (i, 0, 0))),
            chan(rows_t("u"), F32), chan(N_HEADS * V_ROWS, BF16), chan(rows_t("qi"), BF16), chan(rows_t("kw"), F32)]
    return pl.pallas_call(
        functools.partial(_in_proj_kernel, offs=offs, offs_t=offs_t),
        out_shape=[o[0] for o in outs],
        grid=(n // tm,),
        in_specs=[pl.BlockSpec((tm, d), lambda i: (i, 0)), _resident(w_nat.shape), _resident(w_t.shape),
                  _resident(head_ind.shape)],
        out_specs=[o[1] for o in outs],
        compiler_params=pltpu.CompilerParams(dimension_semantics=("parallel",),
                                             vmem_limit_bytes=V7X_VMEM_LIMIT_BYTES),
        name="in_proj",
    )(x2, w_nat, w_t, jnp.asarray(head_ind, BF16))


def _ssm_kernel(u_ref, m_ref, w_ref, v_ref, apow_ref, d_ref, y_ref, u_scr, y_scr, m_scr, *, rows_per_seq):
    t_len = u_ref.shape[2] // 2
    n_rows = u_ref.shape[1]
    even, odd = slice(0, n_rows), slice(n_rows, 2 * n_rows)
    for h in range(SSM_GROUP):
        cols = slice(h * t_len, (h + 1) * t_len)
        u_scr[even, cols] = u_ref[h, :, :t_len]
        u_scr[odd, cols] = u_ref[h, :, t_len:]
    u = u_scr[...]
    ub = u.astype(BF16)
    s = jnp.dot(ub, w_ref[0], preferred_element_type=F32)
    s_even, s_odd = s[even], s[odd]
    row = lax.broadcasted_iota(I32, (n_rows, 2 * STATE), 0) % rows_per_seq

    def times_a(val, k):
        a_same = apow_ref[0, 2 * k:2 * k + 1, :]
        a_cross = apow_ref[0, 2 * k + 1:2 * k + 2, :]
        return val * a_same + pltpu.roll(val, STATE, 1) * a_cross

    def shifted(val, by):
        return jnp.where(row >= by, pltpu.roll(val, by, 0), 0.0)

    x = times_a(s_even, 0) + s_odd
    for k in range(1, rows_per_seq.bit_length()):
        x = x + times_a(shifted(x, 1 << (k - 1)), k)
    h_even = shifted(x, 1)
    h_odd = times_a(h_even, 0) + s_even
    h_in = jnp.concatenate([h_even, h_odd], axis=0)
    for k in range(SSM_GROUP):
        for j in range(SSM_GROUP // 2):
            m_scr[k * t_len:(k + 1) * t_len, j * LANES:(j + 1) * LANES] = m_ref[0, k, j]
    y = jnp.dot(ub, m_scr[...], preferred_element_type=F32)
    y = y + jnp.dot(h_in.astype(BF16), v_ref[0], preferred_element_type=F32)
    y_scr[...] = y + d_ref[0] * u
    for h in range(SSM_GROUP):
        cols = slice(h * t_len, (h + 1) * t_len)
        y_ref[h, :, :t_len] = y_scr[even, cols]
        y_ref[h, :, t_len:] = y_scr[odd, cols]


def _ssm_tables(lam_re, lam_im, log_dt, b_re, b_im, c_re, c_im, d_skip, t_len, chunks_per_seq):
    g = lam_re.shape[0]
    lam = lax.complex(jnp.minimum(lam_re.astype(F32), -1e-4), lam_im.astype(F32))
    dt = jnp.exp(log_dt.astype(F32))[:, None]
    a_bar = jnp.exp(lam * dt)
    b_bar = ((a_bar - 1.0) / lam)[:, :, None] * lax.complex(b_re.astype(F32), b_im.astype(F32))
    c_mat = lax.complex(c_re.astype(F32), c_im.astype(F32))
    taus = jnp.arange(t_len + 1, dtype=F32)
    apow = jnp.exp((lam * dt)[:, None, :] * taus[None, :, None])
    hp = lax.Precision.HIGHEST
    kern = jnp.einsum("ghp,gtp,gpk->gkht", c_mat, apow[:, :t_len], b_bar, precision=hp).real
    pos = np.arange(t_len)
    shift = (pos[None, None, :] - pos[None, :, None]) == pos[:, None, None]
    pair_shift = np.einsum("ab,lst->alsbt", np.eye(2, dtype=bool), shift).reshape(2 * t_len, t_len, 2 * t_len)
    kern_pairs = kern.astype(BF16).reshape(g, SSM_GROUP, SSM_GROUP // 2, 2 * t_len)
    m_tab = jnp.einsum("gkjc,csd->gkjsd", kern_pairs, jnp.asarray(pair_shift, BF16), preferred_element_type=BF16)
    w_c = b_bar.transpose(0, 2, 1)[:, :, None, :] * apow[:, :t_len][:, None, ::-1, :]
    w_c = w_c.reshape(g, SSM_GROUP * t_len, STATE)
    w_tab = jnp.concatenate([w_c.real, w_c.imag], axis=-1)
    v_c = c_mat.transpose(0, 2, 1)[:, :, :, None] * apow[:, 1:t_len + 1].transpose(0, 2, 1)[:, :, None, :]
    v_c = v_c.reshape(g, STATE, SSM_GROUP * t_len)
    v_tab = jnp.concatenate([v_c.real, -v_c.imag], axis=1)
    n_steps = chunks_per_seq.bit_length() - 1
    steps = (t_len * (1 << jnp.arange(n_steps))).astype(F32)
    a_step = jnp.exp((lam * dt)[:, None, :] * steps[None, :, None])
    a_rows = jnp.stack([jnp.concatenate([a_step.real, a_step.real], axis=-1),
                        jnp.concatenate([-a_step.imag, a_step.imag], axis=-1)], axis=2)
    a_rows = a_rows.reshape(g, 2 * n_steps, 2 * STATE)
    d_tab = jnp.repeat(d_skip.astype(F32).reshape(g, 1, SSM_GROUP), t_len, axis=2)
    return m_tab, w_tab.astype(BF16), v_tab.astype(BF16), a_rows, d_tab


def _ssm(u_t, tables, seq):
    m_tab, w_tab, v_tab, a_rows, d_tab = tables
    g = m_tab.shape[0]
    d_ssm, n = u_t.shape
    t_len = SSM_T
    rows_per_seq = seq // (2 * t_len)
    assert 2 * t_len == LANES and seq % LANES == 0 and rows_per_seq & (rows_per_seq - 1) == 0
    assert d_ssm == g * SSM_GROUP
    rows, cols = n // LANES, t_len * SSM_GROUP
    per_group = lambda shape: pl.BlockSpec((1,) + shape, lambda i: (i, 0, 0))
    channels = pl.BlockSpec((SSM_GROUP, rows, LANES), lambda i: (i, 0, 0))
    y_t = pl.pallas_call(
        functools.partial(_ssm_kernel, rows_per_seq=rows_per_seq),
        out_shape=jax.ShapeDtypeStruct((d_ssm, rows, LANES), F32),
        grid=(g,),
        in_specs=[channels, pl.BlockSpec((1,) + m_tab.shape[1:], lambda i: (i, 0, 0, 0, 0)),
                  per_group((cols, 2 * STATE)), per_group((2 * STATE, cols)),
                  per_group(a_rows.shape[1:]), per_group((1, cols))],
        out_specs=channels,
        scratch_shapes=[pltpu.VMEM((2 * rows, cols), F32), pltpu.VMEM((2 * rows, cols), F32),
                        pltpu.VMEM((cols, cols), BF16)],
        compiler_params=pltpu.CompilerParams(dimension_semantics=("parallel",),
                                             vmem_limit_bytes=V7X_VMEM_LIMIT_BYTES),
        name="ssm",
    )(u_t.reshape(d_ssm, rows, LANES), m_tab, w_tab, v_tab, a_rows, d_tab)
    return y_t.reshape(d_ssm, n)


def _t5_bucket_np(rel):
    half = N_BUCKETS // 2
    max_exact = half // 2
    ret = np.where(rel > 0, half, 0)
    n = np.abs(rel)
    n_f = np.maximum(n, 1).astype(np.float32)
    large = max_exact + (np.log(n_f / np.float32(max_exact)) / np.float32(math.log(MAX_DIST / max_exact))
                         * (half - max_exact)).astype(np.int32)
    large = np.minimum(large, half - 1)
    return ret + np.where(n < max_exact, n, large)


def _near_bias_tables(rel_bias, tq, tk):
    kk = np.arange(tk)[:, None]
    qq = np.arange(tq)[None, :]
    b_diag = _t5_bucket_np(kk - qq)
    b_prev = _t5_bucket_np(kk - qq - tk)
    far = int(_t5_bucket_np(np.array(-(tk + 1))))
    assert far == int(_t5_bucket_np(np.array(-(1 << 24)))), "far keys must share one bucket"
    tab = rel_bias.astype(F32)
    one_hot = jax.nn.one_hot(np.stack([b_diag, b_prev]), N_BUCKETS, dtype=F32)
    near = jnp.einsum("wkqb,bh->hwkq", one_hot, tab - tab[far][None, :], precision=lax.Precision.HIGHEST)
    return near * LOG2E


def _sortable(x):
    b = pltpu.bitcast(x, I32)
    return b ^ ((b >> 31) & 0x7FFFFFFF)


def _pair_lanes(h):
    return slice((h // 2) * 2 * HEAD_DIM, (h // 2 + 1) * 2 * HEAD_DIM)


def _dsa_kernel(qit_ref, wi_ref, q_ref, bound_ref, kidx_ref, k_ref, vt_ref, near_ref, o_ref,
                key_scr, k16_scr, mb_scr, qh_scr, m_scr, p_scr, acc_scr, out_scr, *, tq, tk, n_sel):
    i = pl.program_id(1)
    n_tiles = i + 1
    tile_rows = lambda t: pl.ds(pl.multiple_of(t * tk, tk), tk)

    def score_tile(t):
        kid = kidx_ref[tile_rows(t), :]
        acc = jnp.zeros((tk, tq), F32)
        for h in range(IDX_HEADS):
            s = jnp.dot(kid, qit_ref[h * IDX_DIM:(h + 1) * IDX_DIM, :], preferred_element_type=F32)
            acc = acc + wi_ref[h:h + 1, :] * jnp.maximum(s, 0.0)
        return acc

    def store_keys(t, score):
        key = _sortable(score)
        key_scr[tile_rows(t), :] = key
        k16_scr[tile_rows(t), :] = (key >> 16).astype(I16)

    def for_tiles(n, tile_fn):
        def pair_body(j, carry):
            tile_fn(2 * j)
            tile_fn(2 * j + 1)
            return carry

        lax.fori_loop(0, n // 2, pair_body, 0)

        @pl.when(n % 2 == 1)
        def _():
            tile_fn(n - 1)

    for_tiles(i, lambda t: store_keys(t, score_tile(t) + 0.0))
    kpos = lax.broadcasted_iota(I32, (tk, tq), 0)
    qpos = lax.broadcasted_iota(I32, (tk, tq), 1)
    admissible = (kpos // CHUNK) <= (qpos // CHUNK)
    store_keys(i, jnp.where(admissible, score_tile(i) + 0.0, -jnp.inf))
    for extra in range(COUNT_TILES - 1):
        k16_scr[tile_rows(n_tiles + extra), :] = jnp.full((tk, tq), -HALF16, I16)

    def count16(cand):
        c16 = cand.astype(I16)

        def group_body(j, c):
            rows = pl.ds(pl.multiple_of(j * (COUNT_TILES * tk), COUNT_TILES * tk), COUNT_TILES * tk)
            hit = jnp.where(k16_scr[rows, :] >= c16, jnp.int16(1), jnp.int16(0))
            parts = [hit[r * 16:(r + 1) * 16] for r in range(COUNT_TILES * tk // 16)]
            while len(parts) > 1:
                parts = [a + b for a, b in zip(parts[0::2], parts[1::2])]
            return c + parts[0]

        n_groups = (n_tiles + COUNT_TILES - 1) // COUNT_TILES
        part = lax.fori_loop(0, n_groups, group_body, jnp.zeros((16, tq), I16))
        return jnp.sum(part.astype(I32), axis=0, keepdims=True)

    def radix_bits(digit, c_low, bits):
        for b in bits:
            cand = digit | (1 << b)
            cnt = count16(cand - HALF16)
            accept = cnt >= n_sel
            digit = jnp.where(accept, cand, digit)
            c_low = jnp.where(accept, cnt, c_low)
        return digit, c_low

    def radix_bits_if_open(digit, c_low, bits):
        still_open = jnp.max(jnp.where(take_all, 0, c_low - n_sel)) > 0
        return lax.cond(still_open, lambda d, c: radix_bits(d, c, bits), lambda d, c: (d, c), digit, c_low)

    def radix16(c_low, early_exit):
        digit = jnp.zeros((1, tq), I32)
        if not early_exit:
            return radix_bits(digit, c_low, range(15, -1, -1))
        digit, c_low = radix_bits(digit, c_low, range(15, 7, -1))
        digit, c_low = radix_bits_if_open(digit, c_low, range(7, 3, -1))
        return radix_bits_if_open(digit, c_low, range(3, -1, -1))

    q_in_blk = lax.broadcasted_iota(I32, (1, tq), 1)
    n_adm = (i * tq + q_in_blk) // CHUNK * CHUNK + CHUNK
    take_all = n_adm <= n_sel

    hi_digit, c_low = radix16(jnp.full((1, tq), 1, I32) * (n_tiles * tk), early_exit=False)
    hi16 = hi_digit - HALF16

    def low_digit_tile(t):
        key = key_scr[tile_rows(t), :]
        hi = key >> 16
        low = (key & 0xFFFF) - HALF16
        k16_scr[tile_rows(t), :] = jnp.where(hi > hi16, HALF16 - 1, jnp.where(hi < hi16, -HALF16, low)).astype(I16)

    for_tiles(n_tiles, low_digit_tile)
    lo_digit, c_low = radix16(c_low, early_exit=True)
    thr = jnp.where(take_all, KEY_NEG_INF + 1, jnp.left_shift(hi16, 16) | lo_digit)
    excess = jnp.where(take_all, 0, c_low - n_sel)
    any_excess = jnp.max(excess) > 0

    neg_shift = -bound_ref[0:1, :]
    @pl.when(jnp.logical_not(any_excess))
    def _():
        def mask_tile(t):
            mb_scr[tile_rows(t), :] = jnp.where(key_scr[tile_rows(t), :] >= thr, neg_shift, MASK_NEG)

        for_tiles(n_tiles, mask_tile)

    @pl.when(any_excess)
    def _():
        def gt_body(t, c):
            hit = jnp.where(key_scr[tile_rows(t), :] > thr, 1, 0).astype(I32)
            return c + hit.reshape(tk // 8, 8, tq).sum(axis=0)

        n_gt = jnp.sum(lax.fori_loop(0, n_tiles, gt_body, jnp.zeros((8, tq), I32)), axis=0, keepdims=True)
        need = jnp.where(take_all, 2 * tk * n_tiles, n_sel - n_gt).astype(F32)
        r_i = lax.broadcasted_iota(I32, (tk, tk), 0)
        c_i = lax.broadcasted_iota(I32, (tk, tk), 1)
        strict_lower = jnp.where(c_i < r_i, 1.0, 0.0).astype(BF16)

        def mask_body(t, ties_before):
            kt = key_scr[tile_rows(t), :]
            eq = kt == thr
            eq_f = jnp.where(eq, 1.0, 0.0)
            rank = jnp.dot(strict_lower, eq_f.astype(BF16), preferred_element_type=F32) + ties_before
            sel = jnp.logical_or(kt > thr, jnp.logical_and(eq, rank < need))
            mb_scr[tile_rows(t), :] = jnp.where(sel, neg_shift, MASK_NEG)
            return ties_before + jnp.sum(eq_f.reshape(tk // 8, 8, tq).sum(axis=0), axis=0, keepdims=True)

        lax.fori_loop(0, n_tiles, mask_body, jnp.zeros((1, tq), F32))

    lane = lax.broadcasted_iota(I32, (tq, 2 * HEAD_DIM), 1)
    for h in range(N_HEADS):
        q_pair = q_ref[:, _pair_lanes(h)]
        own = (lane < HEAD_DIM) if h % 2 == 0 else (lane >= HEAD_DIM)
        qh_scr[h] = jnp.where(own, q_pair, jnp.zeros_like(q_pair))

    def masked_logits(t, h, near_which):
        lg = lax.dot_general(k_ref[tile_rows(t), _pair_lanes(h)], qh_scr[h], NT_DIMS,
                             preferred_element_type=F32) + mb_scr[tile_rows(t), :]
        if near_which is not None:
            lg = lg + near_ref[h, near_which]
        return lg

    def over_tiles(tile_fn):
        for_tiles(jnp.maximum(i - 1, 0), lambda t: tile_fn(t, None))

        @pl.when(i >= 1)
        def _():
            tile_fn(i - 1, 1)

        tile_fn(i, 0)

    def pv_tile_against(shift_row):
        def tile_fn(t, near_which):
            for h in range(N_HEADS):
                lg = masked_logits(t, h, near_which)
                if shift_row is not None:
                    lg = lg - shift_row(h)
                p_scr[h] = jnp.exp2(lg).astype(BF16)
            for h in range(N_HEADS):
                vrows = slice(h * V_ROWS, (h + 1) * V_ROWS)
                acc_scr[vrows, :] += jnp.dot(vt_ref[vrows, tile_rows(t)], p_scr[h], preferred_element_type=F32)
        return tile_fn

    den_row = lambda h: acc_scr[h * V_ROWS + HEAD_DIM:h * V_ROWS + HEAD_DIM + 1, :]

    acc_scr[...] = jnp.zeros(acc_scr.shape, F32)
    over_tiles(pv_tile_against(None))
    den_min = den_row(0)
    for h in range(1, N_HEADS):
        den_min = jnp.minimum(den_min, den_row(h))
    bound_too_loose = jnp.logical_not(jnp.min(den_min) >= MIN_SAFE_DEN)

    @pl.when(bound_too_loose)
    def _():
        m_scr[...] = jnp.full(m_scr.shape, MASK_NEG, F32)

        def max_tile(t, near_which):
            for h in range(N_HEADS):
                part = masked_logits(t, h, near_which).reshape(tk // 8, 8, tq).max(axis=0)
                m_scr[h * 8:(h + 1) * 8, :] = jnp.maximum(m_scr[h * 8:(h + 1) * 8, :], part)

        over_tiles(max_tile)
        for h in range(N_HEADS):
            m_scr[h * 8:(h + 1) * 8, :] = jnp.broadcast_to(
                jnp.max(m_scr[h * 8:(h + 1) * 8, :], axis=0, keepdims=True), (8, tq))
        acc_scr[...] = jnp.zeros(acc_scr.shape, F32)
        over_tiles(pv_tile_against(lambda h: m_scr[h * 8:h * 8 + 1, :]))

    for h in range(N_HEADS):
        out_scr[h * HEAD_DIM:(h + 1) * HEAD_DIM, :] = acc_scr[h * V_ROWS:h * V_ROWS + HEAD_DIM, :] / den_row(h)
    o_ref[...] = out_scr[...].T


def _logit_bound(stats, near, bsz, tq):
    sq = jnp.max(stats.reshape(bsz, -1, *stats.shape[1:]), axis=1)[:, :2, :N_HEADS]
    norms = jnp.sqrt(sq)
    near_max = jnp.maximum(jnp.max(near, axis=(1, 2, 3)), 0.0)
    bound = jnp.max(norms[:, 0] * norms[:, 1] * BOUND_SLACK + near_max + 1.0, axis=1)
    return jnp.broadcast_to(bound[:, None, None], (bsz, 8, tq))


def _dsa(q, k, kidx, vt, qit, kwt, stats, near, bsz, seq):
    d_attn = q.shape[1]
    tq, tk = DSA_TQ, DSA_TK
    assert tq == tk and seq % tq == 0 and tq % CHUNK == 0 and IDX_DIM % IDX_HEADS == 0
    n_sel = min(TOPK_MAX, seq // 4)
    nq = seq // tq
    bound = _logit_bound(stats, near, bsz, tq)
    once = dict(pipeline_mode=pl.Buffered(1))
    return pl.pallas_call(
        functools.partial(_dsa_kernel, tq=tq, tk=tk, n_sel=n_sel),
        out_shape=jax.ShapeDtypeStruct((bsz * seq, d_attn), F32),
        grid=(bsz, nq),
        in_specs=[pl.BlockSpec((IDX_HEADS * IDX_DIM, tq), lambda b, i: (0, b * nq + i)),
                  pl.BlockSpec((IDX_HEADS, tq), lambda b, i: (IDX_DIM // IDX_HEADS, b * nq + i)),
                  pl.BlockSpec((tq, d_attn), lambda b, i: (b * nq + i, 0)),
                  pl.BlockSpec((None, 8, tq), lambda b, i: (b, 0, 0)),
                  pl.BlockSpec((seq, IDX_DIM), lambda b, i: (b, 0), **once),
                  pl.BlockSpec((seq, d_attn), lambda b, i: (b, 0), **once),
                  pl.BlockSpec((N_HEADS * V_ROWS, seq), lambda b, i: (0, b), **once),
                  _resident(near.shape)],
        out_specs=pl.BlockSpec((tq, d_attn), lambda b, i: (b * nq + i, 0)),
        scratch_shapes=[pltpu.VMEM((seq, tq), I32), pltpu.VMEM((seq + (COUNT_TILES - 1) * tk, tq), I16), pltpu.VMEM((seq, tq), F32),
                        pltpu.VMEM((N_HEADS, tq, 2 * HEAD_DIM), BF16), pltpu.VMEM((N_HEADS * 8, tq), F32),
                        pltpu.VMEM((N_HEADS, tk, tq), BF16),
                        pltpu.VMEM((N_HEADS * V_ROWS, tq), F32), pltpu.VMEM((d_attn, tq), F32)],
        compiler_params=pltpu.CompilerParams(dimension_semantics=("parallel", "arbitrary"),
                                             vmem_limit_bytes=V7X_VMEM_LIMIT_BYTES),
        name="dsa",
    )(qit, kwt, q, bound, kidx, k, vt, near)


def _merge_ln_kernel(x_ref, ys_ref, ya_ref, gs_ref, ga_ref, wglu_ref, wb0_ref, wb1_ref, wout_ref,
                     g_ref, b_ref, o_ref):
    y = jax.nn.gelu(ys_ref[...].T)
    glu = y * jax.nn.sigmoid(jnp.dot(y.astype(BF16), wglu_ref[...], preferred_element_type=F32))
    p_ssm = jnp.dot(glu.astype(BF16), wb0_ref[...], preferred_element_type=F32)
    p_att = jnp.dot(ya_ref[...].astype(BF16), wb1_ref[...], preferred_element_type=F32)
    merged = gs_ref[...].astype(F32) * p_ssm + ga_ref[...].astype(F32) * p_att
    mix = jnp.dot(merged.astype(BF16), wout_ref[...], preferred_element_type=F32)
    o_ref[...] = _layer_norm(ALPHA * x_ref[...] + mix, g_ref[...], b_ref[...])


def _merge_ln(x2, y_ssm_t, y_attn, gs, ga, w_glu, wb0, wb1, w_out, g, b):
    n, d = x2.shape
    tm = min(MERGE_TM, n)
    assert n % tm == 0
    tile = lambda a: pl.BlockSpec((tm, a.shape[1]), lambda i: (i, 0))
    return pl.pallas_call(
        _merge_ln_kernel,
        out_shape=jax.ShapeDtypeStruct((n, d), F32),
        grid=(n // tm,),
        in_specs=[tile(x2), pl.BlockSpec((y_ssm_t.shape[0], tm), lambda i: (0, i)), tile(y_attn), tile(gs), tile(ga),
                  _resident(w_glu.shape), _resident(wb0.shape), _resident(wb1.shape), _resident(w_out.shape),
                  _resident((1, d)), _resident((1, d))],
        out_specs=pl.BlockSpec((tm, d), lambda i: (i, 0)),
        compiler_params=pltpu.CompilerParams(dimension_semantics=("parallel",),
                                             vmem_limit_bytes=V7X_VMEM_LIMIT_BYTES),
        name="merge_ln",
    )(x2, y_ssm_t, y_attn, gs, ga, w_glu, wb0, wb1, w_out, g.reshape(1, d), b.reshape(1, d))


def _pack_cols(parts):
    offs, pos = {}, 0
    for name, w in parts.items():
        offs[name] = (pos, pos + w.shape[1])
        pos += w.shape[1]
    return jnp.concatenate(list(parts.values()), axis=1), offs


def _split_w_in(w_in, d_model, d_ssm, d_attn):
    sizes = (d_ssm, d_attn, d_attn, d_attn, IDX_HEADS * IDX_DIM, IDX_DIM, IDX_HEADS, d_model, d_model)
    assert w_in.shape[1] == sum(sizes)
    edges = np.concatenate([[0], np.cumsum(sizes)])
    u, q, k, v, qi, ki, wi, gs, ga = [w_in[:, int(edges[j]):int(edges[j + 1])] for j in range(len(sizes))]
    zeros = lambda c: jnp.zeros((w_in.shape[0], c), w_in.dtype)
    w_nat, offs = _pack_cols(dict(q=q, k=k, kidx=jnp.concatenate([ki, zeros(LANES - IDX_DIM)], axis=1),
                                  gs=gs, ga=ga))
    kw = jnp.concatenate([ki, wi, zeros(LANES - IDX_DIM - IDX_HEADS)], axis=1)
    w_t, offs_t = _pack_cols(dict(u=u, v=v, qi=qi, kw=kw))
    return w_nat.astype(BF16), offs, w_t.T.astype(BF16), offs_t


def kernel(x, ffn1_w_up, ffn1_w_down, ln1_g, ln1_b, w_in, ssm_lam_re, ssm_lam_im, ssm_log_dt, ssm_b_re, ssm_b_im, ssm_c_re, ssm_c_im, ssm_d, ssm_w_glu, w_branch, w_out, ln2_g, ln2_b, ffn2_w_up, ffn2_w_down, ln3_g, ln3_b, rel_bias):
    bsz, seq, d_model = x.shape
    d_ssm = ssm_w_glu.shape[1]
    d_attn = w_branch.shape[2]
    x2 = x.reshape(bsz * seq, d_model)
    near = _near_bias_tables(rel_bias, DSA_TQ, DSA_TK)
    for l in range(ffn1_w_up.shape[0]):
        x2 = _ffn_ln(x2, ffn1_w_up[l].astype(BF16), ffn1_w_down[l].astype(BF16), ln1_g[l], ln1_b[l])
        q, k, kidx, gs, ga, stats, ut, vt, qit, kwt = _in_proj(x2, *_split_w_in(w_in[l], d_model, d_ssm, d_attn), seq)
        tables = _ssm_tables(ssm_lam_re[l], ssm_lam_im[l], ssm_log_dt[l], ssm_b_re[l], ssm_b_im[l],
                             ssm_c_re[l], ssm_c_im[l], ssm_d[l], SSM_T, seq // SSM_T)
        y_ssm = _ssm(ut, tables, seq)
        y_attn = _dsa(q, k, kidx, vt, qit, kwt, stats, near, bsz, seq)
        x2 = _merge_ln(x2, y_ssm, y_attn, gs, ga, ssm_w_glu[l].astype(BF16), w_branch[l, 0].astype(BF16),
                       w_branch[l, 1].astype(BF16), w_out[l].astype(BF16), ln2_g[l], ln2_b[l])
        x2 = _ffn_ln(x2, ffn2_w_up[l].astype(BF16), ffn2_w_down[l].astype(BF16), ln3_g[l], ln3_b[l])
    return x2.reshape(bsz, seq, d_model)
```

```python
import functools
import math

import numpy as np
import jax
import jax.numpy as jnp
from jax import lax
from jax.experimental import pallas as pl
from jax.experimental.pallas import tpu as pltpu

F32 = jnp.float32
BF16 = jnp.bfloat16
I32 = jnp.int32
I16 = jnp.int16

DEPTH = 2
CHUNK = 64
SSM_GROUP = 16
STATE = 64
N_HEADS = 8
HEAD_DIM = 64
IDX_HEADS = 8
IDX_DIM = 32
TOPK_MAX = 256
ATTN_SCALE = HEAD_DIM ** -0.5
LOG2E = math.log2(math.e)
N_BUCKETS = 32
MAX_DIST = 128
ALPHA = (2 * DEPTH) ** 0.25
LN_EPS = 1e-5

V7X_VMEM_LIMIT_BYTES = 58 * 1024 * 1024
LANES = 128
FFN_TM = 512
FFN_FC = 256
PROJ_TM = 1024
MERGE_TM = 1024
SSM_T = 64
DSA_TQ = 256
DSA_TK = 256
COUNT_TILES = 2
MASK_NEG = -1e30
W_ROWS = 16
V_ROWS = HEAD_DIM + 16
KEY_NEG_INF = int(np.int32(np.uint32(0x807FFFFF)))
HALF16 = 1 << 15
MIN_SAFE_DEN = 2.0 ** -80
BOUND_SLACK = 1.02


def _resident(shape):
    nd = len(shape)
    return pl.BlockSpec(shape, lambda *_: (0,) * nd, pipeline_mode=pl.Buffered(1))


def _layer_norm(y, g, b):
    mu = jnp.mean(y, axis=-1, keepdims=True)
    yc = y - mu
    var = jnp.mean(yc * yc, axis=-1, keepdims=True)
    return yc * lax.rsqrt(var + LN_EPS) * g + b


NT_DIMS = (((1,), (1,)), ((), ()))


def _ffn_ln_kernel(x_ref, wup_ref, wdn_ref, g_ref, b_ref, o_ref, *, d_ff, fc):
    x = x_ref[...]
    xb = x.astype(BF16)
    acc = jnp.zeros(x.shape, F32)
    for j in range(d_ff // fc):
        gate = jnp.dot(xb, wup_ref[:, j * fc:(j + 1) * fc], preferred_element_type=F32)
        up = jnp.dot(xb, wup_ref[:, d_ff + j * fc:d_ff + (j + 1) * fc], preferred_element_type=F32)
        a = (gate * jax.nn.sigmoid(gate)) * up
        acc = acc + jnp.dot(a.astype(BF16), wdn_ref[j * fc:(j + 1) * fc, :], preferred_element_type=F32)
    o_ref[...] = _layer_norm(ALPHA * x + 0.5 * acc, g_ref[...], b_ref[...])


def _ffn_ln(x2, w_up, w_down, g, b):
    n, d = x2.shape
    d_ff = w_down.shape[0]
    tm = min(FFN_TM, n)
    assert n % tm == 0 and d_ff % FFN_FC == 0
    return pl.pallas_call(
        functools.partial(_ffn_ln_kernel, d_ff=d_ff, fc=FFN_FC),
        out_shape=jax.ShapeDtypeStruct((n, d), F32),
        grid=(n // tm,),
        in_specs=[pl.BlockSpec((tm, d), lambda i: (i, 0)),
                  _resident(w_up.shape), _resident(w_down.shape),
                  _resident((1, d)), _resident((1, d))],
        out_specs=pl.BlockSpec((tm, d), lambda i: (i, 0)),
        compiler_params=pltpu.CompilerParams(dimension_semantics=("parallel",),
                                             vmem_limit_bytes=V7X_VMEM_LIMIT_BYTES),
        name="ffn_ln",
    )(x2, w_up, w_down, g.reshape(1, d), b.reshape(1, d))


def _in_proj_kernel(x_ref, w_ref, wt_ref, head_ind_ref, q_ref, k_ref, kidx_ref, gs_ref, ga_ref, stats_ref,
                    ut_ref, vt_ref, qit_ref, kwt_ref, *, offs, offs_t):
    xb = x_ref[...].astype(BF16)

    def proj(name):
        lo, hi = offs[name]
        return jnp.dot(xb, w_ref[:, lo:hi], preferred_element_type=F32)

    def proj_t(name):
        lo, hi = offs_t[name]
        return lax.dot_general(wt_ref[lo:hi, :], xb, NT_DIMS, preferred_element_type=F32)

    ut_ref[...] = proj_t("u")
    q_b = (proj("q") * (ATTN_SCALE * LOG2E)).astype(BF16)
    k_b = proj("k").astype(BF16)
    q_ref[...] = q_b
    k_ref[...] = k_b

    def head_sq_max(v_b):
        per_head = jnp.dot(jnp.square(v_b.astype(F32)).astype(BF16), head_ind_ref[...], preferred_element_type=F32)
        return jnp.max(per_head, axis=0, keepdims=True)

    stats_ref[0] = jnp.concatenate([head_sq_max(q_b), head_sq_max(k_b), jnp.zeros((6, LANES), F32)], axis=0)
    kidx_ref[...] = proj("kidx")[:, :IDX_DIM].astype(BF16)
    gs_ref[...] = jax.nn.sigmoid(proj("gs")).astype(BF16)
    ga_ref[...] = jax.nn.sigmoid(proj("ga")).astype(BF16)
    qit_ref[...] = proj_t("qi").astype(BF16)
    kwt_ref[...] = proj_t("kw")
    v_t = proj_t("v").astype(BF16)
    tm = v_t.shape[1]
    pad_rows = lax.broadcasted_iota(I32, (V_ROWS - HEAD_DIM, tm), 0)
    pad = jnp.where(pad_rows == 0, 1.0, 0.0).astype(BF16)
    for h in range(N_HEADS):
        vt_ref[h * V_ROWS:h * V_ROWS + HEAD_DIM, :] = v_t[h * HEAD_DIM:(h + 1) * HEAD_DIM, :]
        vt_ref[h * V_ROWS + HEAD_DIM:(h + 1) * V_ROWS, :] = pad


def _in_proj(x2, w_nat, offs, w_t, offs_t, seq):
    n, d = x2.shape
    tm = min(PROJ_TM, seq)
    assert seq % tm == 0 and n % seq == 0
    d_attn = offs["q"][1] - offs["q"][0]
    head_ind = (np.arange(d_attn)[:, None] // HEAD_DIM == np.arange(LANES)[None, :]).astype(np.float32)
    width = lambda name: offs[name][1] - offs[name][0]
    rows_t = lambda name: offs_t[name][1] - offs_t[name][0]
    tok = lambda w, dt: (jax.ShapeDtypeStruct((n, w), dt), pl.BlockSpec((tm, w), lambda i: (i, 0)))
    chan = lambda r, dt: (jax.ShapeDtypeStruct((r, n), dt), pl.BlockSpec((r, tm), lambda i: (0, i)))
    outs = [tok(width("q"), BF16), tok(width("k"), BF16), tok(IDX_DIM, BF16),
            tok(width("gs"), BF16), tok(width("ga"), BF16),
            (jax.ShapeDtypeStruct((n // tm, 8, LANES), F32), pl.BlockSpec((1, 8, LANES), lambda i: (i, 0, 0))),
            chan(rows_t("u"), F32), chan(N_HEADS * V_ROWS, BF16), chan(rows_t("qi"), BF16), chan(rows_t("kw"), F32)]
    return pl.pallas_call(
        functools.partial(_in_proj_kernel, offs=offs, offs_t=offs_t),
        out_shape=[o[0] for o in outs],
        grid=(n // tm,),
        in_specs=[pl.BlockSpec((tm, d), lambda i: (i, 0)), _resident(w_nat.shape), _resident(w_t.shape),
                  _resident(head_ind.shape)],
        out_specs=[o[1] for o in outs],
        compiler_params=pltpu.CompilerParams(dimension_semantics=("parallel",),
                                             vmem_limit_bytes=V7X_VMEM_LIMIT_BYTES),
        name="in_proj",
    )(x2, w_nat, w_t, jnp.asarray(head_ind, BF16))


def _ssm_kernel(u_ref, m_ref, w_ref, v_ref, apow_ref, d_ref, y_ref, u_scr, y_scr, m_scr, *, rows_per_seq):
    t_len = u_ref.shape[2] // 2
    n_rows = u_ref.shape[1]
    even, odd = slice(0, n_rows), slice(n_rows, 2 * n_rows)
    for h in range(SSM_GROUP):
        cols = slice(h * t_len, (h + 1) * t_len)
        u_scr[even, cols] = u_ref[h, :, :t_len]
        u_scr[odd, cols] = u_ref[h, :, t_len:]
    u = u_scr[...]
    ub = u.astype(BF16)
    s = jnp.dot(ub, w_ref[0], preferred_element_type=F32)
    s_even, s_odd = s[even], s[odd]
    row = lax.broadcasted_iota(I32, (n_rows, 2 * STATE), 0) % rows_per_seq

    def times_a(val, k):
        a_same = apow_ref[0, 2 * k:2 * k + 1, :]
        a_cross = apow_ref[0, 2 * k + 1:2 * k + 2, :]
        return val * a_same + pltpu.roll(val, STATE, 1) * a_cross

    def shifted(val, by):
        return jnp.where(row >= by, pltpu.roll(val, by, 0), 0.0)

    x = times_a(s_even, 0) + s_odd
    for k in range(1, rows_per_seq.bit_length()):
        x = x + times_a(shifted(x, 1 << (k - 1)), k)
    h_even = shifted(x, 1)
    h_odd = times_a(h_even, 0) + s_even
    h_in = jnp.concatenate([h_even, h_odd], axis=0)
    for k in range(SSM_GROUP):
        for j in range(SSM_GROUP // 2):
            m_scr[k * t_len:(k + 1) * t_len, j * LANES:(j + 1) * LANES] = m_ref[0, k, j]
    y = jnp.dot(ub, m_scr[...], preferred_element_type=F32)
    y = y + jnp.dot(h_in.astype(BF16), v_ref[0], preferred_element_type=F32)
    y_scr[...] = y + d_ref[0] * u
    for h in range(SSM_GROUP):
        cols = slice(h * t_len, (h + 1) * t_len)
        y_ref[h, :, :t_len] = y_scr[even, cols]
        y_ref[h, :, t_len:] = y_scr[odd, cols]


def _ssm_tables(lam_re, lam_im, log_dt, b_re, b_im, c_re, c_im, d_skip, t_len, chunks_per_seq):
    g = lam_re.shape[0]
    lam = lax.complex(jnp.minimum(lam_re.astype(F32), -1e-4), lam_im.astype(F32))
    dt = jnp.exp(log_dt.astype(F32))[:, None]
    a_bar = jnp.exp(lam * dt)
    b_bar = ((a_bar - 1.0) / lam)[:, :, None] * lax.complex(b_re.astype(F32), b_im.astype(F32))
    c_mat = lax.complex(c_re.astype(F32), c_im.astype(F32))
    taus = jnp.arange(t_len + 1, dtype=F32)
    apow = jnp.exp((lam * dt)[:, None, :] * taus[None, :, None])
    hp = lax.Precision.HIGHEST
    kern = jnp.einsum("ghp,gtp,gpk->gkht", c_mat, apow[:, :t_len], b_bar, precision=hp).real
    pos = np.arange(t_len)
    shift = (pos[None, None, :] - pos[None, :, None]) == pos[:, None, None]
    pair_shift = np.einsum("ab,lst->alsbt", np.eye(2, dtype=bool), shift).reshape(2 * t_len, t_len, 2 * t_len)
    kern_pairs = kern.astype(BF16).reshape(g, SSM_GROUP, SSM_GROUP // 2, 2 * t_len)
    m_tab = jnp.einsum("gkjc,csd->gkjsd", kern_pairs, jnp.asarray(pair_shift, BF16), preferred_element_type=BF16)
    w_c = b_bar.transpose(0, 2, 1)[:, :, None, :] * apow[:, :t_len][:, None, ::-1, :]
    w_c = w_c.reshape(g, SSM_GROUP * t_len, STATE)
    w_tab = jnp.concatenate([w_c.real, w_c.imag], axis=-1)
    v_c = c_mat.transpose(0, 2, 1)[:, :, :, None] * apow[:, 1:t_len + 1].transpose(0, 2, 1)[:, :, None, :]
    v_c = v_c.reshape(g, STATE, SSM_GROUP * t_len)
    v_tab = jnp.concatenate([v_c.real, -v_c.imag], axis=1)
    n_steps = chunks_per_seq.bit_length() - 1
    steps = (t_len * (1 << jnp.arange(n_steps))).astype(F32)
    a_step = jnp.exp((lam * dt)[:, None, :] * steps[None, :, None])
    a_rows = jnp.stack([jnp.concatenate([a_step.real, a_step.real], axis=-1),
                        jnp.concatenate([-a_step.imag, a_step.imag], axis=-1)], axis=2)
    a_rows = a_rows.reshape(g, 2 * n_steps, 2 * STATE)
    d_tab = jnp.repeat(d_skip.astype(F32).reshape(g, 1, SSM_GROUP), t_len, axis=2)
    return m_tab, w_tab.astype(BF16), v_tab.astype(BF16), a_rows, d_tab


def _ssm(u_t, tables, seq):
    m_tab, w_tab, v_tab, a_rows, d_tab = tables
    g = m_tab.shape[0]
    d_ssm, n = u_t.shape
    t_len = SSM_T
    rows_per_seq = seq // (2 * t_len)
    assert 2 * t_len == LANES and seq % LANES == 0 and rows_per_seq & (rows_per_seq - 1) == 0
    assert d_ssm == g * SSM_GROUP
    rows, cols = n // LANES, t_len * SSM_GROUP
    per_group = lambda shape: pl.BlockSpec((1,) + shape, lambda i: (i, 0, 0))
    channels = pl.BlockSpec((SSM_GROUP, rows, LANES), lambda i: (i, 0, 0))
    y_t = pl.pallas_call(
        functools.partial(_ssm_kernel, rows_per_seq=rows_per_seq),
        out_shape=jax.ShapeDtypeStruct((d_ssm, rows, LANES), F32),
        grid=(g,),
        in_specs=[channels, pl.BlockSpec((1,) + m_tab.shape[1:], lambda i: (i, 0, 0, 0, 0)),
                  per_group((cols, 2 * STATE)), per_group((2 * STATE, cols)),
                  per_group(a_rows.shape[1:]), per_group((1, cols))],
        out_specs=channels,
        scratch_shapes=[pltpu.VMEM((2 * rows, cols), F32), pltpu.VMEM((2 * rows, cols), F32),
                        pltpu.VMEM((cols, cols), BF16)],
        compiler_params=pltpu.CompilerParams(dimension_semantics=("parallel",),
                                             vmem_limit_bytes=V7X_VMEM_LIMIT_BYTES),
        name="ssm",
    )(u_t.reshape(d_ssm, rows, LANES), m_tab, w_tab, v_tab, a_rows, d_tab)
    return y_t.reshape(d_ssm, n)


def _t5_bucket_np(rel):
    half = N_BUCKETS // 2
    max_exact = half // 2
    ret = np.where(rel > 0, half, 0)
    n = np.abs(rel)
    n_f = np.maximum(n, 1).astype(np.float32)
    large = max_exact + (np.log(n_f / np.float32(max_exact)) / np.float32(math.log(MAX_DIST / max_exact))
                         * (half - max_exact)).astype(np.int32)
    large = np.minimum(large, half - 1)
    return ret + np.where(n < max_exact, n, large)


def _near_bias_tables(rel_bias, tq, tk):
    kk = np.arange(tk)[:, None]
    qq = np.arange(tq)[None, :]
    b_diag = _t5_bucket_np(kk - qq)
    b_prev = _t5_bucket_np(kk - qq - tk)
    far = int(_t5_bucket_np(np.array(-(tk + 1))))
    assert far == int(_t5_bucket_np(np.array(-(1 << 24)))), "far keys must share one bucket"
    tab = rel_bias.astype(F32)
    one_hot = jax.nn.one_hot(np.stack([b_diag, b_prev]), N_BUCKETS, dtype=F32)
    near = jnp.einsum("wkqb,bh->hwkq", one_hot, tab - tab[far][None, :], precision=lax.Precision.HIGHEST)
    return near * LOG2E


def _sortable(x):
    b = pltpu.bitcast(x, I32)
    return b ^ ((b >> 31) & 0x7FFFFFFF)


def _pair_lanes(h):
    return slice((h // 2) * 2 * HEAD_DIM, (h // 2 + 1) * 2 * HEAD_DIM)


def _dsa_kernel(qit_ref, wi_ref, q_ref, bound_ref, kidx_ref, k_ref, vt_ref, near_ref, o_ref,
                key_scr, k16_scr, mb_scr, qh_scr, m_scr, p_scr, acc_scr, out_scr, *, tq, tk, n_sel):
    i = pl.program_id(1)
    n_tiles = i + 1
    tile_rows = lambda t: pl.ds(pl.multiple_of(t * tk, tk), tk)

    def score_tile(t):
        kid = kidx_ref[tile_rows(t), :]
        acc = jnp.zeros((tk, tq), F32)
        for h in range(IDX_HEADS):
            s = jnp.dot(kid, qit_ref[h * IDX_DIM:(h + 1) * IDX_DIM, :], preferred_element_type=F32)
            acc = acc + wi_ref[h:h + 1, :] * jnp.maximum(s, 0.0)
        return acc

    def store_keys(t, score):
        key = _sortable(score)
        key_scr[tile_rows(t), :] = key
        k16_scr[tile_rows(t), :] = (key >> 16).astype(I16)

    def for_tiles(n, tile_fn):
        def pair_body(j, carry):
            tile_fn(2 * j)
            tile_fn(2 * j + 1)
            return carry

        lax.fori_loop(0, n // 2, pair_body, 0)

        @pl.when(n % 2 == 1)
        def _():
            tile_fn(n - 1)

    for_tiles(i, lambda t: store_keys(t, score_tile(t) + 0.0))
    kpos = lax.broadcasted_iota(I32, (tk, tq), 0)
    qpos = lax.broadcasted_iota(I32, (tk, tq), 1)
    admissible = (kpos // CHUNK) <= (qpos // CHUNK)
    store_keys(i, jnp.where(admissible, score_tile(i) + 0.0, -jnp.inf))
    for extra in range(COUNT_TILES - 1):
        k16_scr[tile_rows(n_tiles + extra), :] = jnp.full((tk, tq), -HALF16, I16)

    def count16(cand):
        c16 = cand.astype(I16)

        def group_body(j, c):
            rows = pl.ds(pl.multiple_of(j * (COUNT_TILES * tk), COUNT_TILES * tk), COUNT_TILES * tk)
            hit = jnp.where(k16_scr[rows, :] >= c16, jnp.int16(1), jnp.int16(0))
            parts = [hit[r * 16:(r + 1) * 16] for r in range(COUNT_TILES * tk // 16)]
            while len(parts) > 1:
                parts = [a + b for a, b in zip(parts[0::2], parts[1::2])]
            return c + parts[0]

        n_groups = (n_tiles + COUNT_TILES - 1) // COUNT_TILES
        part = lax.fori_loop(0, n_groups, group_body, jnp.zeros((16, tq), I16))
        return jnp.sum(part.astype(I32), axis=0, keepdims=True)

    def radix_bits(digit, c_low, bits):
        for b in bits:
            cand = digit | (1 << b)
            cnt = count16(cand - HALF16)
            accept = cnt >= n_sel
            digit = jnp.where(accept, cand, digit)
            c_low = jnp.where(accept, cnt, c_low)
        return digit, c_low

    def radix_bits_if_open(digit, c_low, bits):
        still_open = jnp.max(jnp.where(take_all, 0, c_low - n_sel)) > 0
        return lax.cond(still_open, lambda d, c: radix_bits(d, c, bits), lambda d, c: (d, c), digit, c_low)

    def radix16(c_low, early_exit):
        digit = jnp.zeros((1, tq), I32)
        if not early_exit:
            return radix_bits(digit, c_low, range(15, -1, -1))
        digit, c_low = radix_bits(digit, c_low, range(15, 7, -1))
        digit, c_low = radix_bits_if_open(digit, c_low, range(7, 3, -1))
        return radix_bits_if_open(digit, c_low, range(3, -1, -1))

    q_in_blk = lax.broadcasted_iota(I32, (1, tq), 1)
    n_adm = (i * tq + q_in_blk) // CHUNK * CHUNK + CHUNK
    take_all = n_adm <= n_sel

    hi_digit, c_low = radix16(jnp.full((1, tq), 1, I32) * (n_tiles * tk), early_exit=False)
    hi16 = hi_digit - HALF16

    def low_digit_tile(t):
        key = key_scr[tile_rows(t), :]
        hi = key >> 16
        low = (key & 0xFFFF) - HALF16
        k16_scr[tile_rows(t), :] = jnp.where(hi > hi16, HALF16 - 1, jnp.where(hi < hi16, -HALF16, low)).astype(I16)

    for_tiles(n_tiles, low_digit_tile)
    lo_digit, c_low = radix16(c_low, early_exit=True)
    thr = jnp.where(take_all, KEY_NEG_INF + 1, jnp.left_shift(hi16, 16) | lo_digit)
    excess = jnp.where(take_all, 0, c_low - n_sel)
    any_excess = jnp.max(excess) > 0

    neg_shift = -bound_ref[0:1, :]
    @pl.when(jnp.logical_not(any_excess))
    def _():
        def mask_tile(t):
            mb_scr[tile_rows(t), :] = jnp.where(key_scr[tile_rows(t), :] >= thr, neg_shift, MASK_NEG)

        for_tiles(n_tiles, mask_tile)

    @pl.when(any_excess)
    def _():
        def gt_body(t, c):
            hit = jnp.where(key_scr[tile_rows(t), :] > thr, 1, 0).astype(I32)
            return c + hit.reshape(tk // 8, 8, tq).sum(axis=0)

        n_gt = jnp.sum(lax.fori_loop(0, n_tiles, gt_body, jnp.zeros((8, tq), I32)), axis=0, keepdims=True)
        need = jnp.where(take_all, 2 * tk * n_tiles, n_sel - n_gt).astype(F32)
        r_i = lax.broadcasted_iota(I32, (tk, tk), 0)
        c_i = lax.broadcasted_iota(I32, (tk, tk), 1)
        strict_lower = jnp.where(c_i < r_i, 1.0, 0.0).astype(BF16)

        def mask_body(t, ties_before):
            kt = key_scr[tile_rows(t), :]
            eq = kt == thr
            eq_f = jnp.where(eq, 1.0, 0.0)
            rank = jnp.dot(strict_lower, eq_f.astype(BF16), preferred_element_type=F32) + ties_before
            sel = jnp.logical_or(kt > thr, jnp.logical_and(eq, rank < need))
            mb_scr[tile_rows(t), :] = jnp.where(sel, neg_shift, MASK_NEG)
            return ties_before + jnp.sum(eq_f.reshape(tk // 8, 8, tq).sum(axis=0), axis=0, keepdims=True)

        lax.fori_loop(0, n_tiles, mask_body, jnp.zeros((1, tq), F32))

    lane = lax.broadcasted_iota(I32, (tq, 2 * HEAD_DIM), 1)
    for h in range(N_HEADS):
        q_pair = q_ref[:, _pair_lanes(h)]
        own = (lane < HEAD_DIM) if h % 2 == 0 else (lane >= HEAD_DIM)
        qh_scr[h] = jnp.where(own, q_pair, jnp.zeros_like(q_pair))

    def masked_logits(t, h, near_which):
        lg = lax.dot_general(k_ref[tile_rows(t), _pair_lanes(h)], qh_scr[h], NT_DIMS,
                             preferred_element_type=F32) + mb_scr[tile_rows(t), :]
        if near_which is not None:
            lg = lg + near_ref[h, near_which]
        return lg

    def over_tiles(tile_fn):
        for_tiles(jnp.maximum(i - 1, 0), lambda t: tile_fn(t, None))

        @pl.when(i >= 1)
        def _():
            tile_fn(i - 1, 1)

        tile_fn(i, 0)

    def pv_tile_against(shift_row):
        def tile_fn(t, near_which):
            for h in range(N_HEADS):
                lg = masked_logits(t, h, near_which)
                if shift_row is not None:
                    lg = lg - shift_row(h)
                p_scr[h] = jnp.exp2(lg).astype(BF16)
            for h in range(N_HEADS):
                vrows = slice(h * V_ROWS, (h + 1) * V_ROWS)
                acc_scr[vrows, :] += jnp.dot(vt_ref[vrows, tile_rows(t)], p_scr[h], preferred_element_type=F32)
        return tile_fn

    den_row = lambda h: acc_scr[h * V_ROWS + HEAD_DIM:h * V_ROWS + HEAD_DIM + 1, :]

    acc_scr[...] = jnp.zeros(acc_scr.shape, F32)
    over_tiles(pv_tile_against(None))
    den_min = den_row(0)
    for h in range(1, N_HEADS):
        den_min = jnp.minimum(den_min, den_row(h))
    bound_too_loose = jnp.logical_not(jnp.min(den_min) >= MIN_SAFE_DEN)

    @pl.when(bound_too_loose)
    def _():
        m_scr[...] = jnp.full(m_scr.shape, MASK_NEG, F32)

        def max_tile(t, near_which):
            for h in range(N_HEADS):
                part = masked_logits(t, h, near_which).reshape(tk // 8, 8, tq).max(axis=0)
                m_scr[h * 8:(h + 1) * 8, :] = jnp.maximum(m_scr[h * 8:(h + 1) * 8, :], part)

        over_tiles(max_tile)
        for h in range(N_HEADS):
            m_scr[h * 8:(h + 1) * 8, :] = jnp.broadcast_to(
                jnp.max(m_scr[h * 8:(h + 1) * 8, :], axis=0, keepdims=True), (8, tq))
        acc_scr[...] = jnp.zeros(acc_scr.shape, F32)
        over_tiles(pv_tile_against(lambda h: m_scr[h * 8:h * 8 + 1, :]))

    for h in range(N_HEADS):
        out_scr[h * HEAD_DIM:(h + 1) * HEAD_DIM, :] = acc_scr[h * V_ROWS:h * V_ROWS + HEAD_DIM, :] / den_row(h)
    o_ref[...] = out_scr[...].T


def _logit_bound(stats, near, bsz, tq):
    sq = jnp.max(stats.reshape(bsz, -1, *stats.shape[1:]), axis=1)[:, :2, :N_HEADS]
    norms = jnp.sqrt(sq)
    near_max = jnp.maximum(jnp.max(near, axis=(1, 2, 3)), 0.0)
    bound = jnp.max(norms[:, 0] * norms[:, 1] * BOUND_SLACK + near_max + 1.0, axis=1)
    return jnp.broadcast_to(bound[:, None, None], (bsz, 8, tq))


def _dsa(q, k, kidx, vt, qit, kwt, stats, near, bsz, seq):
    d_attn = q.shape[1]
    tq, tk = DSA_TQ, DSA_TK
    assert tq == tk and seq % tq == 0 and tq % CHUNK == 0 and IDX_DIM % IDX_HEADS == 0
    n_sel = min(TOPK_MAX, seq // 4)
    nq = seq // tq
    bound = _logit_bound(stats, near, bsz, tq)
    once = dict(pipeline_mode=pl.Buffered(1))
    return pl.pallas_call(
        functools.partial(_dsa_kernel, tq=tq, tk=tk, n_sel=n_sel),
        out_shape=jax.ShapeDtypeStruct((bsz * seq, d_attn), F32),
        grid=(bsz, nq),
        in_specs=[pl.BlockSpec((IDX_HEADS * IDX_DIM, tq), lambda b, i: (0, b * nq + i)),
                  pl.BlockSpec((IDX_HEADS, tq), lambda b, i: (0, b * nq + i)),
                  pl.BlockSpec((tq, d_attn), lambda b, i: (b * nq + i, 0)),
                  pl.BlockSpec((None, 8, tq), lambda b, i: (b, 0, 0)),
                  pl.BlockSpec((seq, IDX_DIM), lambda b, i: (b, 0), **once),
                  pl.BlockSpec((seq, d_attn), lambda b, i: (b, 0), **once),
                  pl.BlockSpec((N_HEADS * V_ROWS, seq), lambda b, i: (0, b), **once),
                  _resident(near.shape)],
        out_specs=pl.BlockSpec((tq, d_attn), lambda b, i: (b * nq + i, 0)),
        scratch_shapes=[pltpu.VMEM((seq, tq), I32), pltpu.VMEM((seq + (COUNT_TILES - 1) * tk, tq), I16), pltpu.VMEM((seq, tq), F32),
                        pltpu.VMEM((N_HEADS, tq, 2 * HEAD_DIM), BF16), pltpu.VMEM((N_HEADS * 8, tq), F32),
                        pltpu.VMEM((N_HEADS, tk, tq), BF16),
                        pltpu.VMEM((N_HEADS * V_ROWS, tq), F32), pltpu.VMEM((d_attn, tq), F32)],
        compiler_params=pltpu.CompilerParams(dimension_semantics=("parallel", "arbitrary"),
                                             vmem_limit_bytes=V7X_VMEM_LIMIT_BYTES),
        name="dsa",
    )(qit, kwt, q, bound, kidx, k, vt, near)


def _merge_ln_kernel(x_ref, ys_ref, ya_ref, gs_ref, ga_ref, wglu_ref, wb0_ref, wb1_ref, wout_ref,
                     g_ref, b_ref, o_ref):
    y = jax.nn.gelu(ys_ref[...].T)
    glu = y * jax.nn.sigmoid(jnp.dot(y.astype(BF16), wglu_ref[...], preferred_element_type=F32))
    p_ssm = jnp.dot(glu.astype(BF16), wb0_ref[...], preferred_element_type=F32)
    p_att = jnp.dot(ya_ref[...].astype(BF16), wb1_ref[...], preferred_element_type=F32)
    merged = gs_ref[...].astype(F32) * p_ssm + ga_ref[...].astype(F32) * p_att
    mix = jnp.dot(merged.astype(BF16), wout_ref[...], preferred_element_type=F32)
    o_ref[...] = _layer_norm(ALPHA * x_ref[...] + mix, g_ref[...], b_ref[...])


def _merge_ln(x2, y_ssm_t, y_attn, gs, ga, w_glu, wb0, wb1, w_out, g, b):
    n, d = x2.shape
    tm = min(MERGE_TM, n)
    assert n % tm == 0
    tile = lambda a: pl.BlockSpec((tm, a.shape[1]), lambda i: (i, 0))
    return pl.pallas_call(
        _merge_ln_kernel,
        out_shape=jax.ShapeDtypeStruct((n, d), F32),
        grid=(n // tm,),
        in_specs=[tile(x2), pl.BlockSpec((y_ssm_t.shape[0], tm), lambda i: (0, i)), tile(y_attn), tile(gs), tile(ga),
                  _resident(w_glu.shape), _resident(wb0.shape), _resident(wb1.shape), _resident(w_out.shape),
                  _resident((1, d)), _resident((1, d))],
        out_specs=pl.BlockSpec((tm, d), lambda i: (i, 0)),
        compiler_params=pltpu.CompilerParams(dimension_semantics=("parallel",),
                                             vmem_limit_bytes=V7X_VMEM_LIMIT_BYTES),
        name="merge_ln",
    )(x2, y_ssm_t, y_attn, gs, ga, w_glu, wb0, wb1, w_out, g.reshape(1, d), b.reshape(1, d))


def _pack_cols(parts):
    offs, pos = {}, 0
    for name, w in parts.items():
        offs[name] = (pos, pos + w.shape[1])
        pos += w.shape[1]
    return jnp.concatenate(list(parts.values()), axis=1), offs


def _split_w_in(w_in, d_model, d_ssm, d_attn):
    sizes = (d_ssm, d_attn, d_attn, d_attn, IDX_HEADS * IDX_DIM, IDX_DIM, IDX_HEADS, d_model, d_model)
    assert w_in.shape[1] == sum(sizes)
    edges = np.concatenate([[0], np.cumsum(sizes)])
    u, q, k, v, qi, ki, wi, gs, ga = [w_in[:, int(edges[j]):int(edges[j + 1])] for j in range(len(sizes))]
    zeros = lambda c: jnp.zeros((w_in.shape[0], c), w_in.dtype)
    w_nat, offs = _pack_cols(dict(q=q, k=k, kidx=jnp.concatenate([ki, zeros(LANES - IDX_DIM)], axis=1),
                                  gs=gs, ga=ga))
    kw = jnp.concatenate([wi, zeros(W_ROWS - IDX_HEADS)], axis=1)
    w_t, offs_t = _pack_cols(dict(u=u, v=v, qi=qi, kw=kw))
    return w_nat.astype(BF16), offs, w_t.T.astype(BF16), offs_t


def kernel(x, ffn1_w_up, ffn1_w_down, ln1_g, ln1_b, w_in, ssm_lam_re, ssm_lam_im, ssm_log_dt, ssm_b_re, ssm_b_im, ssm_c_re, ssm_c_im, ssm_d, ssm_w_glu, w_branch, w_out, ln2_g, ln2_b, ffn2_w_up, ffn2_w_down, ln3_g, ln3_b, rel_bias):
    bsz, seq, d_model = x.shape
    d_ssm = ssm_w_glu.shape[1]
    d_attn = w_branch.shape[2]
    x2 = x.reshape(bsz * seq, d_model)
    near = _near_bias_tables(rel_bias, DSA_TQ, DSA_TK)
    for l in range(ffn1_w_up.shape[0]):
        x2 = _ffn_ln(x2, ffn1_w_up[l].astype(BF16), ffn1_w_down[l].astype(BF16), ln1_g[l], ln1_b[l])
        q, k, kidx, gs, ga, stats, ut, vt, qit, kwt = _in_proj(x2, *_split_w_in(w_in[l], d_model, d_ssm, d_attn), seq)
        tables = _ssm_tables(ssm_lam_re[l], ssm_lam_im[l], ssm_log_dt[l], ssm_b_re[l], ssm_b_im[l],
                             ssm_c_re[l], ssm_c_im[l], ssm_d[l], SSM_T, seq // SSM_T)
        y_ssm = _ssm(ut, tables, seq)
        y_attn = _dsa(q, k, kidx, vt, qit, kwt, stats, near, bsz, seq)
        x2 = _merge_ln(x2, y_ssm, y_attn, gs, ga, ssm_w_glu[l].astype(BF16), w_branch[l, 0].astype(BF16),
                       w_branch[l, 1].astype(BF16), w_out[l].astype(BF16), ln2_g[l], ln2_b[l])
        x2 = _ffn_ln(x2, ffn2_w_up[l].astype(BF16), ffn2_w_down[l].astype(BF16), ln3_g[l], ln3_b[l])
    return x2.reshape(bsz, seq, d_model)
```

```python
import functools
import math

import numpy as np
import jax
import jax.numpy as jnp
from jax import lax
from jax.experimental import pallas as pl
from jax.experimental.pallas import tpu as pltpu

F32 = jnp.float32
BF16 = jnp.bfloat16
I32 = jnp.int32
I16 = jnp.int16

DEPTH = 2
CHUNK = 64
SSM_GROUP = 16
STATE = 64
N_HEADS = 8
HEAD_DIM = 64
IDX_HEADS = 8
IDX_DIM = 32
TOPK_MAX = 256
ATTN_SCALE = HEAD_DIM ** -0.5
LOG2E = math.log2(math.e)
N_BUCKETS = 32
MAX_DIST = 128
ALPHA = (2 * DEPTH) ** 0.25
LN_EPS = 1e-5

V7X_VMEM_LIMIT_BYTES = 58 * 1024 * 1024
LANES = 128
FFN_TM = 512
FFN_FC = 256
PROJ_TM = 1024
MERGE_TM = 512
SSM_T = 64
DSA_TQ = 256
DSA_TK = 256
COUNT_TILES = 2
MASK_NEG = -1e30
W_ROWS = 16
V_ROWS = HEAD_DIM + 16
KEY_NEG_INF = int(np.int32(np.uint32(0x807FFFFF)))
HALF16 = 1 << 15
MIN_SAFE_DEN = 2.0 ** -80
BOUND_SLACK = 1.02


def _resident(shape):
    nd = len(shape)
    return pl.BlockSpec(shape, lambda *_: (0,) * nd, pipeline_mode=pl.Buffered(1))


def _layer_norm(y, g, b):
    mu = jnp.mean(y, axis=-1, keepdims=True)
    yc = y - mu
    var = jnp.mean(yc * yc, axis=-1, keepdims=True)
    return yc * lax.rsqrt(var + LN_EPS) * g + b


NT_DIMS = (((1,), (1,)), ((), ()))


def _ffn_ln_value(x, wup_ref, wdn_ref, g_ref, b_ref, d_ff, fc):
    xb = x.astype(BF16)
    acc = jnp.zeros(x.shape, F32)
    for j in range(d_ff // fc):
        gate = jnp.dot(xb, wup_ref[:, j * fc:(j + 1) * fc], preferred_element_type=F32)
        up = jnp.dot(xb, wup_ref[:, d_ff + j * fc:d_ff + (j + 1) * fc], preferred_element_type=F32)
        a = (gate * jax.nn.sigmoid(gate)) * up
        acc = acc + jnp.dot(a.astype(BF16), wdn_ref[j * fc:(j + 1) * fc, :], preferred_element_type=F32)
    return _layer_norm(ALPHA * x + 0.5 * acc, g_ref[...], b_ref[...])


def _ffn_ln_kernel(x_ref, wup_ref, wdn_ref, g_ref, b_ref, o_ref, *, d_ff, fc):
    o_ref[...] = _ffn_ln_value(x_ref[...], wup_ref, wdn_ref, g_ref, b_ref, d_ff, fc)


def _ffn_ln(x2, w_up, w_down, g, b):
    n, d = x2.shape
    d_ff = w_down.shape[0]
    tm = min(FFN_TM, n)
    assert n % tm == 0 and d_ff % FFN_FC == 0
    return pl.pallas_call(
        functools.partial(_ffn_ln_kernel, d_ff=d_ff, fc=FFN_FC),
        out_shape=jax.ShapeDtypeStruct((n, d), F32),
        grid=(n // tm,),
        in_specs=[pl.BlockSpec((tm, d), lambda i: (i, 0)),
                  _resident(w_up.shape), _resident(w_down.shape),
                  _resident((1, d)), _resident((1, d))],
        out_specs=pl.BlockSpec((tm, d), lambda i: (i, 0)),
        compiler_params=pltpu.CompilerParams(dimension_semantics=("parallel",),
                                             vmem_limit_bytes=V7X_VMEM_LIMIT_BYTES),
        name="ffn_ln",
    )(x2, w_up, w_down, g.reshape(1, d), b.reshape(1, d))


def _in_proj_kernel(x_ref, w_ref, wt_ref, head_ind_ref, q_ref, k_ref, kidx_ref, gs_ref, ga_ref, stats_ref,
                    ut_ref, vt_ref, qit_ref, kwt_ref, *, offs, offs_t):
    xb = x_ref[...].astype(BF16)

    def proj(name):
        lo, hi = offs[name]
        return jnp.dot(xb, w_ref[:, lo:hi], preferred_element_type=F32)

    def proj_t(name):
        lo, hi = offs_t[name]
        return lax.dot_general(wt_ref[lo:hi, :], xb, NT_DIMS, preferred_element_type=F32)

    ut_ref[...] = proj_t("u")
    q_b = (proj("q") * (ATTN_SCALE * LOG2E)).astype(BF16)
    k_b = proj("k").astype(BF16)
    q_ref[...] = q_b
    k_ref[...] = k_b

    def head_sq_max(v_b):
        per_head = jnp.dot(jnp.square(v_b.astype(F32)).astype(BF16), head_ind_ref[...], preferred_element_type=F32)
        return jnp.max(per_head, axis=0, keepdims=True)

    stats_ref[0] = jnp.concatenate([head_sq_max(q_b), head_sq_max(k_b), jnp.zeros((6, LANES), F32)], axis=0)
    kidx_ref[...] = proj("kidx")[:, :IDX_DIM].astype(BF16)
    gs_ref[...] = jax.nn.sigmoid(proj("gs")).astype(BF16)
    ga_ref[...] = jax.nn.sigmoid(proj("ga")).astype(BF16)
    qit_ref[...] = proj_t("qi").astype(BF16)
    kwt_ref[...] = proj_t("kw")
    v_t = proj_t("v").astype(BF16)
    tm = v_t.shape[1]
    pad_rows = lax.broadcasted_iota(I32, (V_ROWS - HEAD_DIM, tm), 0)
    pad = jnp.where(pad_rows == 0, 1.0, 0.0).astype(BF16)
    for h in range(N_HEADS):
        vt_ref[h * V_ROWS:h * V_ROWS + HEAD_DIM, :] = v_t[h * HEAD_DIM:(h + 1) * HEAD_DIM, :]
        vt_ref[h * V_ROWS + HEAD_DIM:(h + 1) * V_ROWS, :] = pad


def _in_proj(x2, w_nat, offs, w_t, offs_t, seq):
    n, d = x2.shape
    tm = min(PROJ_TM, seq)
    assert seq % tm == 0 and n % seq == 0
    d_attn = offs["q"][1] - offs["q"][0]
    head_ind = (np.arange(d_attn)[:, None] // HEAD_DIM == np.arange(LANES)[None, :]).astype(np.float32)
    width = lambda name: offs[name][1] - offs[name][0]
    rows_t = lambda name: offs_t[name][1] - offs_t[name][0]
    tok = lambda w, dt: (jax.ShapeDtypeStruct((n, w), dt), pl.BlockSpec((tm, w), lambda i: (i, 0)))
    chan = lambda r, dt: (jax.ShapeDtypeStruct((r, n), dt), pl.BlockSpec((r, tm), lambda i: (0, i)))
    outs = [tok(width("q"), BF16), tok(width("k"), BF16), tok(IDX_DIM, BF16),
            tok(width("gs"), BF16), tok(width("ga"), BF16),
            (jax.ShapeDtypeStruct((n // tm, 8, LANES), F32), pl.BlockSpec((1, 8, LANES), lambda i: (i, 0, 0))),
            chan(rows_t("u"), F32), chan(N_HEADS * V_ROWS, BF16), chan(rows_t("qi"), BF16), chan(rows_t("kw"), F32)]
    return pl.pallas_call(
        functools.partial(_in_proj_kernel, offs=offs, offs_t=offs_t),
        out_shape=[o[0] for o in outs],
        grid=(n // tm,),
        in_specs=[pl.BlockSpec((tm, d), lambda i: (i, 0)), _resident(w_nat.shape), _resident(w_t.shape),
                  _resident(head_ind.shape)],
        out_specs=[o[1] for o in outs],
        compiler_params=pltpu.CompilerParams(dimension_semantics=("parallel",),
                                             vmem_limit_bytes=V7X_VMEM_LIMIT_BYTES),
        name="in_proj",
    )(x2, w_nat, w_t, jnp.asarray(head_ind, BF16))


def _ssm_kernel(u_ref, m_ref, w_ref, v_ref, apow_ref, d_ref, y_ref, u_scr, y_scr, m_scr, *, rows_per_seq):
    t_len = u_ref.shape[2] // 2
    n_rows = u_ref.shape[1]
    even, odd = slice(0, n_rows), slice(n_rows, 2 * n_rows)
    for h in range(SSM_GROUP):
        cols = slice(h * t_len, (h + 1) * t_len)
        u_scr[even, cols] = u_ref[h, :, :t_len]
        u_scr[odd, cols] = u_ref[h, :, t_len:]
    u = u_scr[...]
    ub = u.astype(BF16)
    s = jnp.dot(ub, w_ref[0], preferred_element_type=F32)
    s_even, s_odd = s[even], s[odd]
    row = lax.broadcasted_iota(I32, (n_rows, 2 * STATE), 0) % rows_per_seq

    def times_a(val, k):
        a_same = apow_ref[0, 2 * k:2 * k + 1, :]
        a_cross = apow_ref[0, 2 * k + 1:2 * k + 2, :]
        return val * a_same + pltpu.roll(val, STATE, 1) * a_cross

    def shifted(val, by):
        return jnp.where(row >= by, pltpu.roll(val, by, 0), 0.0)

    x = times_a(s_even, 0) + s_odd
    for k in range(1, rows_per_seq.bit_length()):
        x = x + times_a(shifted(x, 1 << (k - 1)), k)
    h_even = shifted(x, 1)
    h_odd = times_a(h_even, 0) + s_even
    h_in = jnp.concatenate([h_even, h_odd], axis=0)
    for k in range(SSM_GROUP):
        for j in range(SSM_GROUP // 2):
            m_scr[k * t_len:(k + 1) * t_len, j * LANES:(j + 1) * LANES] = m_ref[0, k, j]
    y = jnp.dot(ub, m_scr[...], preferred_element_type=F32)
    y = y + jnp.dot(h_in.astype(BF16), v_ref[0], preferred_element_type=F32)
    y_scr[...] = y + d_ref[0] * u
    for h in range(SSM_GROUP):
        cols = slice(h * t_len, (h + 1) * t_len)
        y_ref[h, :, :t_len] = y_scr[even, cols]
        y_ref[h, :, t_len:] = y_scr[odd, cols]


def _ssm_tables(lam_re, lam_im, log_dt, b_re, b_im, c_re, c_im, d_skip, t_len, chunks_per_seq):
    g = lam_re.shape[0]
    lam = lax.complex(jnp.minimum(lam_re.astype(F32), -1e-4), lam_im.astype(F32))
    dt = jnp.exp(log_dt.astype(F32))[:, None]
    a_bar = jnp.exp(lam * dt)
    b_bar = ((a_bar - 1.0) / lam)[:, :, None] * lax.complex(b_re.astype(F32), b_im.astype(F32))
    c_mat = lax.complex(c_re.astype(F32), c_im.astype(F32))
    taus = jnp.arange(t_len + 1, dtype=F32)
    apow = jnp.exp((lam * dt)[:, None, :] * taus[None, :, None])
    hp = lax.Precision.HIGHEST
    kern = jnp.einsum("ghp,gtp,gpk->gkht", c_mat, apow[:, :t_len], b_bar, precision=hp).real
    pos = np.arange(t_len)
    shift = (pos[None, None, :] - pos[None, :, None]) == pos[:, None, None]
    pair_shift = np.einsum("ab,lst->alsbt", np.eye(2, dtype=bool), shift).reshape(2 * t_len, t_len, 2 * t_len)
    kern_pairs = kern.astype(BF16).reshape(g, SSM_GROUP, SSM_GROUP // 2, 2 * t_len)
    m_tab = jnp.einsum("gkjc,csd->gkjsd", kern_pairs, jnp.asarray(pair_shift, BF16), preferred_element_type=BF16)
    w_c = b_bar.transpose(0, 2, 1)[:, :, None, :] * apow[:, :t_len][:, None, ::-1, :]
    w_c = w_c.reshape(g, SSM_GROUP * t_len, STATE)
    w_tab = jnp.concatenate([w_c.real, w_c.imag], axis=-1)
    v_c = c_mat.transpose(0, 2, 1)[:, :, :, None] * apow[:, 1:t_len + 1].transpose(0, 2, 1)[:, :, None, :]
    v_c = v_c.reshape(g, STATE, SSM_GROUP * t_len)
    v_tab = jnp.concatenate([v_c.real, -v_c.imag], axis=1)
    n_steps = chunks_per_seq.bit_length() - 1
    steps = (t_len * (1 << jnp.arange(n_steps))).astype(F32)
    a_step = jnp.exp((lam * dt)[:, None, :] * steps[None, :, None])
    a_rows = jnp.stack([jnp.concatenate([a_step.real, a_step.real], axis=-1),
                        jnp.concatenate([-a_step.imag, a_step.imag], axis=-1)], axis=2)
    a_rows = a_rows.reshape(g, 2 * n_steps, 2 * STATE)
    d_tab = jnp.repeat(d_skip.astype(F32).reshape(g, 1, SSM_GROUP), t_len, axis=2)
    return m_tab, w_tab.astype(BF16), v_tab.astype(BF16), a_rows, d_tab


def _ssm(u_t, tables, seq):
    m_tab, w_tab, v_tab, a_rows, d_tab = tables
    g = m_tab.shape[0]
    d_ssm, n = u_t.shape
    t_len = SSM_T
    rows_per_seq = seq // (2 * t_len)
    assert 2 * t_len == LANES and seq % LANES == 0 and rows_per_seq & (rows_per_seq - 1) == 0
    assert d_ssm == g * SSM_GROUP
    rows, cols = n // LANES, t_len * SSM_GROUP
    per_group = lambda shape: pl.BlockSpec((1,) + shape, lambda i: (i, 0, 0))
    channels = pl.BlockSpec((SSM_GROUP, rows, LANES), lambda i: (i, 0, 0))
    y_t = pl.pallas_call(
        functools.partial(_ssm_kernel, rows_per_seq=rows_per_seq),
        out_shape=jax.ShapeDtypeStruct((d_ssm, rows, LANES), F32),
        grid=(g,),
        in_specs=[channels, pl.BlockSpec((1,) + m_tab.shape[1:], lambda i: (i, 0, 0, 0, 0)),
                  per_group((cols, 2 * STATE)), per_group((2 * STATE, cols)),
                  per_group(a_rows.shape[1:]), per_group((1, cols))],
        out_specs=channels,
        scratch_shapes=[pltpu.VMEM((2 * rows, cols), F32), pltpu.VMEM((2 * rows, cols), F32),
                        pltpu.VMEM((cols, cols), BF16)],
        compiler_params=pltpu.CompilerParams(dimension_semantics=("parallel",),
                                             vmem_limit_bytes=V7X_VMEM_LIMIT_BYTES),
        name="ssm",
    )(u_t.reshape(d_ssm, rows, LANES), m_tab, w_tab, v_tab, a_rows, d_tab)
    return y_t.reshape(d_ssm, n)


def _t5_bucket_np(rel):
    half = N_BUCKETS // 2
    max_exact = half // 2
    ret = np.where(rel > 0, half, 0)
    n = np.abs(rel)
    n_f = np.maximum(n, 1).astype(np.float32)
    large = max_exact + (np.log(n_f / np.float32(max_exact)) / np.float32(math.log(MAX_DIST / max_exact))
                         * (half - max_exact)).astype(np.int32)
    large = np.minimum(large, half - 1)
    return ret + np.where(n < max_exact, n, large)


def _near_bias_tables(rel_bias, tq, tk):
    kk = np.arange(tk)[:, None]
    qq = np.arange(tq)[None, :]
    b_diag = _t5_bucket_np(kk - qq)
    b_prev = _t5_bucket_np(kk - qq - tk)
    far = int(_t5_bucket_np(np.array(-(tk + 1))))
    assert far == int(_t5_bucket_np(np.array(-(1 << 24)))), "far keys must share one bucket"
    tab = rel_bias.astype(F32)
    one_hot = jax.nn.one_hot(np.stack([b_diag, b_prev]), N_BUCKETS, dtype=F32)
    near = jnp.einsum("wkqb,bh->hwkq", one_hot, tab - tab[far][None, :], precision=lax.Precision.HIGHEST)
    return near * LOG2E


def _sortable(x):
    b = pltpu.bitcast(x, I32)
    return b ^ ((b >> 31) & 0x7FFFFFFF)


def _pair_lanes(h):
    return slice((h // 2) * 2 * HEAD_DIM, (h // 2 + 1) * 2 * HEAD_DIM)


def _dsa_kernel(qit_ref, wi_ref, q_ref, bound_ref, kidx_ref, k_ref, vt_ref, near_ref, o_ref,
                key_scr, k16_scr, mb_scr, qh_scr, m_scr, p_scr, acc_scr, out_scr, *, tq, tk, n_sel):
    i = pl.program_id(1)
    n_tiles = i + 1
    tile_rows = lambda t: pl.ds(pl.multiple_of(t * tk, tk), tk)

    def score_tile(t):
        kid = kidx_ref[tile_rows(t), :]
        acc = jnp.zeros((tk, tq), F32)
        for h in range(IDX_HEADS):
            s = jnp.dot(kid, qit_ref[h * IDX_DIM:(h + 1) * IDX_DIM, :], preferred_element_type=F32)
            acc = acc + wi_ref[h:h + 1, :] * jnp.maximum(s, 0.0)
        return acc

    def store_keys(t, score):
        key = _sortable(score)
        key_scr[tile_rows(t), :] = key
        k16_scr[tile_rows(t), :] = (key >> 16).astype(I16)

    def for_tiles(n, tile_fn):
        def pair_body(j, carry):
            tile_fn(2 * j)
            tile_fn(2 * j + 1)
            return carry

        lax.fori_loop(0, n // 2, pair_body, 0)

        @pl.when(n % 2 == 1)
        def _():
            tile_fn(n - 1)

    for_tiles(i, lambda t: store_keys(t, score_tile(t) + 0.0))
    kpos = lax.broadcasted_iota(I32, (tk, tq), 0)
    qpos = lax.broadcasted_iota(I32, (tk, tq), 1)
    admissible = (kpos // CHUNK) <= (qpos // CHUNK)
    store_keys(i, jnp.where(admissible, score_tile(i) + 0.0, -jnp.inf))
    for extra in range(COUNT_TILES - 1):
        k16_scr[tile_rows(n_tiles + extra), :] = jnp.full((tk, tq), -HALF16, I16)

    def count16(cand):
        c16 = cand.astype(I16)

        def group_body(j, c):
            rows = pl.ds(pl.multiple_of(j * (COUNT_TILES * tk), COUNT_TILES * tk), COUNT_TILES * tk)
            hit = jnp.where(k16_scr[rows, :] >= c16, jnp.int16(1), jnp.int16(0))
            parts = [hit[r * 16:(r + 1) * 16] for r in range(COUNT_TILES * tk // 16)]
            while len(parts) > 1:
                parts = [a + b for a, b in zip(parts[0::2], parts[1::2])]
            return c + parts[0]

        n_groups = (n_tiles + COUNT_TILES - 1) // COUNT_TILES
        part = lax.fori_loop(0, n_groups, group_body, jnp.zeros((16, tq), I16))
        return jnp.sum(part.astype(I32), axis=0, keepdims=True)

    def radix_bits(digit, c_low, bits):
        for b in bits:
            cand = digit | (1 << b)
            cnt = count16(cand - HALF16)
            accept = cnt >= n_sel
            digit = jnp.where(accept, cand, digit)
            c_low = jnp.where(accept, cnt, c_low)
        return digit, c_low

    def radix_bits_if_open(digit, c_low, bits):
        still_open = jnp.max(jnp.where(take_all, 0, c_low - n_sel)) > 0
        return lax.cond(still_open, lambda d, c: radix_bits(d, c, bits), lambda d, c: (d, c), digit, c_low)

    def radix16(c_low, early_exit):
        digit = jnp.zeros((1, tq), I32)
        if not early_exit:
            return radix_bits(digit, c_low, range(15, -1, -1))
        digit, c_low = radix_bits(digit, c_low, range(15, 7, -1))
        digit, c_low = radix_bits_if_open(digit, c_low, range(7, 3, -1))
        return radix_bits_if_open(digit, c_low, range(3, -1, -1))

    q_in_blk = lax.broadcasted_iota(I32, (1, tq), 1)
    n_adm = (i * tq + q_in_blk) // CHUNK * CHUNK + CHUNK
    take_all = n_adm <= n_sel

    hi_digit, c_low = radix16(jnp.full((1, tq), 1, I32) * (n_tiles * tk), early_exit=False)
    hi16 = hi_digit - HALF16

    def low_digit_tile(t):
        key = key_scr[tile_rows(t), :]
        hi = key >> 16
        low = (key & 0xFFFF) - HALF16
        k16_scr[tile_rows(t), :] = jnp.where(hi > hi16, HALF16 - 1, jnp.where(hi < hi16, -HALF16, low)).astype(I16)

    for_tiles(n_tiles, low_digit_tile)
    lo_digit, c_low = radix16(c_low, early_exit=True)
    thr = jnp.where(take_all, KEY_NEG_INF + 1, jnp.left_shift(hi16, 16) | lo_digit)
    excess = jnp.where(take_all, 0, c_low - n_sel)
    any_excess = jnp.max(excess) > 0

    neg_shift = -bound_ref[0:1, :]
    @pl.when(jnp.logical_not(any_excess))
    def _():
        def mask_tile(t):
            mb_scr[tile_rows(t), :] = jnp.where(key_scr[tile_rows(t), :] >= thr, neg_shift, MASK_NEG)

        for_tiles(n_tiles, mask_tile)

    @pl.when(any_excess)
    def _():
        def gt_body(t, c):
            hit = jnp.where(key_scr[tile_rows(t), :] > thr, 1, 0).astype(I32)
            return c + hit.reshape(tk // 8, 8, tq).sum(axis=0)

        n_gt = jnp.sum(lax.fori_loop(0, n_tiles, gt_body, jnp.zeros((8, tq), I32)), axis=0, keepdims=True)
        need = jnp.where(take_all, 2 * tk * n_tiles, n_sel - n_gt).astype(F32)
        r_i = lax.broadcasted_iota(I32, (tk, tk), 0)
        c_i = lax.broadcasted_iota(I32, (tk, tk), 1)
        strict_lower = jnp.where(c_i < r_i, 1.0, 0.0).astype(BF16)

        def mask_body(t, ties_before):
            kt = key_scr[tile_rows(t), :]
            eq = kt == thr
            eq_f = jnp.where(eq, 1.0, 0.0)
            rank = jnp.dot(strict_lower, eq_f.astype(BF16), preferred_element_type=F32) + ties_before
            sel = jnp.logical_or(kt > thr, jnp.logical_and(eq, rank < need))
            mb_scr[tile_rows(t), :] = jnp.where(sel, neg_shift, MASK_NEG)
            return ties_before + jnp.sum(eq_f.reshape(tk // 8, 8, tq).sum(axis=0), axis=0, keepdims=True)

        lax.fori_loop(0, n_tiles, mask_body, jnp.zeros((1, tq), F32))

    lane = lax.broadcasted_iota(I32, (tq, 2 * HEAD_DIM), 1)
    for h in range(N_HEADS):
        q_pair = q_ref[:, _pair_lanes(h)]
        own = (lane < HEAD_DIM) if h % 2 == 0 else (lane >= HEAD_DIM)
        qh_scr[h] = jnp.where(own, q_pair, jnp.zeros_like(q_pair))

    def masked_logits(t, h, near_which):
        lg = lax.dot_general(k_ref[tile_rows(t), _pair_lanes(h)], qh_scr[h], NT_DIMS,
                             preferred_element_type=F32) + mb_scr[tile_rows(t), :]
        if near_which is not None:
            lg = lg + near_ref[h, near_which]
        return lg

    def over_tiles(tile_fn):
        for_tiles(jnp.maximum(i - 1, 0), lambda t: tile_fn(t, None))

        @pl.when(i >= 1)
        def _():
            tile_fn(i - 1, 1)

        tile_fn(i, 0)

    def pv_tile_against(shift_row):
        def tile_fn(t, near_which):
            for h in range(N_HEADS):
                lg = masked_logits(t, h, near_which)
                if shift_row is not None:
                    lg = lg - shift_row(h)
                p_scr[h] = jnp.exp2(lg).astype(BF16)
            for h in range(N_HEADS):
                vrows = slice(h * V_ROWS, (h + 1) * V_ROWS)
                acc_scr[vrows, :] += jnp.dot(vt_ref[vrows, tile_rows(t)], p_scr[h], preferred_element_type=F32)
        return tile_fn

    den_row = lambda h: acc_scr[h * V_ROWS + HEAD_DIM:h * V_ROWS + HEAD_DIM + 1, :]

    acc_scr[...] = jnp.zeros(acc_scr.shape, F32)
    over_tiles(pv_tile_against(None))
    den_min = den_row(0)
    for h in range(1, N_HEADS):
        den_min = jnp.minimum(den_min, den_row(h))
    bound_too_loose = jnp.logical_not(jnp.min(den_min) >= MIN_SAFE_DEN)

    @pl.when(bound_too_loose)
    def _():
        m_scr[...] = jnp.full(m_scr.shape, MASK_NEG, F32)

        def max_tile(t, near_which):
            for h in range(N_HEADS):
                part = masked_logits(t, h, near_which).reshape(tk // 8, 8, tq).max(axis=0)
                m_scr[h * 8:(h + 1) * 8, :] = jnp.maximum(m_scr[h * 8:(h + 1) * 8, :], part)

        over_tiles(max_tile)
        for h in range(N_HEADS):
            m_scr[h * 8:(h + 1) * 8, :] = jnp.broadcast_to(
                jnp.max(m_scr[h * 8:(h + 1) * 8, :], axis=0, keepdims=True), (8, tq))
        acc_scr[...] = jnp.zeros(acc_scr.shape, F32)
        over_tiles(pv_tile_against(lambda h: m_scr[h * 8:h * 8 + 1, :]))

    for h in range(N_HEADS):
        out_scr[h * HEAD_DIM:(h + 1) * HEAD_DIM, :] = acc_scr[h * V_ROWS:h * V_ROWS + HEAD_DIM, :] / den_row(h)
    o_ref[...] = out_scr[...].T


def _logit_bound(stats, near, bsz, tq):
    sq = jnp.max(stats.reshape(bsz, -1, *stats.shape[1:]), axis=1)[:, :2, :N_HEADS]
    norms = jnp.sqrt(sq)
    near_max = jnp.maximum(jnp.max(near, axis=(1, 2, 3)), 0.0)
    bound = jnp.max(norms[:, 0] * norms[:, 1] * BOUND_SLACK + near_max + 1.0, axis=1)
    return jnp.broadcast_to(bound[:, None, None], (bsz, 8, tq))


def _dsa(q, k, kidx, vt, qit, kwt, stats, near, bsz, seq):
    d_attn = q.shape[1]
    tq, tk = DSA_TQ, DSA_TK
    assert tq == tk and seq % tq == 0 and tq % CHUNK == 0 and IDX_DIM % IDX_HEADS == 0
    n_sel = min(TOPK_MAX, seq // 4)
    nq = seq // tq
    bound = _logit_bound(stats, near, bsz, tq)
    once = dict(pipeline_mode=pl.Buffered(1))
    return pl.pallas_call(
        functools.partial(_dsa_kernel, tq=tq, tk=tk, n_sel=n_sel),
        out_shape=jax.ShapeDtypeStruct((bsz * seq, d_attn), F32),
        grid=(bsz, nq),
        in_specs=[pl.BlockSpec((IDX_HEADS * IDX_DIM, tq), lambda b, i: (0, b * nq + i)),
                  pl.BlockSpec((IDX_HEADS, tq), lambda b, i: (0, b * nq + i)),
                  pl.BlockSpec((tq, d_attn), lambda b, i: (b * nq + i, 0)),
                  pl.BlockSpec((None, 8, tq), lambda b, i: (b, 0, 0)),
                  pl.BlockSpec((seq, IDX_DIM), lambda b, i: (b, 0), **once),
                  pl.BlockSpec((seq, d_attn), lambda b, i: (b, 0), **once),
                  pl.BlockSpec((N_HEADS * V_ROWS, seq), lambda b, i: (0, b), **once),
                  _resident(near.shape)],
        out_specs=pl.BlockSpec((tq, d_attn), lambda b, i: (b * nq + i, 0)),
        scratch_shapes=[pltpu.VMEM((seq, tq), I32), pltpu.VMEM((seq + (COUNT_TILES - 1) * tk, tq), I16), pltpu.VMEM((seq, tq), F32),
                        pltpu.VMEM((N_HEADS, tq, 2 * HEAD_DIM), BF16), pltpu.VMEM((N_HEADS * 8, tq), F32),
                        pltpu.VMEM((N_HEADS, tk, tq), BF16),
                        pltpu.VMEM((N_HEADS * V_ROWS, tq), F32), pltpu.VMEM((d_attn, tq), F32)],
        compiler_params=pltpu.CompilerParams(dimension_semantics=("parallel", "arbitrary"),
                                             vmem_limit_bytes=V7X_VMEM_LIMIT_BYTES),
        name="dsa",
    )(qit, kwt, q, bound, kidx, k, vt, near)


def _merge_ffn_kernel(x_ref, ys_ref, ya_ref, gs_ref, ga_ref, wglu_ref, wb0_ref, wb1_ref, wout_ref,
                      g_ref, b_ref, wup_ref, wdn_ref, g3_ref, b3_ref, o_ref, *, d_ff, fc):
    y = jax.nn.gelu(ys_ref[...].T)
    glu = y * jax.nn.sigmoid(jnp.dot(y.astype(BF16), wglu_ref[...], preferred_element_type=F32))
    p_ssm = jnp.dot(glu.astype(BF16), wb0_ref[...], preferred_element_type=F32)
    p_att = jnp.dot(ya_ref[...].astype(BF16), wb1_ref[...], preferred_element_type=F32)
    merged = gs_ref[...].astype(F32) * p_ssm + ga_ref[...].astype(F32) * p_att
    mix = jnp.dot(merged.astype(BF16), wout_ref[...], preferred_element_type=F32)
    x_mid = _layer_norm(ALPHA * x_ref[...] + mix, g_ref[...], b_ref[...])
    o_ref[...] = _ffn_ln_value(x_mid, wup_ref, wdn_ref, g3_ref, b3_ref, d_ff, fc)


def _merge_ffn_ln(x2, y_ssm_t, y_attn, gs, ga, w_glu, wb0, wb1, w_out, g, b, w_up, w_down, g3, b3):
    n, d = x2.shape
    d_ff = w_down.shape[0]
    tm = min(MERGE_TM, n)
    assert n % tm == 0 and d_ff % FFN_FC == 0
    tile = lambda a: pl.BlockSpec((tm, a.shape[1]), lambda i: (i, 0))
    return pl.pallas_call(
        functools.partial(_merge_ffn_kernel, d_ff=d_ff, fc=FFN_FC),
        out_shape=jax.ShapeDtypeStruct((n, d), F32),
        grid=(n // tm,),
        in_specs=[tile(x2), pl.BlockSpec((y_ssm_t.shape[0], tm), lambda i: (0, i)), tile(y_attn), tile(gs), tile(ga),
                  _resident(w_glu.shape), _resident(wb0.shape), _resident(wb1.shape), _resident(w_out.shape),
                  _resident((1, d)), _resident((1, d)), _resident(w_up.shape), _resident(w_down.shape),
                  _resident((1, d)), _resident((1, d))],
        out_specs=pl.BlockSpec((tm, d), lambda i: (i, 0)),
        compiler_params=pltpu.CompilerParams(dimension_semantics=("parallel",),
                                             vmem_limit_bytes=V7X_VMEM_LIMIT_BYTES),
        name="merge_ffn_ln",
    )(x2, y_ssm_t, y_attn, gs, ga, w_glu, wb0, wb1, w_out, g.reshape(1, d), b.reshape(1, d),
      w_up, w_down, g3.reshape(1, d), b3.reshape(1, d))


def _pack_cols(parts):
    offs, pos = {}, 0
    for name, w in parts.items():
        offs[name] = (pos, pos + w.shape[1])
        pos += w.shape[1]
    return jnp.concatenate(list(parts.values()), axis=1), offs


def _split_w_in(w_in, d_model, d_ssm, d_attn):
    sizes = (d_ssm, d_attn, d_attn, d_attn, IDX_HEADS * IDX_DIM, IDX_DIM, IDX_HEADS, d_model, d_model)
    assert w_in.shape[1] == sum(sizes)
    edges = np.concatenate([[0], np.cumsum(sizes)])
    u, q, k, v, qi, ki, wi, gs, ga = [w_in[:, int(edges[j]):int(edges[j + 1])] for j in range(len(sizes))]
    zeros = lambda c: jnp.zeros((w_in.shape[0], c), w_in.dtype)
    w_nat, offs = _pack_cols(dict(q=q, k=k, kidx=jnp.concatenate([ki, zeros(LANES - IDX_DIM)], axis=1),
                                  gs=gs, ga=ga))
    kw = jnp.concatenate([wi, zeros(W_ROWS - IDX_HEADS)], axis=1)
    w_t, offs_t = _pack_cols(dict(u=u, v=v, qi=qi, kw=kw))
    return w_nat.astype(BF16), offs, w_t.T.astype(BF16), offs_t


def kernel(x, ffn1_w_up, ffn1_w_down, ln1_g, ln1_b, w_in, ssm_lam_re, ssm_lam_im, ssm_log_dt, ssm_b_re, ssm_b_im, ssm_c_re, ssm_c_im, ssm_d, ssm_w_glu, w_branch, w_out, ln2_g, ln2_b, ffn2_w_up, ffn2_w_down, ln3_g, ln3_b, rel_bias):
    bsz, seq, d_model = x.shape
    d_ssm = ssm_w_glu.shape[1]
    d_attn = w_branch.shape[2]
    x2 = x.reshape(bsz * seq, d_model)
    near = _near_bias_tables(rel_bias, DSA_TQ, DSA_TK)
    for l in range(ffn1_w_up.shape[0]):
        x2 = _ffn_ln(x2, ffn1_w_up[l].astype(BF16), ffn1_w_down[l].astype(BF16), ln1_g[l], ln1_b[l])
        q, k, kidx, gs, ga, stats, ut, vt, qit, kwt = _in_proj(x2, *_split_w_in(w_in[l], d_model, d_ssm, d_attn), seq)
        tables = _ssm_tables(ssm_lam_re[l], ssm_lam_im[l], ssm_log_dt[l], ssm_b_re[l], ssm_b_im[l],
                             ssm_c_re[l], ssm_c_im[l], ssm_d[l], SSM_T, seq // SSM_T)
        y_ssm = _ssm(ut, tables, seq)
        y_attn = _dsa(q, k, kidx, vt, qit, kwt, stats, near, bsz, seq)
        x2 = _merge_ffn_ln(x2, y_ssm, y_attn, gs, ga, ssm_w_glu[l].astype(BF16), w_branch[l, 0].astype(BF16),
                           w_branch[l, 1].astype(BF16), w_out[l].astype(BF16), ln2_g[l], ln2_b[l],
                           ffn2_w_up[l].astype(BF16), ffn2_w_down[l].astype(BF16), ln3_g[l], ln3_b[l])
    return x2.reshape(bsz, seq, d_model)
```
